```python
import jax, jax.numpy as jnp
from jax import lax
import numpy as np

D_MODEL = 4096
BATCH = 2
SEQ = 4096
DEPTH = 2

N_A_LAYERS = DEPTH // 2
N_B_LAYERS = DEPTH - N_A_LAYERS
D_FF = 4 * D_MODEL
EPS = 1e-6
NEG_BIG = -1e30

M_HEADS = 8
M_DV = D_MODEL // M_HEADS
M_DK = M_DV // 2
M_CHUNK = 64
GATE_CAP = 15.0
M_QK = M_HEADS * M_DK
M_SPLITS = (M_QK, 2 * M_QK, 2 * M_QK + D_MODEL, 2 * M_QK + 2 * D_MODEL)
M_IN = 2 * M_QK + 2 * D_MODEL + 2 * M_HEADS

N_HEADS = 32
HEAD_DIM = D_MODEL // N_HEADS
N_KV = 4
HPG = N_HEADS // N_KV
KV_DIM = N_KV * HEAD_DIM
N_BRANCH = 3
CMP_BLOCK = 32
CMP_STRIDE = 16
CMP_HIDDEN = 4 * HEAD_DIM
SEL_BLOCK = 64
N_SEL = 16
WINDOW = 512
Q_BLOCK = 64

kernel_name = "hybrid_mlstm_nsa_yoco"


def rms_norm(x, g):
    xf = x.astype(jnp.float32)
    y = xf * lax.rsqrt(jnp.mean(xf * xf, axis=-1, keepdims=True) + EPS)
    return (y * g.astype(jnp.float32)).astype(x.dtype)


def masked_softmax(s, mask):
    s = jnp.where(mask, s.astype(jnp.float32), NEG_BIG)
    return jax.nn.softmax(s, axis=-1) * mask


def soft_cap(x, cap):
    return cap * jnp.tanh(x / cap)


def squared_relu_mlp(xn, w_up, w_down):
    h = jax.nn.relu(xn @ w_up)
    return (h * h) @ w_down


def mlstm_chunkwise(q, k, v, log_i, log_f):
    B, H, S, DK = q.shape
    DV = v.shape[-1]
    L = M_CHUNK
    NC = S // L

    def to_chunks(a):
        a = a.astype(jnp.float32).reshape(B, H, NC, L, *a.shape[3:])
        return jnp.moveaxis(a, 2, 0)

    xs = (to_chunks(q), to_chunks(k), to_chunks(v), to_chunks(log_i), to_chunks(log_f))
    causal = jnp.tril(jnp.ones((L, L), dtype=bool))

    def step(carry, inp):
        C, n, m = carry
        qc, kc, vc, li, lf = inp
        b = jnp.cumsum(lf, axis=-1)
        log_d = jnp.where(causal, b[..., :, None] - b[..., None, :] + li[..., None, :], -jnp.inf)
        log_inter = b + m[..., None]
        m_t = jnp.maximum(log_inter, jnp.max(log_d, axis=-1))
        w_intra = jnp.exp(log_d - m_t[..., None])
        w_inter = jnp.exp(log_inter - m_t)
        s = jnp.einsum("bhtd,bhsd->bhts", qc, kc) * w_intra
        num = w_inter[..., None] * jnp.einsum("bhtd,bhde->bhte", qc, C) + jnp.einsum("bhts,bhse->bhte", s, vc)
        den = w_inter * jnp.einsum("bhtd,bhd->bht", qc, n) + jnp.sum(s, axis=-1)
        h = num / jnp.maximum(jnp.abs(den), jnp.exp(-m_t))[..., None]
        b_last = b[..., -1]
        log_w = b_last[..., None] - b + li
        m_new = jnp.maximum(b_last + m, jnp.max(log_w, axis=-1))
        wk = jnp.exp(log_w - m_new[..., None])[..., None] * kc
        decay = jnp.exp(b_last + m - m_new)
        C = decay[..., None, None] * C + jnp.einsum("bhsd,bhse->bhde", wk, vc)
        n = decay[..., None] * n + jnp.sum(wk, axis=2)
        return (C, n, m_new), h

    init = (jnp.zeros((B, H, DK, DV), jnp.float32),
            jnp.zeros((B, H, DK), jnp.float32),
            jnp.zeros((B, H), jnp.float32))
    _, h = lax.scan(step, init, xs)
    return jnp.moveaxis(h, 0, 2).reshape(B, H, S, DV)


def mlstm_mixer(xn, w_in, b_gate, head_g, w_out):
    B, S, _ = xn.shape
    proj = xn @ w_in
    q, k, v, o, g = jnp.split(proj, M_SPLITS, axis=-1)
    q = q.reshape(B, S, M_HEADS, M_DK).transpose(0, 2, 1, 3)
    k = k.reshape(B, S, M_HEADS, M_DK).transpose(0, 2, 1, 3) * (M_DK ** -0.5)
    v = v.reshape(B, S, M_HEADS, M_DV).transpose(0, 2, 1, 3)
    g = soft_cap(g.astype(jnp.float32) + b_gate.astype(jnp.float32), GATE_CAP)
    log_i = g[..., :M_HEADS].transpose(0, 2, 1)
    log_f = jax.nn.log_sigmoid(g[..., M_HEADS:]).transpose(0, 2, 1)
    h = mlstm_chunkwise(q, k, v, log_i, log_f)
    h = rms_norm(h, head_g.reshape(M_HEADS, 1, M_DV))
    h = h.transpose(0, 2, 1, 3).reshape(B, S, D_MODEL).astype(xn.dtype)
    return (jax.nn.sigmoid(o) * h) @ w_out


def nsa_shared_kv(xs, w_kv, k_norm_g, cmp_pos, cmp_w1, cmp_w2):
    B, S, _ = xs.shape
    kv = (xs @ w_kv).reshape(B, S, 2 * N_BRANCH, N_KV, HEAD_DIM)
    n_cmp = (S - CMP_BLOCK) // CMP_STRIDE + 1
    idx = np.arange(n_cmp)[:, None] * CMP_STRIDE + np.arange(CMP_BLOCK)[None, :]

    def compress(a, j):
        blk = a[:, idx] + cmp_pos[j][None, None, :, None, :]
        blk = blk.transpose(0, 3, 1, 2, 4).reshape(B, N_KV, n_cmp, CMP_BLOCK * HEAD_DIM)
        return jax.nn.gelu(blk @ cmp_w1[j]) @ cmp_w2[j]

    def to_bgsd(a):
        return a.transpose(0, 2, 1, 3)

    k_cmp = rms_norm(compress(kv[:, :, 0], 0), k_norm_g[0])
    v_cmp = compress(kv[:, :, 1], 1)
    k_sel = rms_norm(to_bgsd(kv[:, :, 2]), k_norm_g[1])
    v_sel = to_bgsd(kv[:, :, 3])
    k_win = rms_norm(to_bgsd(kv[:, :, 4]), k_norm_g[2])
    v_win = to_bgsd(kv[:, :, 5])
    return k_cmp, v_cmp, k_sel, v_sel, k_win, v_win


def cmp_to_sel_matrix(n_cmp, n_sb):
    c0 = np.arange(n_cmp)[:, None] * CMP_STRIDE
    s0 = np.arange(n_sb)[None, :] * SEL_BLOCK
    ov = np.minimum(c0 + CMP_BLOCK, s0 + SEL_BLOCK) - np.maximum(c0, s0)
    return jnp.asarray(np.maximum(ov, 0) / CMP_BLOCK, dtype=jnp.float32)


def nsa_mixer(xn, kv, w_qg, q_norm_g, w_out):
    k_cmp, v_cmp, k_sel, v_sel, k_win, v_win = kv
    B, S, _ = xn.shape
    n_cmp = k_cmp.shape[2]
    n_sb = S // SEL_BLOCK
    n_qb = S // Q_BLOCK
    top = min(N_SEL, n_sb)
    scale = HEAD_DIM ** -0.5
    t = jnp.arange(S)

    proj = xn @ w_qg
    q = proj[..., :D_MODEL].reshape(B, S, N_KV, HPG, HEAD_DIM).transpose(0, 2, 3, 1, 4)
    gates = jax.nn.sigmoid(proj[..., D_MODEL:].astype(jnp.float32))
    gates = gates.reshape(B, S, N_BRANCH, N_KV, HPG).transpose(2, 0, 3, 4, 1)[..., None]
    q_cmp = rms_norm(q, q_norm_g[0])
    q_sel = rms_norm(q, q_norm_g[1])
    q_win = rms_norm(q, q_norm_g[2])

    s = jnp.einsum("bghtd,bgcd->bghtc", q_cmp, k_cmp) * scale
    cmp_valid = (jnp.arange(n_cmp) * CMP_STRIDE + CMP_BLOCK - 1)[None, :] <= t[:, None]
    p_cmp = masked_softmax(s, cmp_valid)
    o_cmp = jnp.einsum("bghtc,bgcd->bghtd", p_cmp.astype(v_cmp.dtype), v_cmp)

    p_blk = jnp.einsum("bghtc,cj->bgtj", p_cmp, cmp_to_sel_matrix(n_cmp, n_sb))
    blk = jnp.arange(n_sb)[None, :]
    cur = (t // SEL_BLOCK)[:, None]
    forced = (blk == cur) | (blk == 0)
    p_blk = jnp.where(forced, jnp.inf, jnp.where(blk <= cur, p_blk, -jnp.inf))
    _, sel_idx = lax.top_k(p_blk, top)

    k_blocks = k_sel.reshape(B, N_KV, n_sb, SEL_BLOCK, HEAD_DIM)
    v_blocks = v_sel.reshape(B, N_KV, n_sb, SEL_BLOCK, HEAD_DIM)
    k_pad = jnp.pad(k_win, ((0, 0), (0, 0), (WINDOW, 0), (0, 0)))
    v_pad = jnp.pad(v_win, ((0, 0), (0, 0), (WINDOW, 0), (0, 0)))
    gather = jax.vmap(jax.vmap(lambda blocks, ix: blocks[ix].reshape(ix.shape[0], -1, HEAD_DIM)))

    def split_q(a):
        return jnp.moveaxis(a.reshape(B, N_KV, HPG, n_qb, Q_BLOCK, HEAD_DIM), 3, 0)

    xs = (split_q(q_sel), split_q(q_win),
          jnp.moveaxis(sel_idx.reshape(B, N_KV, n_qb, Q_BLOCK, top), 2, 0),
          jnp.arange(n_qb) * Q_BLOCK)

    def block_fn(inp):
        qs, qw, ix, start = inp
        tq = start + jnp.arange(Q_BLOCK)
        ks = gather(k_blocks, ix)
        vs = gather(v_blocks, ix)
        kpos = (ix[..., None] * SEL_BLOCK + jnp.arange(SEL_BLOCK)).reshape(B, N_KV, Q_BLOCK, -1)
        s_sel = jnp.einsum("bghqd,bgqkd->bghqk", qs, ks) * scale
        p_sel = masked_softmax(s_sel, (kpos <= tq[:, None])[:, :, None])
        o_sel = jnp.einsum("bghqk,bgqkd->bghqd", p_sel.astype(vs.dtype), vs)
        kw = lax.dynamic_slice_in_dim(k_pad, start, WINDOW + Q_BLOCK, axis=2)
        vw = lax.dynamic_slice_in_dim(v_pad, start, WINDOW + Q_BLOCK, axis=2)
        wpos = start - WINDOW + jnp.arange(WINDOW + Q_BLOCK)
        diff = tq[:, None] - wpos[None, :]
        wmask = (diff >= 0) & (diff < WINDOW) & (wpos[None, :] >= 0)
        s_win = jnp.einsum("bghqd,bgkd->bghqk", qw, kw) * scale
        p_win = masked_softmax(s_win, wmask)
        o_win = jnp.einsum("bghqk,bgkd->bghqd", p_win.astype(vw.dtype), vw)
        return o_sel, o_win

    o_sel, o_win = lax.map(block_fn, xs)

    def merge(o):
        return jnp.moveaxis(o, 0, 3).reshape(B, N_KV, HPG, S, HEAD_DIM)

    o = gates[0] * o_cmp + gates[1] * merge(o_sel) + gates[2] * merge(o_win)
    o = o.transpose(0, 3, 1, 2, 4).reshape(B, S, D_MODEL).astype(xn.dtype)
    return o @ w_out


def setup_inputs(seed: int = 0) -> dict:
    key = jax.random.key(seed)
    ks = jax.random.split(key, 24)
    f32 = jnp.float32

    def dense(k, shape, fan_in):
        return jax.random.normal(k, shape, f32) * (fan_in ** -0.5)

    def gain(k, shape):
        return 1.0 + 0.02 * jax.random.normal(k, shape, f32)

    gate_base = jnp.concatenate([jnp.full((M_HEADS,), -2.0, f32), jnp.full((M_HEADS,), 3.0, f32)])
    m_b_gate = gate_base[None, :] + 0.1 * jax.random.normal(ks[4], (N_A_LAYERS, 2 * M_HEADS), f32)
    return {
        "x": jax.random.normal(ks[0], (BATCH, SEQ, D_MODEL), f32),
        "attn_norm_g": gain(ks[1], (DEPTH, D_MODEL)),
        "mlp_norm_g": gain(ks[2], (DEPTH, D_MODEL)),
        "m_w_in": dense(ks[3], (N_A_LAYERS, D_MODEL, M_IN), D_MODEL),
        "m_b_gate": m_b_gate,
        "m_head_g": gain(ks[5], (N_A_LAYERS, D_MODEL)),
        "m_w_out": dense(ks[6], (N_A_LAYERS, D_MODEL, D_MODEL), D_MODEL),
        "kv_norm_g": gain(ks[7], (D_MODEL,)),
        "w_kv": dense(ks[8], (D_MODEL, 2 * N_BRANCH * KV_DIM), D_MODEL),
        "k_norm_g": gain(ks[9], (N_BRANCH, HEAD_DIM)),
        "cmp_pos": 0.1 * jax.random.normal(ks[10], (2, CMP_BLOCK, HEAD_DIM), f32),
        "cmp_w1": dense(ks[11], (2, CMP_BLOCK * HEAD_DIM, CMP_HIDDEN), CMP_BLOCK * HEAD_DIM),
        "cmp_w2": dense(ks[12], (2, CMP_HIDDEN, HEAD_DIM), CMP_HIDDEN),
        "n_w_qg": dense(ks[13], (N_B_LAYERS, D_MODEL, D_MODEL + N_BRANCH * N_HEADS), D_MODEL),
        "q_norm_g": gain(ks[14], (N_B_LAYERS, N_BRANCH, HEAD_DIM)),
        "n_w_out": dense(ks[15], (N_B_LAYERS, D_MODEL, D_MODEL), D_MODEL),
        "mlp_w_up": dense(ks[16], (DEPTH, D_MODEL, D_FF), D_MODEL),
        "mlp_w_down": dense(ks[17], (DEPTH, D_FF, D_MODEL), D_FF),
    }


def reference(x, attn_norm_g, mlp_norm_g, m_w_in, m_b_gate, m_head_g, m_w_out, kv_norm_g, w_kv,
              k_norm_g, cmp_pos, cmp_w1, cmp_w2, n_w_qg, q_norm_g, n_w_out, mlp_w_up, mlp_w_down):
    shared_kv = None
    for layer in range(DEPTH):
        h = rms_norm(x, attn_norm_g[layer])
        if layer < N_A_LAYERS:
            a = layer
            x = x + mlstm_mixer(h, m_w_in[a], m_b_gate[a], m_head_g[a], m_w_out[a])
        else:
            if shared_kv is None:
                shared_kv = nsa_shared_kv(rms_norm(x, kv_norm_g), w_kv, k_norm_g, cmp_pos, cmp_w1, cmp_w2)
            b = layer - N_A_LAYERS
            x = x + nsa_mixer(h, shared_kv, n_w_qg[b], q_norm_g[b], n_w_out[b])
        x = x + squared_relu_mlp(rms_norm(x, mlp_norm_g[layer]), mlp_w_up[layer], mlp_w_down[layer])
    return x
```

```python
import functools

import numpy as np
import jax
import jax.numpy as jnp
from jax import lax
from jax.experimental import pallas as pl
from jax.experimental.pallas import tpu as pltpu

F32 = jnp.float32
BF16 = jnp.bfloat16

EPS = 1e-6
NEG_BIG = -1e30

M_HEADS = 8
GATE_CAP = 15.0
M_CHUNK = 256

N_HEADS = 32
N_KV = 4
HEAD_DIM = 128
N_BRANCH = 3
CMP_BLOCK = 32
CMP_STRIDE = 16
SEL_BLOCK = 64
N_SEL = 16
WINDOW = 512

LANES = 128
VMEM_LIMIT = 48 * 1024 * 1024


def _params(*sem):
    return pltpu.CompilerParams(dimension_semantics=sem, vmem_limit_bytes=VMEM_LIMIT)


def _dot(a, b):
    return jnp.dot(a, b, preferred_element_type=F32)


def _dot_nt(a, b):
    return lax.dot_general(a, b, (((1,), (1,)), ((), ())), preferred_element_type=F32)


def _split3(x):
    x1 = x.astype(BF16)
    r1 = x - x1.astype(F32)
    x2 = r1.astype(BF16)
    r2 = r1 - x2.astype(F32)
    return x1, x2, r2.astype(BF16)


def _rmsnorm_kernel(x_ref, g_ref, o_ref):
    x = x_ref[...]
    ms = jnp.mean(x * x, axis=-1, keepdims=True)
    o_ref[...] = (x * lax.rsqrt(ms + EPS) * g_ref[...]).astype(o_ref.dtype)


def _rmsnorm(x, g, tm=256):
    T, D = x.shape
    tm = min(tm, T)
    return pl.pallas_call(
        _rmsnorm_kernel,
        grid=(T // tm,),
        in_specs=[pl.BlockSpec((tm, D), lambda i: (i, 0)),
                  pl.BlockSpec((1, D), lambda i: (0, 0))],
        out_specs=pl.BlockSpec((tm, D), lambda i: (i, 0)),
        out_shape=jax.ShapeDtypeStruct((T, D), BF16),
        compiler_params=_params("parallel"),
        name="rmsnorm",
    )(x, g.reshape(1, D).astype(F32))


def _group_rstd(y):
    out = []
    for g in range(y.shape[1] // HEAD_DIM):
        yg = y[:, g * HEAD_DIM:(g + 1) * HEAD_DIM]
        out.append(lax.rsqrt(jnp.mean(yg * yg, axis=-1, keepdims=True) + EPS))
    return out


def _mm_kernel(*refs, epilogue, nk, tn, cfg):
    if epilogue in ("resid", "kvnorm"):
        a_ref, w_ref, e_ref, o_ref = refs[:4]
        rest = refs[4:]
    else:
        a_ref, w_ref, o_ref = refs[:3]
        e_ref = None
        rest = refs[3:]
    acc_ref = rest[0] if nk > 1 else None
    j = pl.program_id(1)
    k = pl.program_id(2)

    part = _dot(a_ref[...], w_ref[...])

    def finish(y):
        if epilogue == "none":
            o_ref[...] = y.astype(o_ref.dtype)
        elif epilogue == "relu2":
            r = jnp.maximum(y, 0.0)
            o_ref[...] = (r * r).astype(o_ref.dtype)
        elif epilogue == "sigmoid":
            o_ref[...] = jax.nn.sigmoid(y).astype(o_ref.dtype)
        elif epilogue == "resid":
            o_ref[...] = (e_ref[...] + y).astype(o_ref.dtype)
        elif epilogue == "headnorm":
            rstd = _group_rstd(y)
            for g, r in enumerate(rstd):
                sl = slice(g * HEAD_DIM, (g + 1) * HEAD_DIM)
                o_ref[:, sl] = (y[:, sl] * r).astype(o_ref.dtype)
        elif epilogue == "kvnorm":
            slot = (j * tn) // cfg["kv_dim"]
            is_norm = jnp.logical_or(slot == 2, slot == 4)
            rstd = _group_rstd(y)
            gain = e_ref[...]
            for g, r in enumerate(rstd):
                sl = slice(g * HEAD_DIM, (g + 1) * HEAD_DIM)
                mult = jnp.where(is_norm, r, 1.0)
                o_ref[:, sl] = (y[:, sl] * mult * gain[:, sl]).astype(o_ref.dtype)
        elif epilogue == "mlstm_in":
            col0 = j * tn
            qk, d = cfg["m_qk"], cfg["d_model"]
            is_k = jnp.logical_and(col0 >= qk, col0 < 2 * qk)
            is_o = col0 >= 2 * qk + d

            @pl.when(is_o)
            def _():
                o_ref[...] = jax.nn.sigmoid(y).astype(o_ref.dtype)

            @pl.when(jnp.logical_not(is_o))
            def _():
                o_ref[...] = (y * jnp.where(is_k, cfg["k_scale"], 1.0)).astype(o_ref.dtype)
        else:
            raise ValueError(epilogue)

    if nk == 1:
        finish(part)
        return

    @pl.when(k == 0)
    def _():
        acc_ref[...] = part

    @pl.when(jnp.logical_and(k > 0, k < nk - 1))
    def _():
        acc_ref[...] += part

    @pl.when(k == nk - 1)
    def _():
        finish(acc_ref[...] + part)


def _matmul(a, w, *, epilogue="none", extra=None, out_dtype=BF16, tm=1024, tn=1024, tk=1024,
            cfg=None, name="matmul"):
    M, K = a.shape
    K2, N = w.shape
    assert K == K2
    tm, tn, tk = min(tm, M), min(tn, N), min(tk, K)
    assert M % tm == 0 and N % tn == 0 and K % tk == 0
    nk = K // tk
    in_specs = [pl.BlockSpec((tm, tk), lambda i, j, k: (i, k)),
                pl.BlockSpec((tk, tn), lambda i, j, k: (k, j))]
    args = [a, w]
    if epilogue == "resid":
        in_specs.append(pl.BlockSpec((tm, tn), lambda i, j, k: (i, j)))
        args.append(extra)
    elif epilogue == "kvnorm":
        in_specs.append(pl.BlockSpec((1, tn), lambda i, j, k: (0, j)))
        args.append(extra)
    scratch = [pltpu.VMEM((tm, tn), F32)] if nk > 1 else []
    return pl.pallas_call(
        functools.partial(_mm_kernel, epilogue=epilogue, nk=nk, tn=tn, cfg=cfg),
        grid=(M // tm, N // tn, nk),
        in_specs=in_specs,
        out_specs=pl.BlockSpec((tm, tn), lambda i, j, k: (i, j)),
        out_shape=jax.ShapeDtypeStruct((M, N), out_dtype),
        scratch_shapes=scratch,
        compiler_params=_params("parallel", "parallel", "arbitrary"),
        name=name,
    )(*args)


def _log_sigmoid(x):
    return jnp.minimum(x, 0.0) - jnp.log(1.0 + jnp.exp(-jnp.abs(x)))


def _gates_kernel(xn_ref, wg_ref, wgt_ref, brow_ref, bcol_ref, gcol_ref, grow_ref):
    xn = xn_ref[...]
    pre_c = _dot(xn, wg_ref[...]) + brow_ref[...]
    g_c = GATE_CAP * jnp.tanh(pre_c / GATE_CAP)
    col = lax.broadcasted_iota(jnp.int32, g_c.shape, 1)
    gcol_ref[...] = jnp.where(col < M_HEADS, g_c, _log_sigmoid(g_c))
    pre_r = _dot_nt(wgt_ref[...], xn) + bcol_ref[...]
    g_r = GATE_CAP * jnp.tanh(pre_r / GATE_CAP)
    row = lax.broadcasted_iota(jnp.int32, g_r.shape, 0)
    grow_ref[...] = jnp.where(row < M_HEADS, g_r, _log_sigmoid(g_r))


def _mlstm_gates(xn, w_g, b_g, tm=512):
    T, D = xn.shape
    ng = 2 * M_HEADS
    tm = min(tm, T)
    wg = jnp.zeros((D, LANES), F32).at[:, :ng].set(w_g).astype(BF16)
    wgt = w_g.T.astype(BF16)
    brow = jnp.zeros((1, LANES), F32).at[0, :ng].set(b_g)
    bcol = b_g.reshape(ng, 1).astype(F32)
    return pl.pallas_call(
        _gates_kernel,
        grid=(T // tm,),
        in_specs=[pl.BlockSpec((tm, D), lambda i: (i, 0)),
                  pl.BlockSpec((D, LANES), lambda i: (0, 0)),
                  pl.BlockSpec((ng, D), lambda i: (0, 0)),
                  pl.BlockSpec((1, LANES), lambda i: (0, 0)),
                  pl.BlockSpec((ng, 1), lambda i: (0, 0))],
        out_specs=[pl.BlockSpec((tm, LANES), lambda i: (i, 0)),
                   pl.BlockSpec((ng, tm), lambda i: (0, i))],
        out_shape=[jax.ShapeDtypeStruct((T, LANES), F32),
                   jax.ShapeDtypeStruct((ng, T), F32)],
        compiler_params=_params("parallel"),
        name="mlstm_gates",
    )(xn, wg, wgt, brow, bcol)


def _mlstm_kernel(q_ref, k_ref, v_ref, o_ref, gc_ref, gr_ref, hg_ref, out_ref,
                  c_ref, n_ref, m_ref, *, dk, dv):
    L = q_ref.shape[0]

    @pl.when(pl.program_id(1) == 0)
    def _():
        c_ref[...] = jnp.zeros_like(c_ref)
        n_ref[...] = jnp.zeros_like(n_ref)
        m_ref[...] = jnp.zeros_like(m_ref)

    row = lax.broadcasted_iota(jnp.int32, (L, L), 0)
    col = lax.broadcasted_iota(jnp.int32, (L, L), 1)
    causal = col <= row
    tril = jnp.where(causal, 1.0, 0.0).astype(BF16)
    triu = jnp.where(row <= col, 1.0, 0.0).astype(BF16)

    gc = gc_ref[...]
    gr = gr_ref[...]
    c1, c2, c3 = _split3(gc)
    b_col_all = _dot(tril, c1) + _dot(tril, c2) + _dot(tril, c3)
    r1, r2, r3 = _split3(gr)
    b_row_all = _dot(r1, triu) + _dot(r2, triu) + _dot(r3, triu)

    for h in range(M_HEADS):
        li_row = gr[h:h + 1, :]
        li_col = gc[:, h:h + 1]
        b_row = b_row_all[M_HEADS + h:M_HEADS + h + 1, :]
        b_col = b_col_all[:, M_HEADS + h:M_HEADS + h + 1]
        m_prev = m_ref[h]

        log_d = jnp.where(causal, b_col - b_row + li_row, -jnp.inf)
        log_inter = b_col + m_prev
        m_t = jnp.maximum(log_inter, jnp.max(log_d, axis=1, keepdims=True))
        w_intra = jnp.exp(log_d - m_t)
        w_inter = jnp.exp(log_inter - m_t)

        qh = q_ref[:, h * dk:(h + 1) * dk]
        kh = k_ref[:, h * dk:(h + 1) * dk]
        vh = v_ref[:, h * dv:(h + 1) * dv]
        c_old = c_ref[h]
        n_old = n_ref[h]

        s = _dot_nt(qh, kh) * w_intra
        num = w_inter * _dot(qh, c_old.astype(BF16)) + _dot(s.astype(BF16), vh)
        qn = jnp.sum(qh.astype(F32) * n_old, axis=1, keepdims=True)
        den = w_inter * qn + jnp.sum(s, axis=1, keepdims=True)
        hval = num / jnp.maximum(jnp.abs(den), jnp.exp(-m_t))

        ms = jnp.mean(hval * hval, axis=1, keepdims=True)
        hn = hval * lax.rsqrt(ms + EPS) * hg_ref[:, h * dv:(h + 1) * dv]
        out_ref[:, h * dv:(h + 1) * dv] = (o_ref[:, h * dv:(h + 1) * dv].astype(F32) * hn).astype(out_ref.dtype)

        b_last = b_col[L - 1:L, :]
        log_w = b_last - b_col + li_col
        m_new = jnp.maximum(b_last + m_prev, jnp.max(log_w, axis=0, keepdims=True))
        wk = jnp.exp(log_w - m_new) * kh.astype(F32)
        decay = jnp.exp(b_last + m_prev - m_new)
        c_ref[h] = decay * c_old + _dot(wk.T.astype(BF16), vh)
        n_ref[h] = decay * n_old + jnp.sum(wk, axis=0, keepdims=True)
        m_ref[h] = m_new


def _mlstm(proj, gcol, grow, head_g, *, B, S, D):
    T = B * S
    dv = D // M_HEADS
    dk = dv // 2
    qk = M_HEADS * dk
    L = min(M_CHUNK, S)
    nc = S // L
    rb = lambda b, c: b * nc + c
    assert D % qk == 0
    return pl.pallas_call(
        functools.partial(_mlstm_kernel, dk=dk, dv=dv),
        grid=(B, nc),
        in_specs=[pl.BlockSpec((L, qk), lambda b, c: (rb(b, c), 0)),
                  pl.BlockSpec((L, qk), lambda b, c: (rb(b, c), 1)),
                  pl.BlockSpec((L, D), lambda b, c: (rb(b, c), (2 * qk) // D)),
                  pl.BlockSpec((L, D), lambda b, c: (rb(b, c), (2 * qk) // D + 1)),
                  pl.BlockSpec((L, LANES), lambda b, c: (rb(b, c), 0)),
                  pl.BlockSpec((2 * M_HEADS, L), lambda b, c: (0, rb(b, c))),
                  pl.BlockSpec((1, D), lambda b, c: (0, 0))],
        out_specs=pl.BlockSpec((L, D), lambda b, c: (rb(b, c), 0)),
        out_shape=jax.ShapeDtypeStruct((T, D), BF16),
        scratch_shapes=[pltpu.VMEM((M_HEADS, dk, dv), F32),
                        pltpu.VMEM((M_HEADS, 1, dk), F32),
                        pltpu.VMEM((M_HEADS, 1, 1), F32)],
        compiler_params=_params("parallel", "arbitrary"),
        name="mlstm",
    )(proj, proj, proj, proj, gcol, grow, head_g.reshape(1, D).astype(F32))


def _compress_kernel(a_ref, pos_ref, w1_ref, w2_ref, gain_ref, o_ref):
    j = pl.program_id(0)
    a = a_ref[...].astype(F32)
    half = a.shape[1]
    lo = (a + pos_ref[0:1, :]).astype(BF16)
    hi = (a + pos_ref[1:2, :]).astype(BF16)
    u = _dot(lo, w1_ref[0:half, :])
    v = _dot(hi, w1_ref[half:2 * half, :])
    ncp = a.shape[0]
    hid = u + pltpu.roll(v, shift=ncp - 1, axis=0)
    cdf = 0.5 * (1.0 + jnp.tanh(np.sqrt(2.0 / np.pi).astype(np.float32) * (hid + 0.044715 * (hid * hid * hid))))
    y = _dot((hid * cdf).astype(BF16), w2_ref[...])
    rstd = lax.rsqrt(jnp.mean(y * y, axis=-1, keepdims=True) + EPS)
    o_ref[...] = (y * jnp.where(j == 0, rstd, 1.0) * gain_ref[...]).astype(o_ref.dtype)


def _compress(a2, pos2, w1, w2, gain):
    _, n, ncp, half = a2.shape
    hid = w1.shape[2]
    return pl.pallas_call(
        _compress_kernel,
        grid=(2, n),
        in_specs=[pl.BlockSpec((None, None, ncp, half), lambda j, i: (j, i, 0, 0)),
                  pl.BlockSpec((None, 2, half), lambda j, i: (j, 0, 0)),
                  pl.BlockSpec((None, 2 * half, hid), lambda j, i: (j, 0, 0)),
                  pl.BlockSpec((None, hid, HEAD_DIM), lambda j, i: (j, 0, 0)),
                  pl.BlockSpec((None, 1, HEAD_DIM), lambda j, i: (j, 0, 0))],
        out_specs=pl.BlockSpec((None, None, ncp, HEAD_DIM), lambda j, i: (j, i, 0, 0)),
        out_shape=jax.ShapeDtypeStruct((2, n, ncp, HEAD_DIM), BF16),
        compiler_params=_params("parallel", "parallel"),
        name="nsa_compress",
    )(a2, pos2, w1, w2, gain)


def _nsa_kernel(q_ref, kc_ref, vc_ref, ks_ref, vs_ref, kw_ref, vw_ref, gate_ref, mt_ref, o_ref,
                out_acc, acc_ref, m_ref, l_ref, sel_ref, *, tq, tk, hpg):
    qi = pl.program_id(2)
    q0 = qi * tq
    t_row = q0 + lax.broadcasted_iota(jnp.int32, (tq, 1), 0)
    hd = HEAD_DIM

    def gate(br, h):
        c = br * hpg + h
        return gate_ref[:, c:c + 1]

    ncp = kc_ref.shape[0]
    c_idx = lax.broadcasted_iota(jnp.int32, (1, ncp), 1)
    cmp_valid = (c_idx * CMP_STRIDE + (CMP_BLOCK - 1)) <= t_row
    kc = kc_ref[...]
    vc = vc_ref[...]
    psum = jnp.zeros((tq, ncp), F32)
    for h in range(hpg):
        qh = q_ref[:, h * hd:(h + 1) * hd]
        s = jnp.where(cmp_valid, _dot_nt(qh, kc), NEG_BIG)
        e = jnp.exp(s - jnp.max(s, axis=1, keepdims=True))
        p = jnp.where(cmp_valid, e / jnp.sum(e, axis=1, keepdims=True), 0.0)
        psum = psum + p
        out_acc[h] = gate(0, h) * _dot(p.astype(BF16), vc)

    mt = mt_ref[...]
    p1, p2, p3 = _split3(psum)
    p_blk = _dot_nt(mt, p1) + _dot_nt(mt, p2) + _dot_nt(mt, p3)
    n_sb = mt.shape[0]
    blk = lax.broadcasted_iota(jnp.int32, (n_sb, 1), 0)
    t_lane = q0 + lax.broadcasted_iota(jnp.int32, (1, tq), 1)
    cur = t_lane // SEL_BLOCK
    forced = jnp.logical_or(blk == cur, blk == 0)
    val = jnp.where(forced, jnp.inf, jnp.where(blk <= cur, p_blk, -jnp.inf))
    cnt = jnp.zeros((n_sb, tq), F32)
    for i in range(n_sb):
        vi = val[i:i + 1, :]
        ge = jnp.where(vi >= val, 1.0, 0.0)
        gt = jnp.where(vi > val, 1.0, 0.0)
        cnt = cnt + jnp.where(blk > i, ge, gt)
    sel_t = jnp.where(cnt < float(min(N_SEL, n_sb)), jnp.where(blk <= cur, 1.0, 0.0), 0.0)
    sel_ref[...] = sel_t.T.astype(BF16)

    def flash_reset():
        m_ref[...] = jnp.full(m_ref.shape, NEG_BIG, F32)
        l_ref[...] = jnp.zeros_like(l_ref)
        acc_ref[...] = jnp.zeros_like(acc_ref)

    def flash_step(k_t, v_t, valid):
        for h in range(hpg):
            qh = q_ref[:, h * hd:(h + 1) * hd]
            s = jnp.where(valid, _dot_nt(qh, k_t), NEG_BIG)
            m_old = m_ref[h]
            m_new = jnp.maximum(m_old, jnp.max(s, axis=1, keepdims=True))
            alpha = jnp.exp(m_old - m_new)
            p = jnp.where(valid, jnp.exp(s - m_new), 0.0)
            l_ref[h] = alpha * l_ref[h] + jnp.sum(p, axis=1, keepdims=True)
            acc_ref[h] = alpha * acc_ref[h] + _dot(p.astype(BF16), v_t)
            m_ref[h] = m_new

    blk_col = lax.broadcasted_iota(jnp.int32, (n_sb, 1), 0)

    def sel_body(kt, carry):
        k0 = pl.multiple_of(kt * tk, tk)
        kpos = k0 + lax.broadcasted_iota(jnp.int32, (1, tk), 1)
        expand = jnp.where((kpos // SEL_BLOCK) == blk_col, 1.0, 0.0).astype(BF16)
        picked = _dot(sel_ref[...], expand)
        valid = jnp.where(kpos <= t_row, picked, 0.0) > 0.5
        flash_step(ks_ref[pl.ds(k0, tk), :], vs_ref[pl.ds(k0, tk), :], valid)
        return carry

    flash_reset()
    lax.fori_loop(0, (q0 + tq) // tk, sel_body, 0)
    for h in range(hpg):
        out_acc[h] = out_acc[h] + gate(1, h) * (acc_ref[h] / l_ref[h])

    def win_body(kt, carry):
        k0 = pl.multiple_of(kt * tk, tk)
        kpos = k0 + lax.broadcasted_iota(jnp.int32, (1, tk), 1)
        d = t_row - kpos
        valid = jnp.where(d >= 0, d, WINDOW) < WINDOW
        flash_step(kw_ref[pl.ds(k0, tk), :], vw_ref[pl.ds(k0, tk), :], valid)
        return carry

    flash_reset()
    lax.fori_loop(jnp.maximum(q0 - WINDOW, 0) // tk, (q0 + tq) // tk, win_body, 0)
    for h in range(hpg):
        o_ref[:, h * hd:(h + 1) * hd] = (out_acc[h] + gate(2, h) * (acc_ref[h] / l_ref[h])).astype(o_ref.dtype)


def _cmp_to_sel_t(ncp, n_cmp, n_sb):
    c0 = np.arange(ncp)[None, :] * CMP_STRIDE
    s0 = np.arange(n_sb)[:, None] * SEL_BLOCK
    ov = np.minimum(c0 + CMP_BLOCK, s0 + SEL_BLOCK) - np.maximum(c0, s0)
    w = np.maximum(ov, 0) / CMP_BLOCK
    w = w * (np.arange(ncp)[None, :] < n_cmp)
    return jnp.asarray(w, dtype=BF16)


def _nsa_attention(qhat, cmp_kv, kv, gates, *, B, S, D, tq=256, tk=256):
    T = B * S
    hpg = N_HEADS // N_KV
    gw = hpg * HEAD_DIM
    tq = min(tq, S)
    tk = min(tk, tq)
    nq = S // tq
    ncp = cmp_kv.shape[2]
    n_cmp = (S - CMP_BLOCK) // CMP_STRIDE + 1
    n_sb = S // SEL_BLOCK
    mt = _cmp_to_sel_t(ncp, n_cmp, n_sb)
    kvspec = lambda slot: pl.BlockSpec((S, HEAD_DIM), lambda b, g, i: (b, slot * N_KV + g))
    return pl.pallas_call(
        functools.partial(_nsa_kernel, tq=tq, tk=tk, hpg=hpg),
        grid=(B, N_KV, nq),
        in_specs=[pl.BlockSpec((tq, gw), lambda b, g, i: (b * nq + i, g)),
                  pl.BlockSpec((None, None, ncp, HEAD_DIM), lambda b, g, i: (0, b * N_KV + g, 0, 0)),
                  pl.BlockSpec((None, None, ncp, HEAD_DIM), lambda b, g, i: (1, b * N_KV + g, 0, 0)),
                  kvspec(2), kvspec(3), kvspec(4), kvspec(5),
                  pl.BlockSpec((tq, LANES), lambda b, g, i: (b * nq + i, g)),
                  pl.BlockSpec((n_sb, ncp), lambda b, g, i: (0, 0))],
        out_specs=pl.BlockSpec((tq, gw), lambda b, g, i: (b * nq + i, g)),
        out_shape=jax.ShapeDtypeStruct((T, D), BF16),
        scratch_shapes=[pltpu.VMEM((hpg, tq, HEAD_DIM), F32),
                        pltpu.VMEM((hpg, tq, HEAD_DIM), F32),
                        pltpu.VMEM((hpg, tq, 1), F32),
                        pltpu.VMEM((hpg, tq, 1), F32),
                        pltpu.VMEM((tq, n_sb), BF16)],
        compiler_params=_params("parallel", "parallel", "parallel"),
        name="nsa_attention",
    )(qhat, cmp_kv, cmp_kv, kv, kv, kv, kv, gates, mt)


def _mlp(x, g, w_up, w_down):
    xn = _rmsnorm(x, g)
    h = _matmul(xn, w_up.astype(BF16), epilogue="relu2", name="mlp_up")
    return _matmul(h, w_down.astype(BF16), epilogue="resid", extra=x, out_dtype=F32, name="mlp_down")


def kernel(x, attn_norm_g, mlp_norm_g, m_w_in, m_b_gate, m_head_g, m_w_out, kv_norm_g, w_kv, k_norm_g,
           cmp_pos, cmp_w1, cmp_w2, n_w_qg, q_norm_g, n_w_out, mlp_w_up, mlp_w_down):
    B, S, D = x.shape
    T = B * S
    xf = x.reshape(T, D)
    dv = D // M_HEADS
    dk = dv // 2
    qk = M_HEADS * dk
    nq = 2 * qk + 2 * D
    kv_dim = N_KV * HEAD_DIM
    hpg = N_HEADS // N_KV
    scale = HEAD_DIM ** -0.5

    xn = _rmsnorm(xf, attn_norm_g[0])
    w_in = m_w_in[0]
    proj = _matmul(xn, w_in[:, :nq].astype(BF16), epilogue="mlstm_in", tn=min(1024, qk),
                   cfg=dict(m_qk=qk, d_model=D, k_scale=float(dk ** -0.5)), name="mlstm_in_proj")
    gcol, grow = _mlstm_gates(xn, w_in[:, nq:], m_b_gate[0])
    hg = _mlstm(proj, gcol, grow, m_head_g[0], B=B, S=S, D=D)
    xf = _matmul(hg, m_w_out[0].astype(BF16), epilogue="resid", extra=xf, out_dtype=F32, name="mlstm_out_proj")
    xf = _mlp(xf, mlp_norm_g[0], mlp_w_up[0], mlp_w_down[0])

    qg = q_norm_g[0]
    ones = jnp.ones((kv_dim,), F32)
    kv_gain = jnp.concatenate([
        ones, ones,
        jnp.tile(k_norm_g[1] * qg[1] * scale, N_KV), ones,
        jnp.tile(k_norm_g[2] * qg[2] * scale, N_KV), ones]).reshape(1, 2 * N_BRANCH * kv_dim)
    xs = _rmsnorm(xf, kv_norm_g)
    kv = _matmul(xs, w_kv.astype(BF16), epilogue="kvnorm", extra=kv_gain, tn=kv_dim,
                 cfg=dict(kv_dim=kv_dim), name="nsa_kv_proj")
    ncp = S // CMP_STRIDE
    a2 = kv[:, :2 * kv_dim].reshape(B, ncp, CMP_STRIDE, 2, N_KV, HEAD_DIM)
    a2 = a2.transpose(3, 0, 4, 1, 2, 5).reshape(2, B * N_KV, ncp, CMP_STRIDE * HEAD_DIM)
    pos2 = cmp_pos.reshape(2, 2, CMP_STRIDE * HEAD_DIM).astype(F32)
    cmp_gain = jnp.stack([k_norm_g[0] * qg[0] * scale, jnp.ones((HEAD_DIM,), F32)]).reshape(2, 1, HEAD_DIM)
    cmp_kv = _compress(a2, pos2, cmp_w1.astype(BF16), cmp_w2.astype(BF16), cmp_gain)

    xn = _rmsnorm(xf, attn_norm_g[1])
    w_qg = n_w_qg[0]
    qhat = _matmul(xn, w_qg[:, :D].astype(BF16), epilogue="headnorm", name="nsa_q_proj")
    w_gate = w_qg[:, D:].reshape(D, N_BRANCH, N_KV, hpg).transpose(0, 2, 1, 3).reshape(D, N_KV, N_BRANCH * hpg)
    w_gate = jnp.zeros((D, N_KV, LANES), F32).at[:, :, :N_BRANCH * hpg].set(w_gate).reshape(D, N_KV * LANES)
    gates = _matmul(xn, w_gate.astype(BF16), epilogue="sigmoid", out_dtype=F32, tn=N_KV * LANES, name="nsa_gate_proj")
    o = _nsa_attention(qhat, cmp_kv, kv, gates, B=B, S=S, D=D)
    xf = _matmul(o, n_w_out[0].astype(BF16), epilogue="resid", extra=xf, out_dtype=F32, name="nsa_out_proj")
    xf = _mlp(xf, mlp_norm_g[1], mlp_w_up[1], mlp_w_down[1])
    return xf.reshape(B, S, D)
```

```python
import functools

import numpy as np
import jax
import jax.numpy as jnp
from jax import lax
from jax.experimental import pallas as pl
from jax.experimental.pallas import tpu as pltpu

F32 = jnp.float32
BF16 = jnp.bfloat16

EPS = 1e-6
NEG_BIG = -1e30
MASK_VAL = -(2.0 ** 100)
LOG2E = 1.4426950408889634

M_HEADS = 8
GATE_CAP = 15.0
M_CHUNK = 256

N_HEADS = 32
N_KV = 4
HEAD_DIM = 128
N_BRANCH = 3
CMP_BLOCK = 32
CMP_STRIDE = 16
SEL_BLOCK = 64
N_SEL = 16
WINDOW = 512

LANES = 128
SUBLANES = 8
VMEM_LIMIT = 56 * 1024 * 1024


def _params(*sem):
    return pltpu.CompilerParams(dimension_semantics=sem, vmem_limit_bytes=VMEM_LIMIT)


def _dot(a, b):
    return jnp.dot(a, b, preferred_element_type=F32)


def _dot_nt(a, b):
    return lax.dot_general(a, b, (((1,), (1,)), ((), ())), preferred_element_type=F32)


def _split3(x):
    x1 = x.astype(BF16)
    r1 = x - x1.astype(F32)
    x2 = r1.astype(BF16)
    r2 = r1 - x2.astype(F32)
    return x1, x2, r2.astype(BF16)


def _rmsnorm_kernel(x_ref, g_ref, o_ref):
    x = x_ref[...]
    ms = jnp.mean(x * x, axis=-1, keepdims=True)
    o_ref[...] = (x * lax.rsqrt(ms + EPS) * g_ref[...]).astype(o_ref.dtype)


def _rmsnorm(x, g, tm=256):
    T, D = x.shape
    tm = min(tm, T)
    return pl.pallas_call(
        _rmsnorm_kernel,
        grid=(T // tm,),
        in_specs=[pl.BlockSpec((tm, D), lambda i: (i, 0)),
                  pl.BlockSpec((1, D), lambda i: (0, 0))],
        out_specs=pl.BlockSpec((tm, D), lambda i: (i, 0)),
        out_shape=jax.ShapeDtypeStruct((T, D), BF16),
        compiler_params=_params("parallel"),
        name="rmsnorm",
    )(x, g.reshape(1, D).astype(F32))


def _group_rstd(y):
    out = []
    for g in range(y.shape[1] // HEAD_DIM):
        yg = y[:, g * HEAD_DIM:(g + 1) * HEAD_DIM]
        out.append(lax.rsqrt(jnp.mean(yg * yg, axis=-1, keepdims=True) + EPS))
    return out


def _mm_kernel(*refs, epilogue, nk, tn, nt, cfg):
    if epilogue in ("resid", "kvnorm"):
        a_ref, w_ref, e_ref, o_ref = refs[:4]
        rest = refs[4:]
    else:
        a_ref, w_ref, o_ref = refs[:3]
        e_ref = None
        rest = refs[3:]
    acc_ref = rest[0] if nk > 1 else None
    j = pl.program_id(1)
    k = pl.program_id(2)

    a = a_ref[...].astype(BF16)
    w = w_ref[...].astype(BF16)
    part = _dot_nt(a, w) if nt else _dot(a, w)

    def finish(y):
        if epilogue == "none":
            o_ref[...] = y.astype(o_ref.dtype)
        elif epilogue == "relu2":
            r = jnp.maximum(y, 0.0)
            o_ref[...] = (r * r).astype(o_ref.dtype)
        elif epilogue == "sigmoid":
            o_ref[...] = jax.nn.sigmoid(y).astype(o_ref.dtype)
        elif epilogue == "resid":
            o_ref[...] = (e_ref[...] + y).astype(o_ref.dtype)
        elif epilogue == "headnorm":
            rstd = _group_rstd(y)
            for g, r in enumerate(rstd):
                sl = slice(g * HEAD_DIM, (g + 1) * HEAD_DIM)
                o_ref[:, sl] = (y[:, sl] * r).astype(o_ref.dtype)
        elif epilogue == "kvnorm":
            slot = (j * tn) // cfg["kv_dim"]
            is_norm = functools.reduce(jnp.logical_or, [slot == s for s in cfg["norm_slots"]])
            rstd = _group_rstd(y)
            gain = e_ref[...]
            for g, r in enumerate(rstd):
                sl = slice(g * HEAD_DIM, (g + 1) * HEAD_DIM)
                mult = jnp.where(is_norm, r, 1.0)
                o_ref[:, sl] = (y[:, sl] * mult * gain[:, sl]).astype(o_ref.dtype)
        elif epilogue == "mlstm_in":
            col0 = j * tn
            qk, d = cfg["m_qk"], cfg["d_model"]
            is_k = jnp.logical_and(col0 >= qk, col0 < 2 * qk)
            is_o = col0 >= 2 * qk + d

            @pl.when(is_o)
            def _():
                o_ref[...] = jax.nn.sigmoid(y).astype(o_ref.dtype)

            @pl.when(jnp.logical_not(is_o))
            def _():
                o_ref[...] = (y * jnp.where(is_k, cfg["k_scale"], 1.0)).astype(o_ref.dtype)
        else:
            raise ValueError(epilogue)

    if nk == 1:
        finish(part)
        return

    @pl.when(k == 0)
    def _():
        acc_ref[...] = part

    @pl.when(jnp.logical_and(k > 0, k < nk - 1))
    def _():
        acc_ref[...] += part

    @pl.when(k == nk - 1)
    def _():
        finish(acc_ref[...] + part)


def _matmul(a, w, *, nt=False, n_out=None, col_map=None, epilogue="none", extra=None, out_dtype=BF16,
            tm=1024, tn=512, tk=4096, cfg=None, name="matmul"):
    M, K = a.shape
    N = n_out if n_out is not None else (w.shape[0] if nt else w.shape[1])
    tm, tn, tk = min(tm, M), min(tn, N), min(tk, K)
    assert M % tm == 0 and N % tn == 0 and K % tk == 0
    nk = K // tk
    cm = col_map if col_map is not None else (lambda j: j)
    if nt:
        w_spec = pl.BlockSpec((tn, tk), lambda i, j, k: (cm(j), k))
    else:
        w_spec = pl.BlockSpec((tk, tn), lambda i, j, k: (k, cm(j)))
    in_specs = [pl.BlockSpec((tm, tk), lambda i, j, k: (i, k)), w_spec]
    args = [a, w]
    if epilogue == "resid":
        in_specs.append(pl.BlockSpec((tm, tn), lambda i, j, k: (i, j)))
        args.append(extra)
    elif epilogue == "kvnorm":
        in_specs.append(pl.BlockSpec((1, tn), lambda i, j, k: (0, j)))
        args.append(extra)
    scratch = [pltpu.VMEM((tm, tn), F32)] if nk > 1 else []
    return pl.pallas_call(
        functools.partial(_mm_kernel, epilogue=epilogue, nk=nk, tn=tn, nt=nt, cfg=cfg),
        grid=(M // tm, N // tn, nk),
        in_specs=in_specs,
        out_specs=pl.BlockSpec((tm, tn), lambda i, j, k: (i, j)),
        out_shape=jax.ShapeDtypeStruct((M, N), out_dtype),
        scratch_shapes=scratch,
        compiler_params=_params("parallel", "parallel", "arbitrary"),
        name=name,
    )(*args)


def _log_sigmoid(x):
    return jnp.minimum(x, 0.0) - jnp.log(1.0 + jnp.exp(-jnp.abs(x)))


def _gates_kernel(xn_ref, wg_ref, wgt_ref, brow_ref, bcol_ref, gcol_ref, grow_ref):
    xn = xn_ref[...]
    pre_c = _dot(xn, wg_ref[...]) + brow_ref[...]
    g_c = GATE_CAP * jnp.tanh(pre_c / GATE_CAP)
    col = lax.broadcasted_iota(jnp.int32, g_c.shape, 1)
    gcol_ref[...] = jnp.where(col < M_HEADS, g_c, _log_sigmoid(g_c))
    pre_r = _dot_nt(wgt_ref[...], xn) + bcol_ref[...]
    g_r = GATE_CAP * jnp.tanh(pre_r / GATE_CAP)
    row = lax.broadcasted_iota(jnp.int32, g_r.shape, 0)
    grow_ref[...] = jnp.where(row < M_HEADS, g_r, _log_sigmoid(g_r))


def _mlstm_gates(xn, w_g, b_g, tm=512):
    T, D = xn.shape
    ng = 2 * M_HEADS
    tm = min(tm, T)
    wg = jnp.zeros((D, LANES), F32).at[:, :ng].set(w_g).astype(BF16)
    wgt = w_g.T.astype(BF16)
    brow = jnp.zeros((1, LANES), F32).at[0, :ng].set(b_g)
    bcol = b_g.reshape(ng, 1).astype(F32)
    return pl.pallas_call(
        _gates_kernel,
        grid=(T // tm,),
        in_specs=[pl.BlockSpec((tm, D), lambda i: (i, 0)),
                  pl.BlockSpec((D, LANES), lambda i: (0, 0)),
                  pl.BlockSpec((ng, D), lambda i: (0, 0)),
                  pl.BlockSpec((1, LANES), lambda i: (0, 0)),
                  pl.BlockSpec((ng, 1), lambda i: (0, 0))],
        out_specs=[pl.BlockSpec((tm, LANES), lambda i: (i, 0)),
                   pl.BlockSpec((ng, tm), lambda i: (0, i))],
        out_shape=[jax.ShapeDtypeStruct((T, LANES), F32),
                   jax.ShapeDtypeStruct((ng, T), F32)],
        compiler_params=_params("parallel"),
        name="mlstm_gates",
    )(xn, wg, wgt, brow, bcol)


def _mlstm_kernel(q_ref, k_ref, v_ref, o_ref, gc_ref, gr_ref, hg_ref, out_ref,
                  c_ref, n_ref, m_ref, *, dk, dv):
    L = q_ref.shape[0]

    @pl.when(pl.program_id(1) == 0)
    def _():
        c_ref[...] = jnp.zeros_like(c_ref)
        n_ref[...] = jnp.zeros_like(n_ref)
        m_ref[...] = jnp.zeros_like(m_ref)

    row = lax.broadcasted_iota(jnp.int32, (L, L), 0)
    col = lax.broadcasted_iota(jnp.int32, (L, L), 1)
    causal = col <= row
    tril = jnp.where(causal, 1.0, 0.0).astype(BF16)
    triu = jnp.where(row <= col, 1.0, 0.0).astype(BF16)

    gc = gc_ref[...]
    gr = gr_ref[...]
    c1, c2, c3 = _split3(gc)
    b_col_all = _dot(tril, c1) + _dot(tril, c2) + _dot(tril, c3)
    r1, r2, r3 = _split3(gr)
    b_row_all = _dot(r1, triu) + _dot(r2, triu) + _dot(r3, triu)

    for h in range(M_HEADS):
        li_row = gr[h:h + 1, :]
        li_col = gc[:, h:h + 1]
        b_row = b_row_all[M_HEADS + h:M_HEADS + h + 1, :]
        b_col = b_col_all[:, M_HEADS + h:M_HEADS + h + 1]
        m_prev = m_ref[h]

        log_d = jnp.where(causal, b_col - b_row + li_row, -jnp.inf)
        log_inter = b_col + m_prev
        m_t = jnp.maximum(log_inter, jnp.max(log_d, axis=1, keepdims=True))
        w_intra = jnp.exp(log_d - m_t)
        w_inter = jnp.exp(log_inter - m_t)

        qh = q_ref[:, h * dk:(h + 1) * dk]
        kh = k_ref[:, h * dk:(h + 1) * dk]
        vh = v_ref[:, h * dv:(h + 1) * dv]
        c_old = c_ref[h]
        n_old = n_ref[h]

        s = _dot_nt(qh, kh) * w_intra
        num = w_inter * _dot(qh, c_old.astype(BF16)) + _dot(s.astype(BF16), vh)
        qn = jnp.sum(qh.astype(F32) * n_old, axis=1, keepdims=True)
        den = w_inter * qn + jnp.sum(s, axis=1, keepdims=True)
        hval = num / jnp.maximum(jnp.abs(den), jnp.exp(-m_t))

        ms = jnp.mean(hval * hval, axis=1, keepdims=True)
        hn = hval * lax.rsqrt(ms + EPS) * hg_ref[:, h * dv:(h + 1) * dv]
        out_ref[:, h * dv:(h + 1) * dv] = (o_ref[:, h * dv:(h + 1) * dv].astype(F32) * hn).astype(out_ref.dtype)

        b_last = b_col[L - 1:L, :]
        log_w = b_last - b_col + li_col
        m_new = jnp.maximum(b_last + m_prev, jnp.max(log_w, axis=0, keepdims=True))
        wk = jnp.exp(log_w - m_new) * kh.astype(F32)
        decay = jnp.exp(b_last + m_prev - m_new)
        c_ref[h] = decay * c_old + _dot(wk.T.astype(BF16), vh)
        n_ref[h] = decay * n_old + jnp.sum(wk, axis=0, keepdims=True)
        m_ref[h] = m_new


def _mlstm(proj, gcol, grow, head_g, *, B, S, D):
    T = B * S
    dv = D // M_HEADS
    dk = dv // 2
    qk = M_HEADS * dk
    L = min(M_CHUNK, S)
    nc = S // L
    rb = lambda b, c: b * nc + c
    assert D % qk == 0
    return pl.pallas_call(
        functools.partial(_mlstm_kernel, dk=dk, dv=dv),
        grid=(B, nc),
        in_specs=[pl.BlockSpec((L, qk), lambda b, c: (rb(b, c), 0)),
                  pl.BlockSpec((L, qk), lambda b, c: (rb(b, c), 1)),
                  pl.BlockSpec((L, D), lambda b, c: (rb(b, c), (2 * qk) // D)),
                  pl.BlockSpec((L, D), lambda b, c: (rb(b, c), (2 * qk) // D + 1)),
                  pl.BlockSpec((L, LANES), lambda b, c: (rb(b, c), 0)),
                  pl.BlockSpec((2 * M_HEADS, L), lambda b, c: (0, rb(b, c))),
                  pl.BlockSpec((1, D), lambda b, c: (0, 0))],
        out_specs=pl.BlockSpec((L, D), lambda b, c: (rb(b, c), 0)),
        out_shape=jax.ShapeDtypeStruct((T, D), BF16),
        scratch_shapes=[pltpu.VMEM((M_HEADS, dk, dv), F32),
                        pltpu.VMEM((M_HEADS, 1, dk), F32),
                        pltpu.VMEM((M_HEADS, 1, 1), F32)],
        compiler_params=_params("parallel", "arbitrary"),
        name="mlstm",
    )(proj, proj, proj, proj, gcol, grow, head_g.reshape(1, D).astype(F32))


def _compress_kernel(a_ref, pos_ref, w1_ref, w2k_ref, w2vt_ref, gain_ref, ok_ref, ovt_ref):
    half = a_ref.shape[2]
    ncp = a_ref.shape[1]

    def hidden(j):
        a = a_ref[j].astype(F32)
        lo = (a + pos_ref[j, 0:1, :]).astype(BF16)
        hi = (a + pos_ref[j, 1:2, :]).astype(BF16)
        u = _dot(lo, w1_ref[j, 0:half, :])
        v = _dot(hi, w1_ref[j, half:2 * half, :])
        hid = u + pltpu.roll(v, shift=ncp - 1, axis=0)
        c0 = float(np.sqrt(2.0 / np.pi))
        cdf = 0.5 * (1.0 + jnp.tanh(c0 * (hid + 0.044715 * (hid * hid * hid))))
        return (hid * cdf).astype(BF16)

    yk = _dot(hidden(0), w2k_ref[...])
    rstd = lax.rsqrt(jnp.mean(yk * yk, axis=-1, keepdims=True) + EPS)
    ok_ref[...] = (yk * rstd * gain_ref[...]).astype(ok_ref.dtype)
    ovt_ref[...] = _dot_nt(w2vt_ref[...], hidden(1)).astype(ovt_ref.dtype)


def _compress(a2, pos2, w1, w2k, w2vt, gain):
    _, n, ncp, half = a2.shape
    hid = w1.shape[2]
    return pl.pallas_call(
        _compress_kernel,
        grid=(n,),
        in_specs=[pl.BlockSpec((2, None, ncp, half), lambda i: (0, i, 0, 0)),
                  pl.BlockSpec((2, 2, half), lambda i: (0, 0, 0)),
                  pl.BlockSpec((2, 2 * half, hid), lambda i: (0, 0, 0)),
                  pl.BlockSpec((hid, HEAD_DIM), lambda i: (0, 0)),
                  pl.BlockSpec((HEAD_DIM, hid), lambda i: (0, 0)),
                  pl.BlockSpec((1, HEAD_DIM), lambda i: (0, 0))],
        out_specs=[pl.BlockSpec((None, ncp, HEAD_DIM), lambda i: (i, 0, 0)),
                   pl.BlockSpec((None, HEAD_DIM, ncp), lambda i: (i, 0, 0))],
        out_shape=[jax.ShapeDtypeStruct((n, ncp, HEAD_DIM), BF16),
                   jax.ShapeDtypeStruct((n, HEAD_DIM, ncp), BF16)],
        compiler_params=_params("parallel"),
        name="nsa_compress",
    )(a2, pos2, w1, w2k, w2vt, gain)


def _nsa_kernel(q_ref, kc_ref, vct_ref, ks_ref, vst_ref, kw_ref, vwt_ref, gate_ref, mt_ref, o_ref,
                qa_ref, out_acc, acc_ref, m_ref, l_ref, s_ref, *, tq, hpg, rank_w):
    qi = pl.program_id(2)
    q0 = pl.multiple_of(qi * tq, tq)
    hd = HEAD_DIM
    t_lane = q0 + lax.broadcasted_iota(jnp.int32, (1, tq), 1)

    def gate(br, h):
        r = br * hpg + h
        return gate_ref[r:r + 1, :]

    def q_head(h):
        return q_ref[:, h * hd:(h + 1) * hd]

    ncp = kc_ref.shape[0]
    c_sub = lax.broadcasted_iota(jnp.int32, (ncp, 1), 0)
    cmp_valid = (c_sub * CMP_STRIDE + (CMP_BLOCK - 1)) <= t_lane
    kc = kc_ref[...]
    vct = vct_ref[...]
    psum = jnp.zeros((ncp, tq), F32)
    for h in range(hpg):
        s = jnp.where(cmp_valid, _dot_nt(kc, q_head(h)), NEG_BIG)
        e = jnp.exp2(s - jnp.max(s, axis=0, keepdims=True))
        p = jnp.where(cmp_valid, e * (1.0 / jnp.sum(e, axis=0, keepdims=True)), 0.0)
        psum = psum + p
        out_acc[h] = gate(0, h) * _dot(vct, p.astype(BF16))

    mt = mt_ref[...]
    n_sb = mt.shape[0]
    p1, p2, p3 = _split3(psum)
    p_blk = _dot(mt, p1) + _dot(mt, p2) + _dot(mt, p3)
    blk = lax.broadcasted_iota(jnp.int32, (n_sb, 1), 0)
    cur = t_lane // SEL_BLOCK
    forced = jnp.logical_or(blk == cur, blk == 0)
    val = jnp.where(forced, jnp.inf, jnp.where(blk <= cur, p_blk, -jnp.inf))

    sub = lax.broadcasted_iota(jnp.int32, (SUBLANES, 1), 0)
    n_rb = n_sb // SUBLANES
    top = float(min(N_SEL, n_sb))
    bias_cols = []
    for c in range(tq // rank_w):
        v = val[:, c * rank_w:(c + 1) * rank_w]
        rows = [v[rb * SUBLANES:(rb + 1) * SUBLANES, :] for rb in range(n_rb)]
        cnt = [jnp.zeros((SUBLANES, rank_w), F32) for _ in range(n_rb)]
        for i in range(n_sb):
            vi = v[i:i + 1, :]
            for rb in range(n_rb):
                lo = rb * SUBLANES
                ge = lambda: jnp.where(vi >= rows[rb], 1.0, 0.0)
                gt = lambda: jnp.where(vi > rows[rb], 1.0, 0.0)
                if lo > i:
                    beats = ge()
                elif lo + SUBLANES - 1 <= i:
                    beats = gt()
                else:
                    beats = jnp.where(sub > (i - lo), ge(), gt())
                cnt[rb] = cnt[rb] + beats
        cnt = jnp.concatenate(cnt, axis=0)
        cur_c = cur[:, c * rank_w:(c + 1) * rank_w]
        picked = jnp.where(cnt < top, jnp.where(blk <= cur_c, 0.0, MASK_VAL), MASK_VAL)
        bias_cols.append(picked)
    sel_bias = jnp.concatenate(bias_cols, axis=1)
    sel_bias = jnp.concatenate([sel_bias, jnp.zeros((LANES - n_sb, tq), F32)], axis=0)
    sb_t = sel_bias.T.astype(BF16)
    for h in range(hpg):
        qa_ref[h, :, 0:hd] = q_head(h)
        qa_ref[h, :, hd:2 * hd] = sb_t

    def flash_reset():
        m_ref[...] = jnp.full(m_ref.shape, NEG_BIG, F32)
        l_ref[...] = jnp.zeros_like(l_ref)
        acc_ref[...] = jnp.zeros_like(acc_ref)

    def flash_step(k_t, vt_t, qsel, mask):
        tile_max = []
        for h in range(hpg):
            s = _dot_nt(k_t, qsel(h))
            if mask is not None:
                s = jnp.where(mask, s, MASK_VAL)
            s_ref[h] = s
            tile_max.append(jnp.max(s, axis=0, keepdims=True))
        for h in range(hpg):
            m_old = m_ref[h]
            m_new = jnp.maximum(m_old, tile_max[h])
            alpha = jnp.exp2(m_old - m_new)
            p = jnp.exp2(s_ref[h] - m_new)
            l_ref[h] = alpha * l_ref[h] + jnp.sum(p, axis=0, keepdims=True)
            acc_ref[h] = alpha * acc_ref[h] + _dot(vt_t, p.astype(BF16))
            m_ref[h] = m_new

    def flash_out(h):
        return acc_ref[h] * (1.0 / l_ref[h])

    tk = tq
    k_sub = lax.broadcasted_iota(jnp.int32, (tk, 1), 0)
    q_lane = lax.broadcasted_iota(jnp.int32, (1, tq), 1)
    causal = k_sub <= q_lane
    lane_j = lax.broadcasted_iota(jnp.int32, (1, LANES), 1)
    key_blk = k_sub // SEL_BLOCK

    def ks_aug(k0):
        onehot = jnp.where(lane_j == key_blk + k0 // SEL_BLOCK, 1.0, 0.0).astype(BF16)
        return jnp.concatenate([ks_ref[pl.ds(k0, tk), :], onehot], axis=1)

    qa_head = lambda h: qa_ref[h]

    flash_reset()

    def sel_body(kt, carry):
        k0 = pl.multiple_of(kt * tk, tk)
        flash_step(ks_aug(k0), vst_ref[:, pl.ds(k0, tk)], qa_head, None)
        return carry

    lax.fori_loop(0, qi, sel_body, 0)
    flash_step(ks_aug(q0), vst_ref[:, pl.ds(q0, tk)], qa_head, causal)
    for h in range(hpg):
        out_acc[h] = out_acc[h] + gate(1, h) * flash_out(h)

    flash_reset()
    flash_step(kw_ref[pl.ds(q0, tk), :], vwt_ref[:, pl.ds(q0, tk)], q_head, causal)
    d = 1
    while d * tq - (tk - 1) < WINDOW:
        all_valid = d * tq + (tq - 1) < WINDOW
        mask = None if all_valid else (q_lane + d * tq - k_sub) < WINDOW

        @pl.when(qi >= d)
        def _(d=d, mask=mask):
            k0 = pl.multiple_of(q0 - d * tq, tk)
            flash_step(kw_ref[pl.ds(k0, tk), :], vwt_ref[:, pl.ds(k0, tk)], q_head, mask)
        d += 1

    for h in range(hpg):
        o = out_acc[h] + gate(2, h) * flash_out(h)
        o_ref[:, h * hd:(h + 1) * hd] = o.T.astype(o_ref.dtype)


def _cmp_to_sel_t(ncp, n_cmp, n_sb):
    c0 = np.arange(ncp)[None, :] * CMP_STRIDE
    s0 = np.arange(n_sb)[:, None] * SEL_BLOCK
    ov = np.minimum(c0 + CMP_BLOCK, s0 + SEL_BLOCK) - np.maximum(c0, s0)
    w = np.maximum(ov, 0) / CMP_BLOCK
    w = w * (np.arange(ncp)[None, :] < n_cmp)
    return jnp.asarray(w, dtype=BF16)


def _nsa_attention(qhat, k_cmp, v_cmp_t, k_tok, v_t, gates_t, *, B, S, D, tq=512):
    T = B * S
    hpg = N_HEADS // N_KV
    gw = hpg * HEAD_DIM
    tq = min(tq, S)
    nq = S // tq
    ncp = k_cmp.shape[1]
    n_cmp = (S - CMP_BLOCK) // CMP_STRIDE + 1
    n_sb = S // SEL_BLOCK
    assert n_sb % SUBLANES == 0 and n_sb <= LANES
    mt = _cmp_to_sel_t(ncp, n_cmp, n_sb)
    grow = gates_t.shape[0] // N_KV
    return pl.pallas_call(
        functools.partial(_nsa_kernel, tq=tq, hpg=hpg, rank_w=min(256, tq)),
        grid=(B, N_KV, nq),
        in_specs=[pl.BlockSpec((tq, gw), lambda b, g, i: (b * nq + i, g)),
                  pl.BlockSpec((None, ncp, HEAD_DIM), lambda b, g, i: (b * N_KV + g, 0, 0)),
                  pl.BlockSpec((None, HEAD_DIM, ncp), lambda b, g, i: (b * N_KV + g, 0, 0)),
                  pl.BlockSpec((S, HEAD_DIM), lambda b, g, i: (b, 2 * N_KV + g)),
                  pl.BlockSpec((HEAD_DIM, S), lambda b, g, i: (g, b)),
                  pl.BlockSpec((S, HEAD_DIM), lambda b, g, i: (b, 3 * N_KV + g)),
                  pl.BlockSpec((HEAD_DIM, S), lambda b, g, i: (N_KV + g, b)),
                  pl.BlockSpec((grow, tq), lambda b, g, i: (g, b * nq + i)),
                  pl.BlockSpec((n_sb, ncp), lambda b, g, i: (0, 0))],
        out_specs=pl.BlockSpec((tq, gw), lambda b, g, i: (b * nq + i, g)),
        out_shape=jax.ShapeDtypeStruct((T, D), BF16),
        scratch_shapes=[pltpu.VMEM((hpg, tq, 2 * HEAD_DIM), BF16),
                        pltpu.VMEM((hpg, HEAD_DIM, tq), F32),
                        pltpu.VMEM((hpg, HEAD_DIM, tq), F32),
                        pltpu.VMEM((hpg, 1, tq), F32),
                        pltpu.VMEM((hpg, 1, tq), F32),
                        pltpu.VMEM((hpg, tq, tq), F32)],
        compiler_params=_params("parallel", "parallel", "parallel"),
        name="nsa_attention",
    )(qhat, k_cmp, v_cmp_t, k_tok, v_t, k_tok, v_t, gates_t, mt)


def _mlp(x, g, w_up, w_down):
    xn = _rmsnorm(x, g)
    h = _matmul(xn, w_up, epilogue="relu2", name="mlp_up")
    return _matmul(h, w_down, epilogue="resid", extra=x, out_dtype=F32, name="mlp_down")


def kernel(x, attn_norm_g, mlp_norm_g, m_w_in, m_b_gate, m_head_g, m_w_out, kv_norm_g, w_kv, k_norm_g,
           cmp_pos, cmp_w1, cmp_w2, n_w_qg, q_norm_g, n_w_out, mlp_w_up, mlp_w_down):
    B, S, D = x.shape
    T = B * S
    xf = x.reshape(T, D)
    dv = D // M_HEADS
    dk = dv // 2
    qk = M_HEADS * dk
    nq = 2 * qk + 2 * D
    kv_dim = N_KV * HEAD_DIM
    hpg = N_HEADS // N_KV
    k_fold = (HEAD_DIM ** -0.5) * LOG2E

    xn = _rmsnorm(xf, attn_norm_g[0])
    w_in = m_w_in[0]
    proj = _matmul(xn, w_in, n_out=nq, epilogue="mlstm_in", tn=min(512, qk),
                   cfg=dict(m_qk=qk, d_model=D, k_scale=float(dk ** -0.5)), name="mlstm_in_proj")
    gcol, grow = _mlstm_gates(xn, w_in[:, nq:], m_b_gate[0])
    hg = _mlstm(proj, gcol, grow, m_head_g[0], B=B, S=S, D=D)
    xf = _matmul(hg, m_w_out[0], epilogue="resid", extra=xf, out_dtype=F32, name="mlstm_out_proj")
    xf = _mlp(xf, mlp_norm_g[0], mlp_w_up[0], mlp_w_down[0])

    qg = q_norm_g[0]
    ones = jnp.ones((kv_dim,), F32)
    kv_gain = jnp.concatenate([
        ones, ones,
        jnp.tile(k_norm_g[1] * qg[1] * k_fold, N_KV),
        jnp.tile(k_norm_g[2] * qg[2] * k_fold, N_KV)]).reshape(1, 4 * kv_dim)
    xs = _rmsnorm(xf, kv_norm_g)
    k_tok = _matmul(xs, w_kv, n_out=4 * kv_dim, col_map=lambda j: j + j // 3, epilogue="kvnorm", extra=kv_gain,
                    tn=kv_dim, cfg=dict(kv_dim=kv_dim, norm_slots=(2, 3)), name="nsa_kv_proj")
    w_vt = jnp.concatenate([w_kv[:, 3 * kv_dim:4 * kv_dim], w_kv[:, 5 * kv_dim:6 * kv_dim]], axis=1).T
    v_t = _matmul(w_vt, xs, nt=True, tm=kv_dim, tn=1024, name="nsa_vt_proj")
    ncp = S // CMP_STRIDE
    a2 = k_tok[:, :2 * kv_dim].reshape(B, ncp, CMP_STRIDE, 2, N_KV, HEAD_DIM)
    a2 = a2.transpose(3, 0, 4, 1, 2, 5).reshape(2, B * N_KV, ncp, CMP_STRIDE * HEAD_DIM)
    pos2 = cmp_pos.reshape(2, 2, CMP_STRIDE * HEAD_DIM).astype(F32)
    cmp_gain = (k_norm_g[0] * qg[0] * k_fold).reshape(1, HEAD_DIM)
    k_cmp, v_cmp_t = _compress(a2, pos2, cmp_w1.astype(BF16), cmp_w2[0].astype(BF16), cmp_w2[1].T.astype(BF16),
                               cmp_gain)

    xn = _rmsnorm(xf, attn_norm_g[1])
    w_qg = n_w_qg[0]
    qhat = _matmul(xn, w_qg, n_out=D, epilogue="headnorm", name="nsa_q_proj")
    grow_n = -(-N_BRANCH * hpg // SUBLANES) * SUBLANES
    w_gate = w_qg[:, D:].reshape(D, N_BRANCH, N_KV, hpg).transpose(2, 1, 3, 0).reshape(N_KV, N_BRANCH * hpg, D)
    w_gate = jnp.zeros((N_KV, grow_n, D), F32).at[:, :N_BRANCH * hpg].set(w_gate).reshape(N_KV * grow_n, D)
    gates_t = _matmul(w_gate, xn, nt=True, epilogue="sigmoid", out_dtype=F32, tn=1024, name="nsa_gate_proj")
    o = _nsa_attention(qhat, k_cmp, v_cmp_t, k_tok, v_t, gates_t, B=B, S=S, D=D)
    xf = _matmul(o, n_w_out[0], epilogue="resid", extra=xf, out_dtype=F32, name="nsa_out_proj")
    xf = _mlp(xf, mlp_norm_g[1], mlp_w_up[1], mlp_w_down[1])
    return xf.reshape(B, S, D)
```

```python
import functools

import numpy as np
import jax
import jax.numpy as jnp
from jax import lax
from jax.experimental import pallas as pl
from jax.experimental.pallas import tpu as pltpu

F32 = jnp.float32
BF16 = jnp.bfloat16

EPS = 1e-6
NEG_BIG = -1e30
MASK_VAL = -(2.0 ** 100)
LOG2E = 1.4426950408889634

M_HEADS = 8
GATE_CAP = 15.0
M_CHUNK = 256

N_HEADS = 32
N_KV = 4
HEAD_DIM = 128
N_BRANCH = 3
CMP_BLOCK = 32
CMP_STRIDE = 16
SEL_BLOCK = 64
N_SEL = 16
WINDOW = 512

LANES = 128
SUBLANES = 8
VMEM_LIMIT = 56 * 1024 * 1024


def _params(*sem):
    return pltpu.CompilerParams(dimension_semantics=sem, vmem_limit_bytes=VMEM_LIMIT)


def _dot(a, b):
    return jnp.dot(a, b, preferred_element_type=F32)


def _dot_nt(a, b):
    return lax.dot_general(a, b, (((1,), (1,)), ((), ())), preferred_element_type=F32)


def _split3(x):
    x1 = x.astype(BF16)
    r1 = x - x1.astype(F32)
    x2 = r1.astype(BF16)
    r2 = r1 - x2.astype(F32)
    return x1, x2, r2.astype(BF16)


def _rmsnorm_kernel(x_ref, g_ref, o_ref):
    x = x_ref[...]
    ms = jnp.mean(x * x, axis=-1, keepdims=True)
    o_ref[...] = (x * lax.rsqrt(ms + EPS) * g_ref[...]).astype(o_ref.dtype)


def _rmsnorm(x, g, tm=256):
    T, D = x.shape
    tm = min(tm, T)
    return pl.pallas_call(
        _rmsnorm_kernel,
        grid=(T // tm,),
        in_specs=[pl.BlockSpec((tm, D), lambda i: (i, 0)),
                  pl.BlockSpec((1, D), lambda i: (0, 0))],
        out_specs=pl.BlockSpec((tm, D), lambda i: (i, 0)),
        out_shape=jax.ShapeDtypeStruct((T, D), BF16),
        compiler_params=_params("parallel"),
        name="rmsnorm",
    )(x, g.reshape(1, D).astype(F32))


def _group_rstd(y):
    out = []
    for g in range(y.shape[1] // HEAD_DIM):
        yg = y[:, g * HEAD_DIM:(g + 1) * HEAD_DIM]
        out.append(lax.rsqrt(jnp.mean(yg * yg, axis=-1, keepdims=True) + EPS))
    return out


def _mm_kernel(*refs, epilogue, nk, tn, nt, cfg):
    if epilogue in ("resid", "kvnorm"):
        a_ref, w_ref, e_ref, o_ref = refs
    else:
        a_ref, w_ref, o_ref = refs
        e_ref = None
    j = pl.program_id(1)
    k = pl.program_id(2)

    a = a_ref[...].astype(BF16)
    w = w_ref[...].astype(BF16)
    part = _dot_nt(a, w) if nt else _dot(a, w)

    if nk > 1:
        assert epilogue == "resid" and o_ref.dtype == F32

        @pl.when(k == 0)
        def _():
            o_ref[...] = e_ref[...]

        o_ref[...] += part
        return

    def finish(y):
        if epilogue == "none":
            o_ref[...] = y.astype(o_ref.dtype)
        elif epilogue == "relu2":
            r = jnp.maximum(y, 0.0)
            o_ref[...] = (r * r).astype(o_ref.dtype)
        elif epilogue == "sigmoid":
            o_ref[...] = jax.nn.sigmoid(y).astype(o_ref.dtype)
        elif epilogue == "resid":
            o_ref[...] = (e_ref[...] + y).astype(o_ref.dtype)
        elif epilogue == "headnorm":
            rstd = _group_rstd(y)
            for g, r in enumerate(rstd):
                sl = slice(g * HEAD_DIM, (g + 1) * HEAD_DIM)
                o_ref[:, sl] = (y[:, sl] * r).astype(o_ref.dtype)
        elif epilogue == "kvnorm":
            slot = (j * tn) // cfg["kv_dim"]
            is_norm = functools.reduce(jnp.logical_or, [slot == s for s in cfg["norm_slots"]])
            rstd = _group_rstd(y)
            gain = e_ref[...]
            for g, r in enumerate(rstd):
                sl = slice(g * HEAD_DIM, (g + 1) * HEAD_DIM)
                mult = jnp.where(is_norm, r, 1.0)
                o_ref[:, sl] = (y[:, sl] * mult * gain[:, sl]).astype(o_ref.dtype)
        elif epilogue == "mlstm_in":
            col0 = j * tn
            qk, d = cfg["m_qk"], cfg["d_model"]
            is_k = jnp.logical_and(col0 >= qk, col0 < 2 * qk)
            is_o = col0 >= 2 * qk + d
            lin = y * jnp.where(is_k, cfg["k_scale"], 1.0)
            o_ref[...] = jnp.where(is_o, jax.nn.sigmoid(y), lin).astype(o_ref.dtype)
        else:
            raise ValueError(epilogue)

    finish(part)


def _matmul(a, w, *, layer=None, nt=False, n_out=None, col_map=None, epilogue="none", extra=None,
            out_dtype=BF16, tm=2048, tn=512, tk=4096, cfg=None, name="matmul"):
    M, K = a.shape
    wshape = w.shape[1:] if layer is not None else w.shape
    N = n_out if n_out is not None else (wshape[0] if nt else wshape[1])
    tm, tn, tk = min(tm, M), min(tn, N), min(tk, K)
    assert M % tm == 0 and N % tn == 0 and K % tk == 0
    nk = K // tk
    cm = col_map if col_map is not None else (lambda j: j)
    if layer is not None:
        assert not nt
        w_spec = pl.BlockSpec((None, tk, tn), lambda i, j, k: (layer, k, cm(j)))
    elif nt:
        w_spec = pl.BlockSpec((tn, tk), lambda i, j, k: (cm(j), k))
    else:
        w_spec = pl.BlockSpec((tk, tn), lambda i, j, k: (k, cm(j)))
    a_mode = dict(pipeline_mode=pl.Buffered(1)) if (nk == 1 and N // tn > 1) else {}
    in_specs = [pl.BlockSpec((tm, tk), lambda i, j, k: (i, k), **a_mode), w_spec]
    args = [a, w]
    if epilogue == "resid":
        in_specs.append(pl.BlockSpec((tm, tn), lambda i, j, k: (i, j)))
        args.append(extra)
    elif epilogue == "kvnorm":
        in_specs.append(pl.BlockSpec((1, tn), lambda i, j, k: (0, j)))
        args.append(extra)
    return pl.pallas_call(
        functools.partial(_mm_kernel, epilogue=epilogue, nk=nk, tn=tn, nt=nt, cfg=cfg),
        grid=(M // tm, N // tn, nk),
        in_specs=in_specs,
        out_specs=pl.BlockSpec((tm, tn), lambda i, j, k: (i, j)),
        out_shape=jax.ShapeDtypeStruct((M, N), out_dtype),
        compiler_params=_params("parallel", "parallel", "arbitrary"),
        name=name,
    )(*args)


def _log_sigmoid(x):
    return jnp.minimum(x, 0.0) - jnp.log(1.0 + jnp.exp(-jnp.abs(x)))


def _gates_kernel(xn_ref, wg_ref, wgt_ref, brow_ref, bcol_ref, gcol_ref, grow_ref):
    xn = xn_ref[...]
    pre_c = _dot(xn, wg_ref[...]) + brow_ref[...]
    g_c = GATE_CAP * jnp.tanh(pre_c / GATE_CAP)
    col = lax.broadcasted_iota(jnp.int32, g_c.shape, 1)
    gcol_ref[...] = jnp.where(col < M_HEADS, g_c, _log_sigmoid(g_c))
    pre_r = _dot_nt(wgt_ref[...], xn) + bcol_ref[...]
    g_r = GATE_CAP * jnp.tanh(pre_r / GATE_CAP)
    row = lax.broadcasted_iota(jnp.int32, g_r.shape, 0)
    grow_ref[...] = jnp.where(row < M_HEADS, g_r, _log_sigmoid(g_r))


def _mlstm_gates(xn, w_g, b_g, tm=512):
    T, D = xn.shape
    ng = 2 * M_HEADS
    tm = min(tm, T)
    wg = jnp.zeros((D, LANES), F32).at[:, :ng].set(w_g).astype(BF16)
    wgt = w_g.T.astype(BF16)
    brow = jnp.zeros((1, LANES), F32).at[0, :ng].set(b_g)
    bcol = b_g.reshape(ng, 1).astype(F32)
    return pl.pallas_call(
        _gates_kernel,
        grid=(T // tm,),
        in_specs=[pl.BlockSpec((tm, D), lambda i: (i, 0)),
                  pl.BlockSpec((D, LANES), lambda i: (0, 0)),
                  pl.BlockSpec((ng, D), lambda i: (0, 0)),
                  pl.BlockSpec((1, LANES), lambda i: (0, 0)),
                  pl.BlockSpec((ng, 1), lambda i: (0, 0))],
        out_specs=[pl.BlockSpec((tm, LANES), lambda i: (i, 0)),
                   pl.BlockSpec((ng, tm), lambda i: (0, i))],
        out_shape=[jax.ShapeDtypeStruct((T, LANES), F32),
                   jax.ShapeDtypeStruct((ng, T), F32)],
        compiler_params=_params("parallel"),
        name="mlstm_gates",
    )(xn, wg, wgt, brow, bcol)


def _mlstm_kernel(q_ref, k_ref, v_ref, o_ref, gc_ref, gr_ref, hg_ref, out_ref,
                  c_ref, n_ref, m_ref, *, dk, dv):
    L = q_ref.shape[0]

    @pl.when(pl.program_id(1) == 0)
    def _():
        c_ref[...] = jnp.zeros_like(c_ref)
        n_ref[...] = jnp.zeros_like(n_ref)
        m_ref[...] = jnp.zeros_like(m_ref)

    row = lax.broadcasted_iota(jnp.int32, (L, L), 0)
    col = lax.broadcasted_iota(jnp.int32, (L, L), 1)
    causal = col <= row
    tril = jnp.where(causal, 1.0, 0.0).astype(BF16)
    triu = jnp.where(row <= col, 1.0, 0.0).astype(BF16)

    gc = gc_ref[...]
    gr = gr_ref[...]
    c1, c2, c3 = _split3(gc)
    b_col_all = _dot(tril, c1) + _dot(tril, c2) + _dot(tril, c3)
    r1, r2, r3 = _split3(gr)
    b_row_all = _dot(r1, triu) + _dot(r2, triu) + _dot(r3, triu)

    for h in range(M_HEADS):
        li_row = gr[h:h + 1, :]
        li_col = gc[:, h:h + 1]
        b_row = b_row_all[M_HEADS + h:M_HEADS + h + 1, :]
        b_col = b_col_all[:, M_HEADS + h:M_HEADS + h + 1]
        m_prev = m_ref[h]

        log_d = jnp.where(causal, b_col - b_row + li_row, -jnp.inf)
        log_inter = b_col + m_prev
        m_t = jnp.maximum(log_inter, jnp.max(log_d, axis=1, keepdims=True))
        w_intra = jnp.exp(log_d - m_t)
        w_inter = jnp.exp(log_inter - m_t)

        qh = q_ref[:, h * dk:(h + 1) * dk]
        kh = k_ref[:, h * dk:(h + 1) * dk]
        vh = v_ref[:, h * dv:(h + 1) * dv]
        c_old = c_ref[h]
        n_old = n_ref[h]

        s = _dot_nt(qh, kh) * w_intra
        num = w_inter * _dot(qh, c_old.astype(BF16)) + _dot(s.astype(BF16), vh)
        qn = jnp.sum(qh.astype(F32) * n_old, axis=1, keepdims=True)
        den = w_inter * qn + jnp.sum(s, axis=1, keepdims=True)
        hval = num / jnp.maximum(jnp.abs(den), jnp.exp(-m_t))

        ms = jnp.mean(hval * hval, axis=1, keepdims=True)
        hn = hval * lax.rsqrt(ms + EPS) * hg_ref[:, h * dv:(h + 1) * dv]
        out_ref[:, h * dv:(h + 1) * dv] = (o_ref[:, h * dv:(h + 1) * dv].astype(F32) * hn).astype(out_ref.dtype)

        b_last = b_col[L - 1:L, :]
        log_w = b_last - b_col + li_col
        m_new = jnp.maximum(b_last + m_prev, jnp.max(log_w, axis=0, keepdims=True))
        wk = jnp.exp(log_w - m_new) * kh.astype(F32)
        decay = jnp.exp(b_last + m_prev - m_new)
        c_ref[h] = decay * c_old + _dot(wk.T.astype(BF16), vh)
        n_ref[h] = decay * n_old + jnp.sum(wk, axis=0, keepdims=True)
        m_ref[h] = m_new


def _mlstm(proj, gcol, grow, head_g, *, B, S, D):
    T = B * S
    dv = D // M_HEADS
    dk = dv // 2
    qk = M_HEADS * dk
    L = min(M_CHUNK, S)
    nc = S // L
    rb = lambda b, c: b * nc + c
    assert D % qk == 0
    return pl.pallas_call(
        functools.partial(_mlstm_kernel, dk=dk, dv=dv),
        grid=(B, nc),
        in_specs=[pl.BlockSpec((L, qk), lambda b, c: (rb(b, c), 0)),
                  pl.BlockSpec((L, qk), lambda b, c: (rb(b, c), 1)),
                  pl.BlockSpec((L, D), lambda b, c: (rb(b, c), (2 * qk) // D)),
                  pl.BlockSpec((L, D), lambda b, c: (rb(b, c), (2 * qk) // D + 1)),
                  pl.BlockSpec((L, LANES), lambda b, c: (rb(b, c), 0)),
                  pl.BlockSpec((2 * M_HEADS, L), lambda b, c: (0, rb(b, c))),
                  pl.BlockSpec((1, D), lambda b, c: (0, 0))],
        out_specs=pl.BlockSpec((L, D), lambda b, c: (rb(b, c), 0)),
        out_shape=jax.ShapeDtypeStruct((T, D), BF16),
        scratch_shapes=[pltpu.VMEM((M_HEADS, dk, dv), F32),
                        pltpu.VMEM((M_HEADS, 1, dk), F32),
                        pltpu.VMEM((M_HEADS, 1, 1), F32)],
        compiler_params=_params("parallel", "arbitrary"),
        name="mlstm",
    )(proj, proj, proj, proj, gcol, grow, head_g.reshape(1, D).astype(F32))


def _compress_kernel(a_ref, pos_ref, w1_ref, w2k_ref, w2vt_ref, gain_ref, ok_ref, ovt_ref):
    half = a_ref.shape[2]
    ncp = a_ref.shape[1]

    def hidden(j):
        a = a_ref[j].astype(F32)
        lo = (a + pos_ref[j, 0:1, :]).astype(BF16)
        hi = (a + pos_ref[j, 1:2, :]).astype(BF16)
        u = _dot(lo, w1_ref[j, 0:half, :])
        v = _dot(hi, w1_ref[j, half:2 * half, :])
        hid = u + pltpu.roll(v, shift=ncp - 1, axis=0)
        c0 = float(np.sqrt(2.0 / np.pi))
        cdf = 0.5 * (1.0 + jnp.tanh(c0 * (hid + 0.044715 * (hid * hid * hid))))
        return (hid * cdf).astype(BF16)

    yk = _dot(hidden(0), w2k_ref[...])
    rstd = lax.rsqrt(jnp.mean(yk * yk, axis=-1, keepdims=True) + EPS)
    ok_ref[...] = (yk * rstd * gain_ref[...]).astype(ok_ref.dtype)
    ovt_ref[...] = _dot_nt(w2vt_ref[...], hidden(1)).astype(ovt_ref.dtype)


def _compress(a2, pos2, w1, w2k, w2vt, gain):
    _, n, ncp, half = a2.shape
    hid = w1.shape[2]
    return pl.pallas_call(
        _compress_kernel,
        grid=(n,),
        in_specs=[pl.BlockSpec((2, None, ncp, half), lambda i: (0, i, 0, 0)),
                  pl.BlockSpec((2, 2, half), lambda i: (0, 0, 0)),
                  pl.BlockSpec((2, 2 * half, hid), lambda i: (0, 0, 0)),
                  pl.BlockSpec((hid, HEAD_DIM), lambda i: (0, 0)),
                  pl.BlockSpec((HEAD_DIM, hid), lambda i: (0, 0)),
                  pl.BlockSpec((1, HEAD_DIM), lambda i: (0, 0))],
        out_specs=[pl.BlockSpec((None, ncp, HEAD_DIM), lambda i: (i, 0, 0)),
                   pl.BlockSpec((None, HEAD_DIM, ncp), lambda i: (i, 0, 0))],
        out_shape=[jax.ShapeDtypeStruct((n, ncp, HEAD_DIM), BF16),
                   jax.ShapeDtypeStruct((n, HEAD_DIM, ncp), BF16)],
        compiler_params=_params("parallel"),
        name="nsa_compress",
    )(a2, pos2, w1, w2k, w2vt, gain)


def _nsa_kernel(q_ref, kc_ref, vct_ref, ks_ref, vst_ref, kw_ref, vwt_ref, gate_ref, mt_ref, o_ref,
                qa_ref, out_acc, acc_ref, m_ref, l_ref, s_ref, *, tq, hpg, rank_w):
    qi = pl.program_id(2)
    q0 = pl.multiple_of(qi * tq, tq)
    hd = HEAD_DIM
    t_lane = q0 + lax.broadcasted_iota(jnp.int32, (1, tq), 1)

    def gate(br, h):
        r = br * hpg + h
        return gate_ref[r:r + 1, :]

    def q_head(h):
        return q_ref[:, h * hd:(h + 1) * hd]

    ncp = kc_ref.shape[0]
    c_sub = lax.broadcasted_iota(jnp.int32, (ncp, 1), 0)
    cmp_valid = (c_sub * CMP_STRIDE + (CMP_BLOCK - 1)) <= t_lane
    kc = kc_ref[...]
    vct = vct_ref[...]
    psum = jnp.zeros((ncp, tq), F32)
    for h in range(hpg):
        s = jnp.where(cmp_valid, _dot_nt(kc, q_head(h)), NEG_BIG)
        e = jnp.exp2(s - jnp.max(s, axis=0, keepdims=True))
        p = jnp.where(cmp_valid, e * (1.0 / jnp.sum(e, axis=0, keepdims=True)), 0.0)
        psum = psum + p
        out_acc[h] = gate(0, h) * _dot(vct, p.astype(BF16))

    mt = mt_ref[...]
    n_sb = mt.shape[0]
    p1, p2, p3 = _split3(psum)
    p_blk = _dot(mt, p1) + _dot(mt, p2) + _dot(mt, p3)
    blk = lax.broadcasted_iota(jnp.int32, (n_sb, 1), 0)
    cur = t_lane // SEL_BLOCK
    forced = jnp.logical_or(blk == cur, blk == 0)
    val = jnp.where(forced, jnp.inf, jnp.where(blk <= cur, p_blk, -jnp.inf))

    sub = lax.broadcasted_iota(jnp.int32, (SUBLANES, 1), 0)
    n_rb = n_sb // SUBLANES
    top = float(min(N_SEL, n_sb))
    bias_cols = []
    for c in range(tq // rank_w):
        v = val[:, c * rank_w:(c + 1) * rank_w]
        rows = [v[rb * SUBLANES:(rb + 1) * SUBLANES, :] for rb in range(n_rb)]
        cnt = [jnp.zeros((SUBLANES, rank_w), F32) for _ in range(n_rb)]
        for i in range(n_sb):
            vi = v[i:i + 1, :]
            for rb in range(n_rb):
                lo = rb * SUBLANES
                ge = lambda: jnp.where(vi >= rows[rb], 1.0, 0.0)
                gt = lambda: jnp.where(vi > rows[rb], 1.0, 0.0)
                if lo > i:
                    beats = ge()
                elif lo + SUBLANES - 1 <= i:
                    beats = gt()
                else:
                    beats = jnp.where(sub > (i - lo), ge(), gt())
                cnt[rb] = cnt[rb] + beats
        cnt = jnp.concatenate(cnt, axis=0)
        cur_c = cur[:, c * rank_w:(c + 1) * rank_w]
        picked = jnp.where(cnt < top, jnp.where(blk <= cur_c, 0.0, MASK_VAL), MASK_VAL)
        bias_cols.append(picked)
    sel_bias = jnp.concatenate(bias_cols, axis=1)
    sel_bias = jnp.concatenate([sel_bias, jnp.zeros((LANES - n_sb, tq), F32)], axis=0)
    sb_t = sel_bias.T.astype(BF16)
    for h in range(hpg):
        qa_ref[h, :, 0:hd] = q_head(h)
        qa_ref[h, :, hd:2 * hd] = sb_t

    def flash_reset():
        m_ref[...] = jnp.full(m_ref.shape, NEG_BIG, F32)
        l_ref[...] = jnp.zeros_like(l_ref)
        acc_ref[...] = jnp.zeros_like(acc_ref)

    def flash_step(k_t, vt_t, qsel, mask):
        tile_max = []
        for h in range(hpg):
            s = _dot_nt(k_t, qsel(h))
            if mask is not None:
                s = jnp.where(mask, s, MASK_VAL)
            s_ref[h] = s
            tile_max.append(jnp.max(s, axis=0, keepdims=True))
        for h in range(hpg):
            m_old = m_ref[h]
            m_new = jnp.maximum(m_old, tile_max[h])
            alpha = jnp.exp2(m_old - m_new)
            p = jnp.exp2(s_ref[h] - m_new)
            l_ref[h] = alpha * l_ref[h] + jnp.sum(p, axis=0, keepdims=True)
            acc_ref[h] = alpha * acc_ref[h] + _dot(vt_t, p.astype(BF16))
            m_ref[h] = m_new

    def flash_out(h):
        return acc_ref[h] * (1.0 / l_ref[h])

    tk = tq
    k_sub = lax.broadcasted_iota(jnp.int32, (tk, 1), 0)
    q_lane = lax.broadcasted_iota(jnp.int32, (1, tq), 1)
    causal = k_sub <= q_lane
    lane_j = lax.broadcasted_iota(jnp.int32, (1, LANES), 1)
    key_blk = k_sub // SEL_BLOCK

    def ks_aug(k0):
        onehot = jnp.where(lane_j == key_blk + k0 // SEL_BLOCK, 1.0, 0.0).astype(BF16)
        return jnp.concatenate([ks_ref[pl.ds(k0, tk), :], onehot], axis=1)

    qa_head = lambda h: qa_ref[h]

    flash_reset()

    def sel_body(kt, carry):
        k0 = pl.multiple_of(kt * tk, tk)
        flash_step(ks_aug(k0), vst_ref[:, pl.ds(k0, tk)], qa_head, None)
        return carry

    lax.fori_loop(0, qi, sel_body, 0)
    flash_step(ks_aug(q0), vst_ref[:, pl.ds(q0, tk)], qa_head, causal)
    for h in range(hpg):
        out_acc[h] = out_acc[h] + gate(1, h) * flash_out(h)

    flash_reset()
    flash_step(kw_ref[pl.ds(q0, tk), :], vwt_ref[:, pl.ds(q0, tk)], q_head, causal)
    d = 1
    while d * tq - (tk - 1) < WINDOW:
        all_valid = d * tq + (tq - 1) < WINDOW
        mask = None if all_valid else (q_lane + d * tq - k_sub) < WINDOW

        @pl.when(qi >= d)
        def _(d=d, mask=mask):
            k0 = pl.multiple_of(q0 - d * tq, tk)
            flash_step(kw_ref[pl.ds(k0, tk), :], vwt_ref[:, pl.ds(k0, tk)], q_head, mask)
        d += 1

    for h in range(hpg):
        o = out_acc[h] + gate(2, h) * flash_out(h)
        o_ref[:, h * hd:(h + 1) * hd] = o.T.astype(o_ref.dtype)


def _cmp_to_sel_t(ncp, n_cmp, n_sb):
    c0 = np.arange(ncp)[None, :] * CMP_STRIDE
    s0 = np.arange(n_sb)[:, None] * SEL_BLOCK
    ov = np.minimum(c0 + CMP_BLOCK, s0 + SEL_BLOCK) - np.maximum(c0, s0)
    w = np.maximum(ov, 0) / CMP_BLOCK
    w = w * (np.arange(ncp)[None, :] < n_cmp)
    return jnp.asarray(w, dtype=BF16)


def _nsa_attention(qhat, k_cmp, v_cmp_t, k_tok, v_t, gates_t, *, B, S, D, tq=512):
    T = B * S
    hpg = N_HEADS // N_KV
    gw = hpg * HEAD_DIM
    tq = min(tq, S)
    nq = S // tq
    ncp = k_cmp.shape[1]
    n_cmp = (S - CMP_BLOCK) // CMP_STRIDE + 1
    n_sb = S // SEL_BLOCK
    assert n_sb % SUBLANES == 0 and n_sb <= LANES
    mt = _cmp_to_sel_t(ncp, n_cmp, n_sb)
    grow = gates_t.shape[0] // N_KV
    return pl.pallas_call(
        functools.partial(_nsa_kernel, tq=tq, hpg=hpg, rank_w=min(256, tq)),
        grid=(B, N_KV, nq),
        in_specs=[pl.BlockSpec((tq, gw), lambda b, g, i: (b * nq + i, g)),
                  pl.BlockSpec((None, ncp, HEAD_DIM), lambda b, g, i: (b * N_KV + g, 0, 0)),
                  pl.BlockSpec((None, HEAD_DIM, ncp), lambda b, g, i: (b * N_KV + g, 0, 0)),
                  pl.BlockSpec((S, HEAD_DIM), lambda b, g, i: (b, 2 * N_KV + g)),
                  pl.BlockSpec((HEAD_DIM, S), lambda b, g, i: (g, b)),
                  pl.BlockSpec((S, HEAD_DIM), lambda b, g, i: (b, 3 * N_KV + g)),
                  pl.BlockSpec((HEAD_DIM, S), lambda b, g, i: (N_KV + g, b)),
                  pl.BlockSpec((grow, tq), lambda b, g, i: (g, b * nq + i)),
                  pl.BlockSpec((n_sb, ncp), lambda b, g, i: (0, 0))],
        out_specs=pl.BlockSpec((tq, gw), lambda b, g, i: (b * nq + i, g)),
        out_shape=jax.ShapeDtypeStruct((T, D), BF16),
        scratch_shapes=[pltpu.VMEM((hpg, tq, 2 * HEAD_DIM), BF16),
                        pltpu.VMEM((hpg, HEAD_DIM, tq), F32),
                        pltpu.VMEM((hpg, HEAD_DIM, tq), F32),
                        pltpu.VMEM((hpg, 1, tq), F32),
                        pltpu.VMEM((hpg, 1, tq), F32),
                        pltpu.VMEM((hpg, tq, tq), F32)],
        compiler_params=_params("parallel", "parallel", "parallel"),
        name="nsa_attention",
    )(qhat, k_cmp, v_cmp_t, k_tok, v_t, k_tok, v_t, gates_t, mt)


def _mlp(x, g, w_up, w_down, layer):
    xn = _rmsnorm(x, g)
    h = _matmul(xn, w_up, layer=layer, epilogue="relu2", name="mlp_up")
    return _matmul(h, w_down, layer=layer, epilogue="resid", extra=x, out_dtype=F32, tm=1024, name="mlp_down")


def kernel(x, attn_norm_g, mlp_norm_g, m_w_in, m_b_gate, m_head_g, m_w_out, kv_norm_g, w_kv, k_norm_g,
           cmp_pos, cmp_w1, cmp_w2, n_w_qg, q_norm_g, n_w_out, mlp_w_up, mlp_w_down):
    B, S, D = x.shape
    T = B * S
    xf = x.reshape(T, D)
    dv = D // M_HEADS
    dk = dv // 2
    qk = M_HEADS * dk
    nq = 2 * qk + 2 * D
    kv_dim = N_KV * HEAD_DIM
    hpg = N_HEADS // N_KV
    k_fold = (HEAD_DIM ** -0.5) * LOG2E

    xn = _rmsnorm(xf, attn_norm_g[0])
    proj = _matmul(xn, m_w_in, layer=0, n_out=nq, epilogue="mlstm_in", tn=min(512, qk),
                   cfg=dict(m_qk=qk, d_model=D, k_scale=float(dk ** -0.5)), name="mlstm_in_proj")
    gcol, grow = _mlstm_gates(xn, m_w_in[0, :, nq:], m_b_gate[0])
    hg = _mlstm(proj, gcol, grow, m_head_g[0], B=B, S=S, D=D)
    xf = _matmul(hg, m_w_out, layer=0, epilogue="resid", extra=xf, out_dtype=F32, tn=256, name="mlstm_out_proj")
    xf = _mlp(xf, mlp_norm_g[0], mlp_w_up, mlp_w_down, 0)

    qg = q_norm_g[0]
    ones = jnp.ones((kv_dim,), F32)
    kv_gain = jnp.concatenate([
        ones, ones,
        jnp.tile(k_norm_g[1] * qg[1] * k_fold, N_KV),
        jnp.tile(k_norm_g[2] * qg[2] * k_fold, N_KV)]).reshape(1, 4 * kv_dim)
    xs = _rmsnorm(xf, kv_norm_g)
    k_tok = _matmul(xs, w_kv, n_out=4 * kv_dim, col_map=lambda j: j + j // 3, epilogue="kvnorm", extra=kv_gain,
                    tn=kv_dim, cfg=dict(kv_dim=kv_dim, norm_slots=(2, 3)), name="nsa_kv_proj")
    w_vt = jnp.concatenate([w_kv[:, 3 * kv_dim:4 * kv_dim], w_kv[:, 5 * kv_dim:6 * kv_dim]], axis=1).T
    v_t = _matmul(w_vt, xs, nt=True, tm=kv_dim, tn=1024, name="nsa_vt_proj")
    ncp = S // CMP_STRIDE
    a2 = k_tok[:, :2 * kv_dim].reshape(B, ncp, CMP_STRIDE, 2, N_KV, HEAD_DIM)
    a2 = a2.transpose(3, 0, 4, 1, 2, 5).reshape(2, B * N_KV, ncp, CMP_STRIDE * HEAD_DIM)
    pos2 = cmp_pos.reshape(2, 2, CMP_STRIDE * HEAD_DIM).astype(F32)
    cmp_gain = (k_norm_g[0] * qg[0] * k_fold).reshape(1, HEAD_DIM)
    k_cmp, v_cmp_t = _compress(a2, pos2, cmp_w1.astype(BF16), cmp_w2[0].astype(BF16), cmp_w2[1].T.astype(BF16),
                               cmp_gain)

    xn = _rmsnorm(xf, attn_norm_g[1])
    w_qg = n_w_qg[0]
    qhat = _matmul(xn, n_w_qg, layer=0, n_out=D, epilogue="headnorm", name="nsa_q_proj")
    grow_n = -(-N_BRANCH * hpg // SUBLANES) * SUBLANES
    w_gate = w_qg[:, D:].reshape(D, N_BRANCH, N_KV, hpg).transpose(2, 1, 3, 0).reshape(N_KV, N_BRANCH * hpg, D)
    w_gate = jnp.zeros((N_KV, grow_n, D), F32).at[:, :N_BRANCH * hpg].set(w_gate).reshape(N_KV * grow_n, D)
    gates_t = _matmul(w_gate, xn, nt=True, epilogue="sigmoid", out_dtype=F32, tn=1024, name="nsa_gate_proj")
    o = _nsa_attention(qhat, k_cmp, v_cmp_t, k_tok, v_t, gates_t, B=B, S=S, D=D)
    xf = _matmul(o, n_w_out, layer=0, epilogue="resid", extra=xf, out_dtype=F32, tn=256, name="nsa_out_proj")
    xf = _mlp(xf, mlp_norm_g[1], mlp_w_up, mlp_w_down, 1)
    return xf.reshape(B, S, D)
```

```python
import functools

import numpy as np
import jax
import jax.numpy as jnp
from jax import lax
from jax.experimental import pallas as pl
from jax.experimental.pallas import tpu as pltpu

F32 = jnp.float32
BF16 = jnp.bfloat16

EPS = 1e-6
NEG_BIG = -1e30
MASK_VAL = -(2.0 ** 100)
LOG2E = 1.4426950408889634

M_HEADS = 8
GATE_CAP = 15.0
M_CHUNK = 256

N_HEADS = 32
N_KV = 4
HEAD_DIM = 128
N_BRANCH = 3
CMP_BLOCK = 32
CMP_STRIDE = 16
SEL_BLOCK = 64
N_SEL = 16
WINDOW = 512

LANES = 128
SUBLANES = 8
VMEM_LIMIT = 56 * 1024 * 1024


def _params(*sem):
    return pltpu.CompilerParams(dimension_semantics=sem, vmem_limit_bytes=VMEM_LIMIT)


def _dot(a, b):
    return jnp.dot(a, b, preferred_element_type=F32)


def _dot_nt(a, b):
    return lax.dot_general(a, b, (((1,), (1,)), ((), ())), preferred_element_type=F32)


def _split3(x):
    x1 = x.astype(BF16)
    r1 = x - x1.astype(F32)
    x2 = r1.astype(BF16)
    r2 = r1 - x2.astype(F32)
    return x1, x2, r2.astype(BF16)


def _rmsnorm_kernel(x_ref, g_ref, o_ref):
    x = x_ref[...]
    ms = jnp.mean(x * x, axis=-1, keepdims=True)
    o_ref[...] = (x * lax.rsqrt(ms + EPS) * g_ref[...]).astype(o_ref.dtype)


def _rmsnorm(x, g, tm=256):
    T, D = x.shape
    tm = min(tm, T)
    return pl.pallas_call(
        _rmsnorm_kernel,
        grid=(T // tm,),
        in_specs=[pl.BlockSpec((tm, D), lambda i: (i, 0)),
                  pl.BlockSpec((1, D), lambda i: (0, 0))],
        out_specs=pl.BlockSpec((tm, D), lambda i: (i, 0)),
        out_shape=jax.ShapeDtypeStruct((T, D), BF16),
        compiler_params=_params("parallel"),
        name="rmsnorm",
    )(x, g.reshape(1, D).astype(F32))


def _group_rstd(y):
    out = []
    for g in range(y.shape[1] // HEAD_DIM):
        yg = y[:, g * HEAD_DIM:(g + 1) * HEAD_DIM]
        out.append(lax.rsqrt(jnp.mean(yg * yg, axis=-1, keepdims=True) + EPS))
    return out


def _mm_kernel(*refs, epilogue, nk, tn, nt, cfg):
    if epilogue in ("resid", "kvnorm"):
        a_ref, w_ref, e_ref, o_ref = refs
    else:
        a_ref, w_ref, o_ref = refs
        e_ref = None
    j = pl.program_id(1)
    k = pl.program_id(2)

    def product():
        a = a_ref[...].astype(BF16)
        w = w_ref[...].astype(BF16)
        return _dot_nt(a, w) if nt else _dot(a, w)

    if nk > 1:
        assert epilogue == "resid" and o_ref.dtype == F32

        @pl.when(k == 0)
        def _():
            o_ref[...] = e_ref[...]

        o_ref[...] += product()
        return

    part = product()

    def finish(y):
        if epilogue == "none":
            o_ref[...] = y.astype(o_ref.dtype)
        elif epilogue == "relu2":
            r = jnp.maximum(y, 0.0)
            o_ref[...] = (r * r).astype(o_ref.dtype)
        elif epilogue == "sigmoid":
            o_ref[...] = jax.nn.sigmoid(y).astype(o_ref.dtype)
        elif epilogue == "resid":
            o_ref[...] = (e_ref[...] + y).astype(o_ref.dtype)
        elif epilogue == "headnorm":
            rstd = _group_rstd(y)
            for g, r in enumerate(rstd):
                sl = slice(g * HEAD_DIM, (g + 1) * HEAD_DIM)
                o_ref[:, sl] = (y[:, sl] * r).astype(o_ref.dtype)
        elif epilogue == "kvnorm":
            slot = (j * tn) // cfg["kv_dim"]
            is_norm = functools.reduce(jnp.logical_or, [slot == s for s in cfg["norm_slots"]])
            rstd = _group_rstd(y)
            gain = e_ref[...]
            for g, r in enumerate(rstd):
                sl = slice(g * HEAD_DIM, (g + 1) * HEAD_DIM)
                mult = jnp.where(is_norm, r, 1.0)
                o_ref[:, sl] = (y[:, sl] * mult * gain[:, sl]).astype(o_ref.dtype)
        elif epilogue == "mlstm_in":
            col0 = j * tn
            qk, d = cfg["m_qk"], cfg["d_model"]
            is_k = jnp.logical_and(col0 >= qk, col0 < 2 * qk)
            is_o = col0 >= 2 * qk + d
            lin = y * jnp.where(is_k, cfg["k_scale"], 1.0)
            o_ref[...] = jnp.where(is_o, jax.nn.sigmoid(y), lin).astype(o_ref.dtype)
        else:
            raise ValueError(epilogue)

    finish(part)


def _matmul(a, w, *, layer=None, nt=False, n_out=None, col_map=None, epilogue="none", extra=None,
            out_dtype=BF16, tm=2048, tn=512, tk=4096, cfg=None, name="matmul"):
    M, K = a.shape
    wshape = w.shape[1:] if layer is not None else w.shape
    N = n_out if n_out is not None else (wshape[0] if nt else wshape[1])
    tm, tn, tk = min(tm, M), min(tn, N), min(tk, K)
    assert M % tm == 0 and N % tn == 0 and K % tk == 0
    nk = K // tk
    cm = col_map if col_map is not None else (lambda j: j)
    if layer is not None and nt:
        w_spec = pl.BlockSpec((None, tn, tk), lambda i, j, k: (layer, cm(j), k))
    elif layer is not None:
        w_spec = pl.BlockSpec((None, tk, tn), lambda i, j, k: (layer, k, cm(j)))
    elif nt:
        w_spec = pl.BlockSpec((tn, tk), lambda i, j, k: (cm(j), k))
    else:
        w_spec = pl.BlockSpec((tk, tn), lambda i, j, k: (k, cm(j)))
    a_mode = dict(pipeline_mode=pl.Buffered(1)) if (nk == 1 and N // tn > 1) else {}
    in_specs = [pl.BlockSpec((tm, tk), lambda i, j, k: (i, k), **a_mode), w_spec]
    args = [a, w]
    if epilogue == "resid":
        in_specs.append(pl.BlockSpec((tm, tn), lambda i, j, k: (i, j)))
        args.append(extra)
    elif epilogue == "kvnorm":
        in_specs.append(pl.BlockSpec((1, tn), lambda i, j, k: (0, j)))
        args.append(extra)
    return pl.pallas_call(
        functools.partial(_mm_kernel, epilogue=epilogue, nk=nk, tn=tn, nt=nt, cfg=cfg),
        grid=(M // tm, N // tn, nk),
        in_specs=in_specs,
        out_specs=pl.BlockSpec((tm, tn), lambda i, j, k: (i, j)),
        out_shape=jax.ShapeDtypeStruct((M, N), out_dtype),
        compiler_params=_params("parallel", "parallel", "arbitrary"),
        name=name,
    )(*args)


def _log_sigmoid(x):
    return jnp.minimum(x, 0.0) - jnp.log(1.0 + jnp.exp(-jnp.abs(x)))


def _gates_kernel(xn_ref, wg_ref, wgt_ref, brow_ref, bcol_ref, gcol_ref, grow_ref):
    xn = xn_ref[...]
    pre_c = _dot(xn, wg_ref[...]) + brow_ref[...]
    g_c = GATE_CAP * jnp.tanh(pre_c / GATE_CAP)
    col = lax.broadcasted_iota(jnp.int32, g_c.shape, 1)
    gcol_ref[...] = jnp.where(col < M_HEADS, g_c, _log_sigmoid(g_c))
    pre_r = _dot_nt(wgt_ref[...], xn) + bcol_ref[...]
    g_r = GATE_CAP * jnp.tanh(pre_r / GATE_CAP)
    row = lax.broadcasted_iota(jnp.int32, g_r.shape, 0)
    grow_ref[...] = jnp.where(row < M_HEADS, g_r, _log_sigmoid(g_r))


def _mlstm_gates(xn, w_g, b_g, tm=512):
    T, D = xn.shape
    ng = 2 * M_HEADS
    tm = min(tm, T)
    wg = jnp.zeros((D, LANES), F32).at[:, :ng].set(w_g).astype(BF16)
    wgt = w_g.T.astype(BF16)
    brow = jnp.zeros((1, LANES), F32).at[0, :ng].set(b_g)
    bcol = b_g.reshape(ng, 1).astype(F32)
    return pl.pallas_call(
        _gates_kernel,
        grid=(T // tm,),
        in_specs=[pl.BlockSpec((tm, D), lambda i: (i, 0)),
                  pl.BlockSpec((D, LANES), lambda i: (0, 0)),
                  pl.BlockSpec((ng, D), lambda i: (0, 0)),
                  pl.BlockSpec((1, LANES), lambda i: (0, 0)),
                  pl.BlockSpec((ng, 1), lambda i: (0, 0))],
        out_specs=[pl.BlockSpec((tm, LANES), lambda i: (i, 0)),
                   pl.BlockSpec((ng, tm), lambda i: (0, i))],
        out_shape=[jax.ShapeDtypeStruct((T, LANES), F32),
                   jax.ShapeDtypeStruct((ng, T), F32)],
        compiler_params=_params("parallel"),
        name="mlstm_gates",
    )(xn, wg, wgt, brow, bcol)


def _mlstm_kernel(q_ref, k_ref, v_ref, o_ref, gc_ref, gr_ref, hg_ref, out_ref,
                  c_ref, n_ref, m_ref, *, dk, dv):
    L = q_ref.shape[0]

    @pl.when(pl.program_id(1) == 0)
    def _():
        c_ref[...] = jnp.zeros_like(c_ref)
        n_ref[...] = jnp.zeros_like(n_ref)
        m_ref[...] = jnp.zeros_like(m_ref)

    row = lax.broadcasted_iota(jnp.int32, (L, L), 0)
    col = lax.broadcasted_iota(jnp.int32, (L, L), 1)
    causal = col <= row
    tril = jnp.where(causal, 1.0, 0.0).astype(BF16)
    triu = jnp.where(row <= col, 1.0, 0.0).astype(BF16)

    gc = gc_ref[...]
    gr = gr_ref[...]
    c1, c2, c3 = _split3(gc)
    b_col_all = _dot(tril, c1) + _dot(tril, c2) + _dot(tril, c3)
    r1, r2, r3 = _split3(gr)
    b_row_all = _dot(r1, triu) + _dot(r2, triu) + _dot(r3, triu)

    for h in range(M_HEADS):
        li_row = gr[h:h + 1, :]
        li_col = gc[:, h:h + 1]
        b_row = b_row_all[M_HEADS + h:M_HEADS + h + 1, :]
        b_col = b_col_all[:, M_HEADS + h:M_HEADS + h + 1]
        m_prev = m_ref[h]

        log_d = jnp.where(causal, b_col - b_row + li_row, -jnp.inf)
        log_inter = b_col + m_prev
        m_t = jnp.maximum(log_inter, jnp.max(log_d, axis=1, keepdims=True))
        w_intra = jnp.exp(log_d - m_t)
        w_inter = jnp.exp(log_inter - m_t)

        qh = q_ref[:, h * dk:(h + 1) * dk]
        kh = k_ref[:, h * dk:(h + 1) * dk]
        vh = v_ref[:, h * dv:(h + 1) * dv]
        c_old = c_ref[h]
        n_old = n_ref[h]

        s = _dot_nt(qh, kh) * w_intra
        num = w_inter * _dot(qh, c_old.astype(BF16)) + _dot(s.astype(BF16), vh)
        qn = jnp.sum(qh.astype(F32) * n_old, axis=1, keepdims=True)
        den = w_inter * qn + jnp.sum(s, axis=1, keepdims=True)
        hval = num / jnp.maximum(jnp.abs(den), jnp.exp(-m_t))

        ms = jnp.mean(hval * hval, axis=1, keepdims=True)
        hn = hval * lax.rsqrt(ms + EPS) * hg_ref[:, h * dv:(h + 1) * dv]
        out_ref[:, h * dv:(h + 1) * dv] = (o_ref[:, h * dv:(h + 1) * dv].astype(F32) * hn).astype(out_ref.dtype)

        b_last = b_col[L - 1:L, :]
        log_w = b_last - b_col + li_col
        m_new = jnp.maximum(b_last + m_prev, jnp.max(log_w, axis=0, keepdims=True))
        wk = jnp.exp(log_w - m_new) * kh.astype(F32)
        decay = jnp.exp(b_last + m_prev - m_new)
        c_ref[h] = decay * c_old + _dot(wk.T.astype(BF16), vh)
        n_ref[h] = decay * n_old + jnp.sum(wk, axis=0, keepdims=True)
        m_ref[h] = m_new


def _mlstm(proj, gcol, grow, head_g, *, B, S, D):
    T = B * S
    dv = D // M_HEADS
    dk = dv // 2
    qk = M_HEADS * dk
    L = min(M_CHUNK, S)
    nc = S // L
    rb = lambda b, c: b * nc + c
    assert D % qk == 0
    return pl.pallas_call(
        functools.partial(_mlstm_kernel, dk=dk, dv=dv),
        grid=(B, nc),
        in_specs=[pl.BlockSpec((L, qk), lambda b, c: (rb(b, c), 0)),
                  pl.BlockSpec((L, qk), lambda b, c: (rb(b, c), 1)),
                  pl.BlockSpec((L, D), lambda b, c: (rb(b, c), (2 * qk) // D)),
                  pl.BlockSpec((L, D), lambda b, c: (rb(b, c), (2 * qk) // D + 1)),
                  pl.BlockSpec((L, LANES), lambda b, c: (rb(b, c), 0)),
                  pl.BlockSpec((2 * M_HEADS, L), lambda b, c: (0, rb(b, c))),
                  pl.BlockSpec((1, D), lambda b, c: (0, 0))],
        out_specs=pl.BlockSpec((L, D), lambda b, c: (rb(b, c), 0)),
        out_shape=jax.ShapeDtypeStruct((T, D), BF16),
        scratch_shapes=[pltpu.VMEM((M_HEADS, dk, dv), F32),
                        pltpu.VMEM((M_HEADS, 1, dk), F32),
                        pltpu.VMEM((M_HEADS, 1, 1), F32)],
        compiler_params=_params("parallel", "arbitrary"),
        name="mlstm",
    )(proj, proj, proj, proj, gcol, grow, head_g.reshape(1, D).astype(F32))


def _compress_kernel(a_ref, pos_ref, w1_ref, w2k_ref, w2vt_ref, gain_ref, ok_ref, ovt_ref):
    half = a_ref.shape[2]
    ncp = a_ref.shape[1]

    def hidden(j):
        a = a_ref[j].astype(F32)
        lo = (a + pos_ref[j, 0:1, :]).astype(BF16)
        hi = (a + pos_ref[j, 1:2, :]).astype(BF16)
        u = _dot(lo, w1_ref[j, 0:half, :])
        v = _dot(hi, w1_ref[j, half:2 * half, :])
        hid = u + pltpu.roll(v, shift=ncp - 1, axis=0)
        c0 = float(np.sqrt(2.0 / np.pi))
        cdf = 0.5 * (1.0 + jnp.tanh(c0 * (hid + 0.044715 * (hid * hid * hid))))
        return (hid * cdf).astype(BF16)

    yk = _dot(hidden(0), w2k_ref[...])
    rstd = lax.rsqrt(jnp.mean(yk * yk, axis=-1, keepdims=True) + EPS)
    ok_ref[...] = (yk * rstd * gain_ref[...]).astype(ok_ref.dtype)
    ovt_ref[...] = _dot_nt(w2vt_ref[...], hidden(1)).astype(ovt_ref.dtype)


def _compress(a2, pos2, w1, w2k, w2vt, gain):
    _, n, ncp, half = a2.shape
    hid = w1.shape[2]
    return pl.pallas_call(
        _compress_kernel,
        grid=(n,),
        in_specs=[pl.BlockSpec((2, None, ncp, half), lambda i: (0, i, 0, 0)),
                  pl.BlockSpec((2, 2, half), lambda i: (0, 0, 0)),
                  pl.BlockSpec((2, 2 * half, hid), lambda i: (0, 0, 0)),
                  pl.BlockSpec((hid, HEAD_DIM), lambda i: (0, 0)),
                  pl.BlockSpec((HEAD_DIM, hid), lambda i: (0, 0)),
                  pl.BlockSpec((1, HEAD_DIM), lambda i: (0, 0))],
        out_specs=[pl.BlockSpec((None, ncp, HEAD_DIM), lambda i: (i, 0, 0)),
                   pl.BlockSpec((None, HEAD_DIM, ncp), lambda i: (i, 0, 0))],
        out_shape=[jax.ShapeDtypeStruct((n, ncp, HEAD_DIM), BF16),
                   jax.ShapeDtypeStruct((n, HEAD_DIM, ncp), BF16)],
        compiler_params=_params("parallel"),
        name="nsa_compress",
    )(a2, pos2, w1, w2k, w2vt, gain)


def _nsa_kernel(q_ref, kc_ref, vct_ref, ks_ref, vst_ref, kw_ref, vwt_ref, gate_ref, mt_ref, o_ref,
                qa_ref, out_acc, acc_ref, m_ref, l_ref, s_ref, *, tq, hpg, rank_w):
    qi = pl.program_id(2)
    q0 = pl.multiple_of(qi * tq, tq)
    hd = HEAD_DIM
    t_lane = q0 + lax.broadcasted_iota(jnp.int32, (1, tq), 1)

    def gate(br, h):
        r = br * hpg + h
        return gate_ref[r:r + 1, :]

    def q_head(h):
        return q_ref[:, h * hd:(h + 1) * hd]

    ncp = kc_ref.shape[0]
    c_sub = lax.broadcasted_iota(jnp.int32, (ncp, 1), 0)
    cmp_valid = (c_sub * CMP_STRIDE + (CMP_BLOCK - 1)) <= t_lane
    kc = kc_ref[...]
    vct = vct_ref[...]
    psum = jnp.zeros((ncp, tq), F32)
    for h in range(hpg):
        s = jnp.where(cmp_valid, _dot_nt(kc, q_head(h)), NEG_BIG)
        e = jnp.exp2(s - jnp.max(s, axis=0, keepdims=True))
        p = jnp.where(cmp_valid, e * (1.0 / jnp.sum(e, axis=0, keepdims=True)), 0.0)
        psum = psum + p
        out_acc[h] = gate(0, h) * _dot(vct, p.astype(BF16))

    mt = mt_ref[...]
    n_sb = mt.shape[0]
    p1, p2, p3 = _split3(psum)
    p_blk = _dot(mt, p1) + _dot(mt, p2) + _dot(mt, p3)
    blk = lax.broadcasted_iota(jnp.int32, (n_sb, 1), 0)
    cur = t_lane // SEL_BLOCK
    forced = jnp.logical_or(blk == cur, blk == 0)
    val = jnp.where(forced, jnp.inf, jnp.where(blk <= cur, p_blk, -jnp.inf))

    sub = lax.broadcasted_iota(jnp.int32, (SUBLANES, 1), 0)
    n_rb = n_sb // SUBLANES
    top = float(min(N_SEL, n_sb))
    bias_cols = []
    for c in range(tq // rank_w):
        v = val[:, c * rank_w:(c + 1) * rank_w]
        rows = [v[rb * SUBLANES:(rb + 1) * SUBLANES, :] for rb in range(n_rb)]
        cnt = [jnp.zeros((SUBLANES, rank_w), F32) for _ in range(n_rb)]
        for i in range(n_sb):
            vi = v[i:i + 1, :]
            for rb in range(n_rb):
                lo = rb * SUBLANES
                ge = lambda: jnp.where(vi >= rows[rb], 1.0, 0.0)
                gt = lambda: jnp.where(vi > rows[rb], 1.0, 0.0)
                if lo > i:
                    beats = ge()
                elif lo + SUBLANES - 1 <= i:
                    beats = gt()
                else:
                    beats = jnp.where(sub > (i - lo), ge(), gt())
                cnt[rb] = cnt[rb] + beats
        cnt = jnp.concatenate(cnt, axis=0)
        cur_c = cur[:, c * rank_w:(c + 1) * rank_w]
        picked = jnp.where(cnt < top, jnp.where(blk <= cur_c, 0.0, MASK_VAL), MASK_VAL)
        bias_cols.append(picked)
    sel_bias = jnp.concatenate(bias_cols, axis=1)
    sel_bias = jnp.concatenate([sel_bias, jnp.zeros((LANES - n_sb, tq), F32)], axis=0)
    sb_t = sel_bias.T.astype(BF16)
    for h in range(hpg):
        qa_ref[h, :, 0:hd] = q_head(h)
        qa_ref[h, :, hd:2 * hd] = sb_t

    def flash_reset():
        m_ref[...] = jnp.full(m_ref.shape, NEG_BIG, F32)
        l_ref[...] = jnp.zeros_like(l_ref)
        acc_ref[...] = jnp.zeros_like(acc_ref)

    def flash_step(k_t, vt_t, qsel, mask):
        tile_max = []
        for h in range(hpg):
            s = _dot_nt(k_t, qsel(h))
            if mask is not None:
                s = jnp.where(mask, s, MASK_VAL)
            s_ref[h] = s
            tile_max.append(jnp.max(s, axis=0, keepdims=True))
        for h in range(hpg):
            m_old = m_ref[h]
            m_new = jnp.maximum(m_old, tile_max[h])
            alpha = jnp.exp2(m_old - m_new)
            p = jnp.exp2(s_ref[h] - m_new)
            l_ref[h] = alpha * l_ref[h] + jnp.sum(p, axis=0, keepdims=True)
            acc_ref[h] = alpha * acc_ref[h] + _dot(vt_t, p.astype(BF16))
            m_ref[h] = m_new

    def flash_out(h):
        return acc_ref[h] * (1.0 / l_ref[h])

    tk = tq
    k_sub = lax.broadcasted_iota(jnp.int32, (tk, 1), 0)
    q_lane = lax.broadcasted_iota(jnp.int32, (1, tq), 1)
    causal = k_sub <= q_lane
    lane_j = lax.broadcasted_iota(jnp.int32, (1, LANES), 1)
    key_blk = k_sub // SEL_BLOCK

    def ks_aug(k0):
        onehot = jnp.where(lane_j == key_blk + k0 // SEL_BLOCK, 1.0, 0.0).astype(BF16)
        return jnp.concatenate([ks_ref[pl.ds(k0, tk), :], onehot], axis=1)

    qa_head = lambda h: qa_ref[h]

    flash_reset()

    def sel_body(kt, carry):
        k0 = pl.multiple_of(kt * tk, tk)
        flash_step(ks_aug(k0), vst_ref[:, pl.ds(k0, tk)], qa_head, None)
        return carry

    lax.fori_loop(0, qi, sel_body, 0)
    flash_step(ks_aug(q0), vst_ref[:, pl.ds(q0, tk)], qa_head, causal)
    for h in range(hpg):
        out_acc[h] = out_acc[h] + gate(1, h) * flash_out(h)

    flash_reset()
    flash_step(kw_ref[pl.ds(q0, tk), :], vwt_ref[:, pl.ds(q0, tk)], q_head, causal)
    d = 1
    while d * tq - (tk - 1) < WINDOW:
        all_valid = d * tq + (tq - 1) < WINDOW
        mask = None if all_valid else (q_lane + d * tq - k_sub) < WINDOW

        @pl.when(qi >= d)
        def _(d=d, mask=mask):
            k0 = pl.multiple_of(q0 - d * tq, tk)
            flash_step(kw_ref[pl.ds(k0, tk), :], vwt_ref[:, pl.ds(k0, tk)], q_head, mask)
        d += 1

    for h in range(hpg):
        o = out_acc[h] + gate(2, h) * flash_out(h)
        o_ref[:, h * hd:(h + 1) * hd] = o.T.astype(o_ref.dtype)


def _cmp_to_sel_t(ncp, n_cmp, n_sb):
    c0 = np.arange(ncp)[None, :] * CMP_STRIDE
    s0 = np.arange(n_sb)[:, None] * SEL_BLOCK
    ov = np.minimum(c0 + CMP_BLOCK, s0 + SEL_BLOCK) - np.maximum(c0, s0)
    w = np.maximum(ov, 0) / CMP_BLOCK
    w = w * (np.arange(ncp)[None, :] < n_cmp)
    return jnp.asarray(w, dtype=BF16)


def _nsa_attention(qhat, k_cmp, v_cmp_t, k_tok, v_t, gates_t, *, B, S, D, tq=512):
    T = B * S
    hpg = N_HEADS // N_KV
    gw = hpg * HEAD_DIM
    tq = min(tq, S)
    nq = S // tq
    ncp = k_cmp.shape[1]
    n_cmp = (S - CMP_BLOCK) // CMP_STRIDE + 1
    n_sb = S // SEL_BLOCK
    assert n_sb % SUBLANES == 0 and n_sb <= LANES
    mt = _cmp_to_sel_t(ncp, n_cmp, n_sb)
    grow = gates_t.shape[0] // N_KV
    return pl.pallas_call(
        functools.partial(_nsa_kernel, tq=tq, hpg=hpg, rank_w=min(256, tq)),
        grid=(B, N_KV, nq),
        in_specs=[pl.BlockSpec((tq, gw), lambda b, g, i: (b * nq + i, g)),
                  pl.BlockSpec((None, ncp, HEAD_DIM), lambda b, g, i: (b * N_KV + g, 0, 0)),
                  pl.BlockSpec((None, HEAD_DIM, ncp), lambda b, g, i: (b * N_KV + g, 0, 0)),
                  pl.BlockSpec((S, HEAD_DIM), lambda b, g, i: (b, 2 * N_KV + g)),
                  pl.BlockSpec((HEAD_DIM, S), lambda b, g, i: (g, b)),
                  pl.BlockSpec((S, HEAD_DIM), lambda b, g, i: (b, 3 * N_KV + g)),
                  pl.BlockSpec((HEAD_DIM, S), lambda b, g, i: (N_KV + g, b)),
                  pl.BlockSpec((grow, tq), lambda b, g, i: (g, b * nq + i)),
                  pl.BlockSpec((n_sb, ncp), lambda b, g, i: (0, 0))],
        out_specs=pl.BlockSpec((tq, gw), lambda b, g, i: (b * nq + i, g)),
        out_shape=jax.ShapeDtypeStruct((T, D), BF16),
        scratch_shapes=[pltpu.VMEM((hpg, tq, 2 * HEAD_DIM), BF16),
                        pltpu.VMEM((hpg, HEAD_DIM, tq), F32),
                        pltpu.VMEM((hpg, HEAD_DIM, tq), F32),
                        pltpu.VMEM((hpg, 1, tq), F32),
                        pltpu.VMEM((hpg, 1, tq), F32),
                        pltpu.VMEM((hpg, tq, tq), F32)],
        compiler_params=_params("parallel", "parallel", "parallel"),
        name="nsa_attention",
    )(qhat, k_cmp, v_cmp_t, k_tok, v_t, k_tok, v_t, gates_t, mt)


def _mlp(x, g, w_up, w_down, layer):
    xn = _rmsnorm(x, g)
    h = _matmul(xn, w_up, layer=layer, epilogue="relu2", name="mlp_up")
    return _matmul(h, w_down, layer=layer, epilogue="resid", extra=x, out_dtype=F32, tm=2048, tn=1024, tk=512,
                   name="mlp_down")


def kernel(x, attn_norm_g, mlp_norm_g, m_w_in, m_b_gate, m_head_g, m_w_out, kv_norm_g, w_kv, k_norm_g,
           cmp_pos, cmp_w1, cmp_w2, n_w_qg, q_norm_g, n_w_out, mlp_w_up, mlp_w_down):
    B, S, D = x.shape
    T = B * S
    xf = x.reshape(T, D)
    dv = D // M_HEADS
    dk = dv // 2
    qk = M_HEADS * dk
    nq = 2 * qk + 2 * D
    kv_dim = N_KV * HEAD_DIM
    hpg = N_HEADS // N_KV
    k_fold = (HEAD_DIM ** -0.5) * LOG2E

    xn = _rmsnorm(xf, attn_norm_g[0])
    proj = _matmul(xn, jnp.swapaxes(m_w_in, 1, 2), layer=0, nt=True, n_out=nq, epilogue="mlstm_in", tn=min(512, qk),
                   cfg=dict(m_qk=qk, d_model=D, k_scale=float(dk ** -0.5)), name="mlstm_in_proj")
    gcol, grow = _mlstm_gates(xn, m_w_in[0, :, nq:], m_b_gate[0])
    hg = _mlstm(proj, gcol, grow, m_head_g[0], B=B, S=S, D=D)
    xf = _matmul(hg, m_w_out, layer=0, epilogue="resid", extra=xf, out_dtype=F32, tn=256, name="mlstm_out_proj")
    xf = _mlp(xf, mlp_norm_g[0], mlp_w_up, mlp_w_down, 0)

    qg = q_norm_g[0]
    ones = jnp.ones((kv_dim,), F32)
    kv_gain = jnp.concatenate([
        ones, ones,
        jnp.tile(k_norm_g[1] * qg[1] * k_fold, N_KV),
        jnp.tile(k_norm_g[2] * qg[2] * k_fold, N_KV)]).reshape(1, 4 * kv_dim)
    xs = _rmsnorm(xf, kv_norm_g)
    k_tok = _matmul(xs, w_kv, n_out=4 * kv_dim, col_map=lambda j: j + j // 3, epilogue="kvnorm", extra=kv_gain,
                    tn=kv_dim, cfg=dict(kv_dim=kv_dim, norm_slots=(2, 3)), name="nsa_kv_proj")
    w_vt = jnp.concatenate([w_kv[:, 3 * kv_dim:4 * kv_dim], w_kv[:, 5 * kv_dim:6 * kv_dim]], axis=1).T
    v_t = _matmul(w_vt, xs, nt=True, tm=kv_dim, tn=1024, name="nsa_vt_proj")
    ncp = S // CMP_STRIDE
    a2 = k_tok[:, :2 * kv_dim].reshape(B, ncp, CMP_STRIDE, 2, N_KV, HEAD_DIM)
    a2 = a2.transpose(3, 0, 4, 1, 2, 5).reshape(2, B * N_KV, ncp, CMP_STRIDE * HEAD_DIM)
    pos2 = cmp_pos.reshape(2, 2, CMP_STRIDE * HEAD_DIM).astype(F32)
    cmp_gain = (k_norm_g[0] * qg[0] * k_fold).reshape(1, HEAD_DIM)
    k_cmp, v_cmp_t = _compress(a2, pos2, cmp_w1.astype(BF16), cmp_w2[0].astype(BF16), cmp_w2[1].T.astype(BF16),
                               cmp_gain)

    xn = _rmsnorm(xf, attn_norm_g[1])
    w_qg = n_w_qg[0]
    qhat = _matmul(xn, jnp.swapaxes(n_w_qg, 1, 2), layer=0, nt=True, n_out=D, epilogue="headnorm", name="nsa_q_proj")
    grow_n = -(-N_BRANCH * hpg // SUBLANES) * SUBLANES
    w_gate = w_qg[:, D:].reshape(D, N_BRANCH, N_KV, hpg).transpose(2, 1, 3, 0).reshape(N_KV, N_BRANCH * hpg, D)
    w_gate = jnp.zeros((N_KV, grow_n, D), F32).at[:, :N_BRANCH * hpg].set(w_gate).reshape(N_KV * grow_n, D)
    gates_t = _matmul(w_gate, xn, nt=True, epilogue="sigmoid", out_dtype=F32, tn=1024, name="nsa_gate_proj")
    o = _nsa_attention(qhat, k_cmp, v_cmp_t, k_tok, v_t, gates_t, B=B, S=S, D=D)
    xf = _matmul(o, n_w_out, layer=0, epilogue="resid", extra=xf, out_dtype=F32, tn=256, name="nsa_out_proj")
    xf = _mlp(xf, mlp_norm_g[1], mlp_w_up, mlp_w_down, 1)
    return xf.reshape(B, S, D)
```

```python
import functools

import numpy as np
import jax
import jax.numpy as jnp
from jax import lax
from jax.experimental import pallas as pl
from jax.experimental.pallas import tpu as pltpu

F32 = jnp.float32
BF16 = jnp.bfloat16

EPS = 1e-6
NEG_BIG = -1e30
MASK_VAL = -(2.0 ** 100)
LOG2E = 1.4426950408889634

M_HEADS = 8
GATE_CAP = 15.0
M_CHUNK = 256

N_HEADS = 32
N_KV = 4
HEAD_DIM = 128
N_BRANCH = 3
CMP_BLOCK = 32
CMP_STRIDE = 16
SEL_BLOCK = 64
N_SEL = 16
WINDOW = 512

LANES = 128
SUBLANES = 8
VMEM_LIMIT = 56 * 1024 * 1024


def _params(*sem):
    return pltpu.CompilerParams(dimension_semantics=sem, vmem_limit_bytes=VMEM_LIMIT)


def _dot(a, b):
    return jnp.dot(a, b, preferred_element_type=F32)


def _dot_nt(a, b):
    return lax.dot_general(a, b, (((1,), (1,)), ((), ())), preferred_element_type=F32)


def _split3(x):
    x1 = x.astype(BF16)
    r1 = x - x1.astype(F32)
    x2 = r1.astype(BF16)
    r2 = r1 - x2.astype(F32)
    return x1, x2, r2.astype(BF16)


def _rmsnorm_kernel(x_ref, g_ref, o_ref):
    x = x_ref[...]
    ms = jnp.mean(x * x, axis=-1, keepdims=True)
    o_ref[...] = (x * lax.rsqrt(ms + EPS) * g_ref[...]).astype(o_ref.dtype)


def _rmsnorm(x, g, tm=256):
    T, D = x.shape
    tm = min(tm, T)
    return pl.pallas_call(
        _rmsnorm_kernel,
        grid=(T // tm,),
        in_specs=[pl.BlockSpec((tm, D), lambda i: (i, 0)),
                  pl.BlockSpec((1, D), lambda i: (0, 0))],
        out_specs=pl.BlockSpec((tm, D), lambda i: (i, 0)),
        out_shape=jax.ShapeDtypeStruct((T, D), BF16),
        compiler_params=_params("parallel"),
        name="rmsnorm",
    )(x, g.reshape(1, D).astype(F32))


def _group_rstd(y):
    out = []
    for g in range(y.shape[1] // HEAD_DIM):
        yg = y[:, g * HEAD_DIM:(g + 1) * HEAD_DIM]
        out.append(lax.rsqrt(jnp.mean(yg * yg, axis=-1, keepdims=True) + EPS))
    return out


def _mm_kernel(*refs, epilogue, nk, tn, nt, gain_on, cfg):
    refs = list(refs)
    a_ref, w_ref = refs[:2]
    o_ref = refs[-1]
    g_ref = refs[2] if gain_on is not None else None
    e_ref = refs[-2] if epilogue in ("resid", "kvnorm") else None
    j = pl.program_id(1)
    k = pl.program_id(2)

    def product():
        a = a_ref[...]
        w = w_ref[...]
        if gain_on == "a":
            a = a * g_ref[...]
        elif gain_on == "w" and nt:
            w = w * g_ref[...]
        elif gain_on == "w":
            g = g_ref[...]
            w = jnp.concatenate([w[:, c * LANES:(c + 1) * LANES] * g for c in range(w.shape[1] // LANES)], axis=1)
        a = a.astype(BF16)
        w = w.astype(BF16)
        return _dot_nt(a, w) if nt else _dot(a, w)

    if nk > 1:
        assert epilogue == "resid" and o_ref.dtype == F32

        @pl.when(k == 0)
        def _():
            o_ref[...] = e_ref[...]

        o_ref[...] += product()
        return

    part = product()

    def finish(y):
        if epilogue == "none":
            o_ref[...] = y.astype(o_ref.dtype)
        elif epilogue == "relu2":
            r = jnp.maximum(y, 0.0)
            o_ref[...] = (r * r).astype(o_ref.dtype)
        elif epilogue == "sigmoid":
            o_ref[...] = jax.nn.sigmoid(y).astype(o_ref.dtype)
        elif epilogue == "resid":
            o_ref[...] = (e_ref[...] + y).astype(o_ref.dtype)
        elif epilogue == "headnorm":
            rstd = _group_rstd(y)
            for g, r in enumerate(rstd):
                sl = slice(g * HEAD_DIM, (g + 1) * HEAD_DIM)
                o_ref[:, sl] = (y[:, sl] * r).astype(o_ref.dtype)
        elif epilogue == "kvnorm":
            slot = (j * tn) // cfg["kv_dim"]
            is_norm = functools.reduce(jnp.logical_or, [slot == s for s in cfg["norm_slots"]])
            rstd = _group_rstd(y)
            gain = e_ref[...]
            for g, r in enumerate(rstd):
                sl = slice(g * HEAD_DIM, (g + 1) * HEAD_DIM)
                mult = jnp.where(is_norm, r, 1.0)
                o_ref[:, sl] = (y[:, sl] * mult * gain[:, sl]).astype(o_ref.dtype)
        elif epilogue == "mlstm_in":
            col0 = j * tn
            qk, d = cfg["m_qk"], cfg["d_model"]
            is_k = jnp.logical_and(col0 >= qk, col0 < 2 * qk)
            is_o = col0 >= 2 * qk + d
            lin = y * jnp.where(is_k, cfg["k_scale"], 1.0)
            o_ref[...] = jnp.where(is_o, jax.nn.sigmoid(y), lin).astype(o_ref.dtype)
        else:
            raise ValueError(epilogue)

    finish(part)


def _matmul(a, w, *, layer=None, nt=False, n_out=None, col_map=None, epilogue="none", extra=None, gain=None,
            gain_on=None, out_dtype=BF16, tm=2048, tn=512, tk=4096, cfg=None, name="matmul"):
    M, K = a.shape
    wshape = w.shape[1:] if layer is not None else w.shape
    N = n_out if n_out is not None else (wshape[0] if nt else wshape[1])
    tm, tn, tk = min(tm, M), min(tn, N), min(tk, K)
    assert M % tm == 0 and N % tn == 0 and K % tk == 0
    nk = K // tk
    cm = col_map if col_map is not None else (lambda j: j)
    if layer is not None and nt:
        w_spec = pl.BlockSpec((None, tn, tk), lambda i, j, k: (layer, cm(j), k))
    elif layer is not None:
        w_spec = pl.BlockSpec((None, tk, tn), lambda i, j, k: (layer, k, cm(j)))
    elif nt:
        w_spec = pl.BlockSpec((tn, tk), lambda i, j, k: (cm(j), k))
    else:
        w_spec = pl.BlockSpec((tk, tn), lambda i, j, k: (k, cm(j)))
    a_mode = dict(pipeline_mode=pl.Buffered(1)) if (nk == 1 and N // tn > 1) else {}
    in_specs = [pl.BlockSpec((tm, tk), lambda i, j, k: (i, k), **a_mode), w_spec]
    args = [a, w]
    if gain_on is not None:
        if gain_on == "w" and not nt:
            in_specs.append(pl.BlockSpec((tk, LANES), lambda i, j, k: (k, 0)))
            args.append(jnp.broadcast_to(gain.astype(F32).reshape(K, 1), (K, LANES)))
        else:
            in_specs.append(pl.BlockSpec((1, tk), lambda i, j, k: (0, k)))
            args.append(gain.astype(F32).reshape(1, K))
    if epilogue == "resid":
        e_mode = dict(pipeline_mode=pl.Buffered(1)) if nk > 1 else {}
        in_specs.append(pl.BlockSpec((tm, tn), lambda i, j, k: (i, j), **e_mode))
        args.append(extra)
    elif epilogue == "kvnorm":
        in_specs.append(pl.BlockSpec((1, tn), lambda i, j, k: (0, j)))
        args.append(extra)
    return pl.pallas_call(
        functools.partial(_mm_kernel, epilogue=epilogue, nk=nk, tn=tn, nt=nt, gain_on=gain_on, cfg=cfg),
        grid=(M // tm, N // tn, nk),
        in_specs=in_specs,
        out_specs=pl.BlockSpec((tm, tn), lambda i, j, k: (i, j)),
        out_shape=jax.ShapeDtypeStruct((M, N), out_dtype),
        compiler_params=_params("parallel", "parallel", "arbitrary"),
        name=name,
    )(*args)


def _log_sigmoid(x):
    return jnp.minimum(x, 0.0) - jnp.log(1.0 + jnp.exp(-jnp.abs(x)))


def _gates_kernel(xn_ref, wg_ref, wgt_ref, brow_ref, bcol_ref, gcol_ref, grow_ref):
    xn = xn_ref[...]
    pre_c = _dot(xn, wg_ref[...]) + brow_ref[...]
    g_c = GATE_CAP * jnp.tanh(pre_c / GATE_CAP)
    col = lax.broadcasted_iota(jnp.int32, g_c.shape, 1)
    gcol_ref[...] = jnp.where(col < M_HEADS, g_c, _log_sigmoid(g_c))
    pre_r = _dot_nt(wgt_ref[...], xn) + bcol_ref[...]
    g_r = GATE_CAP * jnp.tanh(pre_r / GATE_CAP)
    row = lax.broadcasted_iota(jnp.int32, g_r.shape, 0)
    grow_ref[...] = jnp.where(row < M_HEADS, g_r, _log_sigmoid(g_r))


def _mlstm_gates(xn, w_g, b_g, tm=512):
    T, D = xn.shape
    ng = 2 * M_HEADS
    tm = min(tm, T)
    wg = jnp.zeros((D, LANES), F32).at[:, :ng].set(w_g).astype(BF16)
    wgt = w_g.T.astype(BF16)
    brow = jnp.zeros((1, LANES), F32).at[0, :ng].set(b_g)
    bcol = b_g.reshape(ng, 1).astype(F32)
    return pl.pallas_call(
        _gates_kernel,
        grid=(T // tm,),
        in_specs=[pl.BlockSpec((tm, D), lambda i: (i, 0)),
                  pl.BlockSpec((D, LANES), lambda i: (0, 0)),
                  pl.BlockSpec((ng, D), lambda i: (0, 0)),
                  pl.BlockSpec((1, LANES), lambda i: (0, 0)),
                  pl.BlockSpec((ng, 1), lambda i: (0, 0))],
        out_specs=[pl.BlockSpec((tm, LANES), lambda i: (i, 0)),
                   pl.BlockSpec((ng, tm), lambda i: (0, i))],
        out_shape=[jax.ShapeDtypeStruct((T, LANES), F32),
                   jax.ShapeDtypeStruct((ng, T), F32)],
        compiler_params=_params("parallel"),
        name="mlstm_gates",
    )(xn, wg, wgt, brow, bcol)


def _mlstm_kernel(q_ref, k_ref, v_ref, o_ref, gc_ref, gr_ref, hg_ref, out_ref,
                  c_ref, n_ref, m_ref, *, dk, dv):
    L = q_ref.shape[0]

    @pl.when(pl.program_id(1) == 0)
    def _():
        c_ref[...] = jnp.zeros_like(c_ref)
        n_ref[...] = jnp.zeros_like(n_ref)
        m_ref[...] = jnp.zeros_like(m_ref)

    row = lax.broadcasted_iota(jnp.int32, (L, L), 0)
    col = lax.broadcasted_iota(jnp.int32, (L, L), 1)
    causal = col <= row
    tril = jnp.where(causal, 1.0, 0.0).astype(BF16)
    triu = jnp.where(row <= col, 1.0, 0.0).astype(BF16)

    gc = gc_ref[...]
    gr = gr_ref[...]
    c1, c2, c3 = _split3(gc)
    b_col_all = _dot(tril, c1) + _dot(tril, c2) + _dot(tril, c3)
    r1, r2, r3 = _split3(gr)
    b_row_all = _dot(r1, triu) + _dot(r2, triu) + _dot(r3, triu)

    for h in range(M_HEADS):
        li_row = gr[h:h + 1, :]
        li_col = gc[:, h:h + 1]
        b_row = b_row_all[M_HEADS + h:M_HEADS + h + 1, :]
        b_col = b_col_all[:, M_HEADS + h:M_HEADS + h + 1]
        m_prev = m_ref[h]

        log_d = jnp.where(causal, b_col - b_row + li_row, -jnp.inf)
        log_inter = b_col + m_prev
        m_t = jnp.maximum(log_inter, jnp.max(log_d, axis=1, keepdims=True))
        w_intra = jnp.exp(log_d - m_t)
        w_inter = jnp.exp(log_inter - m_t)

        qh = q_ref[:, h * dk:(h + 1) * dk]
        kh = k_ref[:, h * dk:(h + 1) * dk]
        vh = v_ref[:, h * dv:(h + 1) * dv]
        c_old = c_ref[h]
        n_old = n_ref[h]

        s = _dot_nt(qh, kh) * w_intra
        num = w_inter * _dot(qh, c_old.astype(BF16)) + _dot(s.astype(BF16), vh)
        qn = jnp.sum(qh.astype(F32) * n_old, axis=1, keepdims=True)
        den = w_inter * qn + jnp.sum(s, axis=1, keepdims=True)
        hval = num / jnp.maximum(jnp.abs(den), jnp.exp(-m_t))

        ms = jnp.mean(hval * hval, axis=1, keepdims=True)
        hn = hval * lax.rsqrt(ms + EPS) * hg_ref[:, h * dv:(h + 1) * dv]
        out_ref[:, h * dv:(h + 1) * dv] = (o_ref[:, h * dv:(h + 1) * dv].astype(F32) * hn).astype(out_ref.dtype)

        b_last = b_col[L - 1:L, :]
        log_w = b_last - b_col + li_col
        m_new = jnp.maximum(b_last + m_prev, jnp.max(log_w, axis=0, keepdims=True))
        wk = jnp.exp(log_w - m_new) * kh.astype(F32)
        decay = jnp.exp(b_last + m_prev - m_new)
        c_ref[h] = decay * c_old + _dot(wk.T.astype(BF16), vh)
        n_ref[h] = decay * n_old + jnp.sum(wk, axis=0, keepdims=True)
        m_ref[h] = m_new


def _mlstm(proj, gcol, grow, head_g, *, B, S, D):
    T = B * S
    dv = D // M_HEADS
    dk = dv // 2
    qk = M_HEADS * dk
    L = min(M_CHUNK, S)
    nc = S // L
    rb = lambda b, c: b * nc + c
    assert D % qk == 0
    return pl.pallas_call(
        functools.partial(_mlstm_kernel, dk=dk, dv=dv),
        grid=(B, nc),
        in_specs=[pl.BlockSpec((L, qk), lambda b, c: (rb(b, c), 0)),
                  pl.BlockSpec((L, qk), lambda b, c: (rb(b, c), 1)),
                  pl.BlockSpec((L, D), lambda b, c: (rb(b, c), (2 * qk) // D)),
                  pl.BlockSpec((L, D), lambda b, c: (rb(b, c), (2 * qk) // D + 1)),
                  pl.BlockSpec((L, LANES), lambda b, c: (rb(b, c), 0)),
                  pl.BlockSpec((2 * M_HEADS, L), lambda b, c: (0, rb(b, c))),
                  pl.BlockSpec((1, D), lambda b, c: (0, 0))],
        out_specs=pl.BlockSpec((L, D), lambda b, c: (rb(b, c), 0)),
        out_shape=jax.ShapeDtypeStruct((T, D), BF16),
        scratch_shapes=[pltpu.VMEM((M_HEADS, dk, dv), F32),
                        pltpu.VMEM((M_HEADS, 1, dk), F32),
                        pltpu.VMEM((M_HEADS, 1, 1), F32)],
        compiler_params=_params("parallel", "arbitrary"),
        name="mlstm",
    )(proj, proj, proj, proj, gcol, grow, head_g.reshape(1, D).astype(F32))


def _compress_kernel(a_ref, pos_ref, w1_ref, w2k_ref, w2vt_ref, gain_ref, ok_ref, ovt_ref):
    half = a_ref.shape[2]
    ncp = a_ref.shape[1]

    def hidden(j):
        a = a_ref[j].astype(F32)
        lo = (a + pos_ref[j, 0:1, :]).astype(BF16)
        hi = (a + pos_ref[j, 1:2, :]).astype(BF16)
        u = _dot(lo, w1_ref[j, 0:half, :])
        v = _dot(hi, w1_ref[j, half:2 * half, :])
        hid = u + pltpu.roll(v, shift=ncp - 1, axis=0)
        c0 = float(np.sqrt(2.0 / np.pi))
        cdf = 0.5 * (1.0 + jnp.tanh(c0 * (hid + 0.044715 * (hid * hid * hid))))
        return (hid * cdf).astype(BF16)

    yk = _dot(hidden(0), w2k_ref[...])
    rstd = lax.rsqrt(jnp.mean(yk * yk, axis=-1, keepdims=True) + EPS)
    ok_ref[...] = (yk * rstd * gain_ref[...]).astype(ok_ref.dtype)
    ovt_ref[...] = _dot_nt(w2vt_ref[...], hidden(1)).astype(ovt_ref.dtype)


def _compress(a2, pos2, w1, w2k, w2vt, gain):
    _, n, ncp, half = a2.shape
    hid = w1.shape[2]
    return pl.pallas_call(
        _compress_kernel,
        grid=(n,),
        in_specs=[pl.BlockSpec((2, None, ncp, half), lambda i: (0, i, 0, 0)),
                  pl.BlockSpec((2, 2, half), lambda i: (0, 0, 0)),
                  pl.BlockSpec((2, 2 * half, hid), lambda i: (0, 0, 0)),
                  pl.BlockSpec((hid, HEAD_DIM), lambda i: (0, 0)),
                  pl.BlockSpec((HEAD_DIM, hid), lambda i: (0, 0)),
                  pl.BlockSpec((1, HEAD_DIM), lambda i: (0, 0))],
        out_specs=[pl.BlockSpec((None, ncp, HEAD_DIM), lambda i: (i, 0, 0)),
                   pl.BlockSpec((None, HEAD_DIM, ncp), lambda i: (i, 0, 0))],
        out_shape=[jax.ShapeDtypeStruct((n, ncp, HEAD_DIM), BF16),
                   jax.ShapeDtypeStruct((n, HEAD_DIM, ncp), BF16)],
        compiler_params=_params("parallel"),
        name="nsa_compress",
    )(a2, pos2, w1, w2k, w2vt, gain)


def _nsa_kernel(q_ref, kc_ref, vct_ref, ks_ref, vst_ref, kw_ref, vwt_ref, gate_ref, mt_ref, o_ref,
                qa_ref, out_acc, acc_ref, m_ref, l_ref, s_ref, *, tq, hpg, rank_w):
    qi = pl.program_id(2)
    q0 = pl.multiple_of(qi * tq, tq)
    hd = HEAD_DIM
    t_lane = q0 + lax.broadcasted_iota(jnp.int32, (1, tq), 1)

    def gate(br, h):
        r = br * hpg + h
        return gate_ref[r:r + 1, :]

    def q_head(h):
        return q_ref[:, h * hd:(h + 1) * hd]

    ncp = kc_ref.shape[0]
    c_sub = lax.broadcasted_iota(jnp.int32, (ncp, 1), 0)
    cmp_valid = (c_sub * CMP_STRIDE + (CMP_BLOCK - 1)) <= t_lane
    kc = kc_ref[...]
    vct = vct_ref[...]
    psum = jnp.zeros((ncp, tq), F32)
    for h in range(hpg):
        s = jnp.where(cmp_valid, _dot_nt(kc, q_head(h)), NEG_BIG)
        e = jnp.exp2(s - jnp.max(s, axis=0, keepdims=True))
        p = jnp.where(cmp_valid, e * (1.0 / jnp.sum(e, axis=0, keepdims=True)), 0.0)
        psum = psum + p
        out_acc[h] = gate(0, h) * _dot(vct, p.astype(BF16))

    mt = mt_ref[...]
    n_sb = mt.shape[0]
    p1, p2, p3 = _split3(psum)
    p_blk = _dot(mt, p1) + _dot(mt, p2) + _dot(mt, p3)
    blk = lax.broadcasted_iota(jnp.int32, (n_sb, 1), 0)
    cur = t_lane // SEL_BLOCK
    forced = jnp.logical_or(blk == cur, blk == 0)
    val = jnp.where(forced, jnp.inf, jnp.where(blk <= cur, p_blk, -jnp.inf))

    sub = lax.broadcasted_iota(jnp.int32, (SUBLANES, 1), 0)
    n_rb = n_sb // SUBLANES
    top = float(min(N_SEL, n_sb))
    bias_cols = []
    for c in range(tq // rank_w):
        v = val[:, c * rank_w:(c + 1) * rank_w]
        rows = [v[rb * SUBLANES:(rb + 1) * SUBLANES, :] for rb in range(n_rb)]
        cnt = [jnp.zeros((SUBLANES, rank_w), F32) for _ in range(n_rb)]
        for i in range(n_sb):
            vi = v[i:i + 1, :]
            for rb in range(n_rb):
                lo = rb * SUBLANES
                ge = lambda: jnp.where(vi >= rows[rb], 1.0, 0.0)
                gt = lambda: jnp.where(vi > rows[rb], 1.0, 0.0)
                if lo > i:
                    beats = ge()
                elif lo + SUBLANES - 1 <= i:
                    beats = gt()
                else:
                    beats = jnp.where(sub > (i - lo), ge(), gt())
                cnt[rb] = cnt[rb] + beats
        cnt = jnp.concatenate(cnt, axis=0)
        cur_c = cur[:, c * rank_w:(c + 1) * rank_w]
        picked = jnp.where(cnt < top, jnp.where(blk <= cur_c, 0.0, MASK_VAL), MASK_VAL)
        bias_cols.append(picked)
    sel_bias = jnp.concatenate(bias_cols, axis=1)
    sel_bias = jnp.concatenate([sel_bias, jnp.zeros((LANES - n_sb, tq), F32)], axis=0)
    sb_t = sel_bias.T.astype(BF16)
    for h in range(hpg):
        qa_ref[h, :, 0:hd] = q_head(h)
        qa_ref[h, :, hd:2 * hd] = sb_t

    def flash_reset():
        m_ref[...] = jnp.full(m_ref.shape, NEG_BIG, F32)
        l_ref[...] = jnp.zeros_like(l_ref)
        acc_ref[...] = jnp.zeros_like(acc_ref)

    def flash_step(k_t, vt_t, qsel, mask):
        tile_max = []
        for h in range(hpg):
            s = _dot_nt(k_t, qsel(h))
            if mask is not None:
                s = jnp.where(mask, s, MASK_VAL)
            s_ref[h] = s
            tile_max.append(jnp.max(s, axis=0, keepdims=True))
        for h in range(hpg):
            m_old = m_ref[h]
            m_new = jnp.maximum(m_old, tile_max[h])
            alpha = jnp.exp2(m_old - m_new)
            p = jnp.exp2(s_ref[h] - m_new)
            l_ref[h] = alpha * l_ref[h] + jnp.sum(p, axis=0, keepdims=True)
            acc_ref[h] = alpha * acc_ref[h] + _dot(vt_t, p.astype(BF16))
            m_ref[h] = m_new

    def flash_out(h):
        return acc_ref[h] * (1.0 / l_ref[h])

    tk = tq
    k_sub = lax.broadcasted_iota(jnp.int32, (tk, 1), 0)
    q_lane = lax.broadcasted_iota(jnp.int32, (1, tq), 1)
    causal = k_sub <= q_lane
    lane_j = lax.broadcasted_iota(jnp.int32, (1, LANES), 1)
    key_blk = k_sub // SEL_BLOCK

    def ks_aug(k0):
        onehot = jnp.where(lane_j == key_blk + k0 // SEL_BLOCK, 1.0, 0.0).astype(BF16)
        return jnp.concatenate([ks_ref[pl.ds(k0, tk), :], onehot], axis=1)

    qa_head = lambda h: qa_ref[h]

    flash_reset()

    def sel_body(kt, carry):
        k0 = pl.multiple_of(kt * tk, tk)
        flash_step(ks_aug(k0), vst_ref[:, pl.ds(k0, tk)], qa_head, None)
        return carry

    lax.fori_loop(0, qi, sel_body, 0)
    flash_step(ks_aug(q0), vst_ref[:, pl.ds(q0, tk)], qa_head, causal)
    for h in range(hpg):
        out_acc[h] = out_acc[h] + gate(1, h) * flash_out(h)

    flash_reset()
    flash_step(kw_ref[pl.ds(q0, tk), :], vwt_ref[:, pl.ds(q0, tk)], q_head, causal)
    d = 1
    while d * tq - (tk - 1) < WINDOW:
        all_valid = d * tq + (tq - 1) < WINDOW
        mask = None if all_valid else (q_lane + d * tq - k_sub) < WINDOW

        @pl.when(qi >= d)
        def _(d=d, mask=mask):
            k0 = pl.multiple_of(q0 - d * tq, tk)
            flash_step(kw_ref[pl.ds(k0, tk), :], vwt_ref[:, pl.ds(k0, tk)], q_head, mask)
        d += 1

    for h in range(hpg):
        o = out_acc[h] + gate(2, h) * flash_out(h)
        o_ref[:, h * hd:(h + 1) * hd] = o.T.astype(o_ref.dtype)


def _cmp_to_sel_t(ncp, n_cmp, n_sb):
    c0 = np.arange(ncp)[None, :] * CMP_STRIDE
    s0 = np.arange(n_sb)[:, None] * SEL_BLOCK
    ov = np.minimum(c0 + CMP_BLOCK, s0 + SEL_BLOCK) - np.maximum(c0, s0)
    w = np.maximum(ov, 0) / CMP_BLOCK
    w = w * (np.arange(ncp)[None, :] < n_cmp)
    return jnp.asarray(w, dtype=BF16)


def _nsa_attention(qhat, k_cmp, v_cmp_t, k_tok, v_t, gates_t, *, B, S, D, tq=512):
    T = B * S
    hpg = N_HEADS // N_KV
    gw = hpg * HEAD_DIM
    tq = min(tq, S)
    nq = S // tq
    ncp = k_cmp.shape[1]
    n_cmp = (S - CMP_BLOCK) // CMP_STRIDE + 1
    n_sb = S // SEL_BLOCK
    assert n_sb % SUBLANES == 0 and n_sb <= LANES
    mt = _cmp_to_sel_t(ncp, n_cmp, n_sb)
    grow = gates_t.shape[0] // N_KV
    return pl.pallas_call(
        functools.partial(_nsa_kernel, tq=tq, hpg=hpg, rank_w=min(256, tq)),
        grid=(B, N_KV, nq),
        in_specs=[pl.BlockSpec((tq, gw), lambda b, g, i: (b * nq + i, g)),
                  pl.BlockSpec((None, ncp, HEAD_DIM), lambda b, g, i: (b * N_KV + g, 0, 0)),
                  pl.BlockSpec((None, HEAD_DIM, ncp), lambda b, g, i: (b * N_KV + g, 0, 0)),
                  pl.BlockSpec((S, HEAD_DIM), lambda b, g, i: (b, 2 * N_KV + g)),
                  pl.BlockSpec((HEAD_DIM, S), lambda b, g, i: (g, b)),
                  pl.BlockSpec((S, HEAD_DIM), lambda b, g, i: (b, 3 * N_KV + g)),
                  pl.BlockSpec((HEAD_DIM, S), lambda b, g, i: (N_KV + g, b)),
                  pl.BlockSpec((grow, tq), lambda b, g, i: (g, b * nq + i)),
                  pl.BlockSpec((n_sb, ncp), lambda b, g, i: (0, 0))],
        out_specs=pl.BlockSpec((tq, gw), lambda b, g, i: (b * nq + i, g)),
        out_shape=jax.ShapeDtypeStruct((T, D), BF16),
        scratch_shapes=[pltpu.VMEM((hpg, tq, 2 * HEAD_DIM), BF16),
                        pltpu.VMEM((hpg, HEAD_DIM, tq), F32),
                        pltpu.VMEM((hpg, HEAD_DIM, tq), F32),
                        pltpu.VMEM((hpg, 1, tq), F32),
                        pltpu.VMEM((hpg, 1, tq), F32),
                        pltpu.VMEM((hpg, tq, tq), F32)],
        compiler_params=_params("parallel", "parallel", "parallel"),
        name="nsa_attention",
    )(qhat, k_cmp, v_cmp_t, k_tok, v_t, k_tok, v_t, gates_t, mt)


def _mlp(x, g, w_up, w_down, layer):
    xn = _rmsnorm(x, g)
    h = _matmul(xn, w_up, layer=layer, epilogue="relu2", name="mlp_up")
    return _matmul(h, w_down, layer=layer, epilogue="resid", extra=x, out_dtype=F32, tm=2048, tn=1024, tk=1024,
                   name="mlp_down")


def kernel(x, attn_norm_g, mlp_norm_g, m_w_in, m_b_gate, m_head_g, m_w_out, kv_norm_g, w_kv, k_norm_g,
           cmp_pos, cmp_w1, cmp_w2, n_w_qg, q_norm_g, n_w_out, mlp_w_up, mlp_w_down):
    B, S, D = x.shape
    T = B * S
    xf = x.reshape(T, D)
    dv = D // M_HEADS
    dk = dv // 2
    qk = M_HEADS * dk
    nq = 2 * qk + 2 * D
    kv_dim = N_KV * HEAD_DIM
    hpg = N_HEADS // N_KV
    k_fold = (HEAD_DIM ** -0.5) * LOG2E

    xn = _rmsnorm(xf, attn_norm_g[0])
    proj = _matmul(xn, jnp.swapaxes(m_w_in, 1, 2), layer=0, nt=True, n_out=nq, epilogue="mlstm_in", tn=min(512, qk),
                   cfg=dict(m_qk=qk, d_model=D, k_scale=float(dk ** -0.5)), name="mlstm_in_proj")
    gcol, grow = _mlstm_gates(xn, m_w_in[0, :, nq:], m_b_gate[0])
    hg = _mlstm(proj, gcol, grow, m_head_g[0], B=B, S=S, D=D)
    xf = _matmul(hg, m_w_out, layer=0, epilogue="resid", extra=xf, out_dtype=F32, tn=256, name="mlstm_out_proj")
    xf = _mlp(xf, mlp_norm_g[0], mlp_w_up, mlp_w_down, 0)

    qg = q_norm_g[0]
    ones = jnp.ones((kv_dim,), F32)
    kv_gain = jnp.concatenate([
        ones, ones,
        jnp.tile(k_norm_g[1] * qg[1] * k_fold, N_KV),
        jnp.tile(k_norm_g[2] * qg[2] * k_fold, N_KV)]).reshape(1, 4 * kv_dim)
    xh = _rmsnorm(xf, jnp.ones((D,), F32))
    k_tok = _matmul(xh, w_kv, n_out=4 * kv_dim, col_map=lambda j: j + j // 3, epilogue="kvnorm", extra=kv_gain,
                    gain=kv_norm_g, gain_on="w", tn=kv_dim, cfg=dict(kv_dim=kv_dim, norm_slots=(2, 3)),
                    name="nsa_kv_proj")
    w_vt = jnp.concatenate([w_kv[:, 3 * kv_dim:4 * kv_dim], w_kv[:, 5 * kv_dim:6 * kv_dim]], axis=1).T
    v_t = _matmul(w_vt, xh, nt=True, gain=kv_norm_g, gain_on="a", tm=kv_dim, tn=1024,
                  name="nsa_vt_proj")
    ncp = S // CMP_STRIDE
    a2 = k_tok[:, :2 * kv_dim].reshape(B, ncp, CMP_STRIDE, 2, N_KV, HEAD_DIM)
    a2 = a2.transpose(3, 0, 4, 1, 2, 5).reshape(2, B * N_KV, ncp, CMP_STRIDE * HEAD_DIM)
    pos2 = cmp_pos.reshape(2, 2, CMP_STRIDE * HEAD_DIM).astype(F32)
    cmp_gain = (k_norm_g[0] * qg[0] * k_fold).reshape(1, HEAD_DIM)
    k_cmp, v_cmp_t = _compress(a2, pos2, cmp_w1.astype(BF16), cmp_w2[0].astype(BF16), cmp_w2[1].T.astype(BF16),
                               cmp_gain)

    w_qg = n_w_qg[0]
    qhat = _matmul(xh, jnp.swapaxes(n_w_qg, 1, 2), layer=0, nt=True, n_out=D, epilogue="headnorm",
                   gain=attn_norm_g[1], gain_on="w", name="nsa_q_proj")
    grow_n = -(-N_BRANCH * hpg // SUBLANES) * SUBLANES
    w_gate = w_qg[:, D:].reshape(D, N_BRANCH, N_KV, hpg).transpose(2, 1, 3, 0).reshape(N_KV, N_BRANCH * hpg, D)
    w_gate = jnp.zeros((N_KV, grow_n, D), F32).at[:, :N_BRANCH * hpg].set(w_gate).reshape(N_KV * grow_n, D)
    gates_t = _matmul(w_gate, xh, nt=True, epilogue="sigmoid", gain=attn_norm_g[1], gain_on="a", out_dtype=F32,
                      tn=1024, name="nsa_gate_proj")
    o = _nsa_attention(qhat, k_cmp, v_cmp_t, k_tok, v_t, gates_t, B=B, S=S, D=D)
    xf = _matmul(o, n_w_out, layer=0, epilogue="resid", extra=xf, out_dtype=F32, tn=256, name="nsa_out_proj")
    xf = _mlp(xf, mlp_norm_g[1], mlp_w_up, mlp_w_down, 1)
    return xf.reshape(B, S, D)
```

```python
import functools

import numpy as np
import jax
import jax.numpy as jnp
from jax import lax
from jax.experimental import pallas as pl
from jax.experimental.pallas import tpu as pltpu

F32 = jnp.float32
BF16 = jnp.bfloat16

EPS = 1e-6
NEG_BIG = -1e30
MASK_VAL = -(2.0 ** 100)
LOG2E = 1.4426950408889634

M_HEADS = 8
GATE_CAP = 15.0
M_CHUNK = 256

N_HEADS = 32
N_KV = 4
HEAD_DIM = 128
N_BRANCH = 3
CMP_BLOCK = 32
CMP_STRIDE = 16
SEL_BLOCK = 64
N_SEL = 16
WINDOW = 512

LANES = 128
SUBLANES = 8
VMEM_LIMIT = 56 * 1024 * 1024


def _params(*sem):
    return pltpu.CompilerParams(dimension_semantics=sem, vmem_limit_bytes=VMEM_LIMIT)


def _dot(a, b):
    return jnp.dot(a, b, preferred_element_type=F32)


def _dot_nt(a, b):
    return lax.dot_general(a, b, (((1,), (1,)), ((), ())), preferred_element_type=F32)


def _split3(x):
    x1 = x.astype(BF16)
    r1 = x - x1.astype(F32)
    x2 = r1.astype(BF16)
    r2 = r1 - x2.astype(F32)
    return x1, x2, r2.astype(BF16)


def _rmsnorm_kernel(x_ref, g_ref, o_ref):
    x = x_ref[...]
    ms = jnp.mean(x * x, axis=-1, keepdims=True)
    o_ref[...] = (x * lax.rsqrt(ms + EPS) * g_ref[...]).astype(o_ref.dtype)


def _rmsnorm(x, g, tm=256):
    T, D = x.shape
    tm = min(tm, T)
    return pl.pallas_call(
        _rmsnorm_kernel,
        grid=(T // tm,),
        in_specs=[pl.BlockSpec((tm, D), lambda i: (i, 0)),
                  pl.BlockSpec((1, D), lambda i: (0, 0))],
        out_specs=pl.BlockSpec((tm, D), lambda i: (i, 0)),
        out_shape=jax.ShapeDtypeStruct((T, D), BF16),
        compiler_params=_params("parallel"),
        name="rmsnorm",
    )(x, g.reshape(1, D).astype(F32))


def _group_rstd(y):
    out = []
    for g in range(y.shape[1] // HEAD_DIM):
        yg = y[:, g * HEAD_DIM:(g + 1) * HEAD_DIM]
        out.append(lax.rsqrt(jnp.mean(yg * yg, axis=-1, keepdims=True) + EPS))
    return out


def _mm_kernel(*refs, epilogue, nk, tn, nt, gain_on, cfg):
    refs = list(refs)
    a_ref, w_ref = refs[:2]
    n_in = 2 + (gain_on is not None) + (epilogue in ("resid", "kvnorm"))
    g_ref = refs[2] if gain_on is not None else None
    e_ref = refs[n_in - 1] if epilogue in ("resid", "kvnorm") else None
    o_ref = refs[n_in]
    j = pl.program_id(1)
    k = pl.program_id(2)

    def product():
        a = a_ref[...]
        w = w_ref[...]
        if gain_on == "a":
            a = a * g_ref[...]
        elif gain_on == "w" and nt:
            w = w * g_ref[...]
        elif gain_on == "w":
            g = g_ref[...]
            w = jnp.concatenate([w[:, c * LANES:(c + 1) * LANES] * g for c in range(w.shape[1] // LANES)], axis=1)
        a = a.astype(BF16)
        w = w.astype(BF16)
        return _dot_nt(a, w) if nt else _dot(a, w)

    if nk > 1:
        assert epilogue == "resid" and o_ref.dtype == F32

        @pl.when(k == 0)
        def _():
            o_ref[...] = e_ref[...]

        o_ref[...] += product()
        return

    part = product()

    def finish(y):
        if epilogue == "none":
            o_ref[...] = y.astype(o_ref.dtype)
        elif epilogue == "relu2":
            r = jnp.maximum(y, 0.0)
            o_ref[...] = (r * r).astype(o_ref.dtype)
        elif epilogue == "sigmoid":
            o_ref[...] = jax.nn.sigmoid(y).astype(o_ref.dtype)
        elif epilogue == "resid":
            o_ref[...] = (e_ref[...] + y).astype(o_ref.dtype)
        elif epilogue == "headnorm":
            rstd = _group_rstd(y)
            for g, r in enumerate(rstd):
                sl = slice(g * HEAD_DIM, (g + 1) * HEAD_DIM)
                o_ref[:, sl] = (y[:, sl] * r).astype(o_ref.dtype)
        elif epilogue == "kvnorm":
            slot = (j * tn) // cfg["kv_dim"]
            is_norm = functools.reduce(jnp.logical_or, [slot == s for s in cfg["norm_slots"]])
            rstd = _group_rstd(y)
            gain = e_ref[...]
            for g, r in enumerate(rstd):
                sl = slice(g * HEAD_DIM, (g + 1) * HEAD_DIM)
                mult = jnp.where(is_norm, r, 1.0)
                o_ref[:, sl] = (y[:, sl] * mult * gain[:, sl]).astype(o_ref.dtype)

            blk_ref, y_scr = refs[n_in + 1], refs[n_in + 2]
            rows = y.shape[0] // CMP_STRIDE

            @pl.when(slot < 2)
            def _():
                for g in range(y.shape[1] // HEAD_DIM):
                    y_scr[g] = y[:, g * HEAD_DIM:(g + 1) * HEAD_DIM]
                    for r in range(CMP_STRIDE):
                        piece = y_scr[g, pl.ds(r, rows, stride=CMP_STRIDE), :]
                        blk_ref[g, :, r * HEAD_DIM:(r + 1) * HEAD_DIM] = piece.astype(blk_ref.dtype)
        elif epilogue == "mlstm_in":
            col0 = j * tn
            qk, d = cfg["m_qk"], cfg["d_model"]
            is_k = jnp.logical_and(col0 >= qk, col0 < 2 * qk)
            is_o = col0 >= 2 * qk + d
            lin = y * jnp.where(is_k, cfg["k_scale"], 1.0)
            o_ref[...] = jnp.where(is_o, jax.nn.sigmoid(y), lin).astype(o_ref.dtype)
        else:
            raise ValueError(epilogue)

    finish(part)


def _matmul(a, w, *, layer=None, nt=False, n_out=None, col_map=None, epilogue="none", extra=None, gain=None,
            gain_on=None, out_dtype=BF16, tm=2048, tn=512, tk=4096, cfg=None, name="matmul"):
    M, K = a.shape
    wshape = w.shape[1:] if layer is not None else w.shape
    N = n_out if n_out is not None else (wshape[0] if nt else wshape[1])
    tm, tn, tk = min(tm, M), min(tn, N), min(tk, K)
    assert M % tm == 0 and N % tn == 0 and K % tk == 0
    nk = K // tk
    cm = col_map if col_map is not None else (lambda j: j)
    if layer is not None and nt:
        w_spec = pl.BlockSpec((None, tn, tk), lambda i, j, k: (layer, cm(j), k))
    elif layer is not None:
        w_spec = pl.BlockSpec((None, tk, tn), lambda i, j, k: (layer, k, cm(j)))
    elif nt:
        w_spec = pl.BlockSpec((tn, tk), lambda i, j, k: (cm(j), k))
    else:
        w_spec = pl.BlockSpec((tk, tn), lambda i, j, k: (k, cm(j)))
    a_mode = dict(pipeline_mode=pl.Buffered(1)) if (nk == 1 and N // tn > 1) else {}
    in_specs = [pl.BlockSpec((tm, tk), lambda i, j, k: (i, k), **a_mode), w_spec]
    args = [a, w]
    if gain_on is not None:
        if gain_on == "w" and not nt:
            in_specs.append(pl.BlockSpec((tk, LANES), lambda i, j, k: (k, 0)))
            args.append(jnp.broadcast_to(gain.astype(F32).reshape(K, 1), (K, LANES)))
        else:
            in_specs.append(pl.BlockSpec((1, tk), lambda i, j, k: (0, k)))
            args.append(gain.astype(F32).reshape(1, K))
    if epilogue == "resid":
        e_mode = dict(pipeline_mode=pl.Buffered(1)) if nk > 1 else {}
        in_specs.append(pl.BlockSpec((tm, tn), lambda i, j, k: (i, j), **e_mode))
        args.append(extra)
    elif epilogue == "kvnorm":
        in_specs.append(pl.BlockSpec((1, tn), lambda i, j, k: (0, j)))
        args.append(extra)
    out_specs = pl.BlockSpec((tm, tn), lambda i, j, k: (i, j))
    out_shape = jax.ShapeDtypeStruct((M, N), out_dtype)
    scratch = []
    semantics = ("parallel", "parallel", "arbitrary")
    if epilogue == "kvnorm":
        seq, groups = cfg["seq"], tn // HEAD_DIM
        assert seq % tm == 0 and tm % CMP_STRIDE == 0 and tn == cfg["kv_dim"]
        per_seq = seq // tm
        out_specs = [out_specs, pl.BlockSpec((None, None, groups, tm // CMP_STRIDE, CMP_STRIDE * HEAD_DIM),
                                             lambda i, j, k: (jnp.minimum(j, 1), i // per_seq, 0, i % per_seq, 0))]
        out_shape = [out_shape, jax.ShapeDtypeStruct((2, M // seq, groups, seq // CMP_STRIDE, CMP_STRIDE * HEAD_DIM),
                                                     out_dtype)]
        scratch = [pltpu.VMEM((groups, tm, HEAD_DIM), F32)]
        semantics = ("parallel", "arbitrary", "arbitrary")
    return pl.pallas_call(
        functools.partial(_mm_kernel, epilogue=epilogue, nk=nk, tn=tn, nt=nt, gain_on=gain_on, cfg=cfg),
        grid=(M // tm, N // tn, nk),
        in_specs=in_specs,
        out_specs=out_specs,
        out_shape=out_shape,
        scratch_shapes=scratch,
        compiler_params=_params(*semantics),
        name=name,
    )(*args)


def _log_sigmoid(x):
    return jnp.minimum(x, 0.0) - jnp.log(1.0 + jnp.exp(-jnp.abs(x)))


def _gates_kernel(xn_ref, wg_ref, wgt_ref, brow_ref, bcol_ref, gcol_ref, grow_ref):
    xn = xn_ref[...]
    pre_c = _dot(xn, wg_ref[...]) + brow_ref[...]
    g_c = GATE_CAP * jnp.tanh(pre_c / GATE_CAP)
    col = lax.broadcasted_iota(jnp.int32, g_c.shape, 1)
    gcol_ref[...] = jnp.where(col < M_HEADS, g_c, _log_sigmoid(g_c))
    pre_r = _dot_nt(wgt_ref[...], xn) + bcol_ref[...]
    g_r = GATE_CAP * jnp.tanh(pre_r / GATE_CAP)
    row = lax.broadcasted_iota(jnp.int32, g_r.shape, 0)
    grow_ref[...] = jnp.where(row < M_HEADS, g_r, _log_sigmoid(g_r))


def _mlstm_gates(xn, w_g, b_g, tm=512):
    T, D = xn.shape
    ng = 2 * M_HEADS
    tm = min(tm, T)
    wg = jnp.zeros((D, LANES), F32).at[:, :ng].set(w_g).astype(BF16)
    wgt = w_g.T.astype(BF16)
    brow = jnp.zeros((1, LANES), F32).at[0, :ng].set(b_g)
    bcol = b_g.reshape(ng, 1).astype(F32)
    return pl.pallas_call(
        _gates_kernel,
        grid=(T // tm,),
        in_specs=[pl.BlockSpec((tm, D), lambda i: (i, 0)),
                  pl.BlockSpec((D, LANES), lambda i: (0, 0)),
                  pl.BlockSpec((ng, D), lambda i: (0, 0)),
                  pl.BlockSpec((1, LANES), lambda i: (0, 0)),
                  pl.BlockSpec((ng, 1), lambda i: (0, 0))],
        out_specs=[pl.BlockSpec((tm, LANES), lambda i: (i, 0)),
                   pl.BlockSpec((ng, tm), lambda i: (0, i))],
        out_shape=[jax.ShapeDtypeStruct((T, LANES), F32),
                   jax.ShapeDtypeStruct((ng, T), F32)],
        compiler_params=_params("parallel"),
        name="mlstm_gates",
    )(xn, wg, wgt, brow, bcol)


def _mlstm_kernel(q_ref, k_ref, v_ref, o_ref, gc_ref, gr_ref, hg_ref, out_ref,
                  c_ref, n_ref, m_ref, *, dk, dv):
    L = q_ref.shape[0]

    @pl.when(pl.program_id(1) == 0)
    def _():
        c_ref[...] = jnp.zeros_like(c_ref)
        n_ref[...] = jnp.zeros_like(n_ref)
        m_ref[...] = jnp.zeros_like(m_ref)

    row = lax.broadcasted_iota(jnp.int32, (L, L), 0)
    col = lax.broadcasted_iota(jnp.int32, (L, L), 1)
    causal = col <= row
    tril = jnp.where(causal, 1.0, 0.0).astype(BF16)
    triu = jnp.where(row <= col, 1.0, 0.0).astype(BF16)

    gc = gc_ref[...]
    gr = gr_ref[...]
    c1, c2, c3 = _split3(gc)
    b_col_all = _dot(tril, c1) + _dot(tril, c2) + _dot(tril, c3)
    r1, r2, r3 = _split3(gr)
    b_row_all = _dot(r1, triu) + _dot(r2, triu) + _dot(r3, triu)

    for h in range(M_HEADS):
        li_row = gr[h:h + 1, :]
        li_col = gc[:, h:h + 1]
        b_row = b_row_all[M_HEADS + h:M_HEADS + h + 1, :]
        b_col = b_col_all[:, M_HEADS + h:M_HEADS + h + 1]
        m_prev = m_ref[h]

        log_d = jnp.where(causal, b_col - b_row + li_row, -jnp.inf)
        log_inter = b_col + m_prev
        m_t = jnp.maximum(log_inter, jnp.max(log_d, axis=1, keepdims=True))
        w_intra = jnp.exp(log_d - m_t)
        w_inter = jnp.exp(log_inter - m_t)

        qh = q_ref[:, h * dk:(h + 1) * dk]
        kh = k_ref[:, h * dk:(h + 1) * dk]
        vh = v_ref[:, h * dv:(h + 1) * dv]
        c_old = c_ref[h]
        n_old = n_ref[h]

        s = _dot_nt(qh, kh) * w_intra
        num = w_inter * _dot(qh, c_old.astype(BF16)) + _dot(s.astype(BF16), vh)
        qn = jnp.sum(qh.astype(F32) * n_old, axis=1, keepdims=True)
        den = w_inter * qn + jnp.sum(s, axis=1, keepdims=True)
        hval = num / jnp.maximum(jnp.abs(den), jnp.exp(-m_t))

        ms = jnp.mean(hval * hval, axis=1, keepdims=True)
        hn = hval * lax.rsqrt(ms + EPS) * hg_ref[:, h * dv:(h + 1) * dv]
        out_ref[:, h * dv:(h + 1) * dv] = (o_ref[:, h * dv:(h + 1) * dv].astype(F32) * hn).astype(out_ref.dtype)

        b_last = b_col[L - 1:L, :]
        log_w = b_last - b_col + li_col
        m_new = jnp.maximum(b_last + m_prev, jnp.max(log_w, axis=0, keepdims=True))
        wk = jnp.exp(log_w - m_new) * kh.astype(F32)
        decay = jnp.exp(b_last + m_prev - m_new)
        c_ref[h] = decay * c_old + _dot(wk.T.astype(BF16), vh)
        n_ref[h] = decay * n_old + jnp.sum(wk, axis=0, keepdims=True)
        m_ref[h] = m_new


def _mlstm(proj, gcol, grow, head_g, *, B, S, D):
    T = B * S
    dv = D // M_HEADS
    dk = dv // 2
    qk = M_HEADS * dk
    L = min(M_CHUNK, S)
    nc = S // L
    rb = lambda b, c: b * nc + c
    assert D % qk == 0
    return pl.pallas_call(
        functools.partial(_mlstm_kernel, dk=dk, dv=dv),
        grid=(B, nc),
        in_specs=[pl.BlockSpec((L, qk), lambda b, c: (rb(b, c), 0)),
                  pl.BlockSpec((L, qk), lambda b, c: (rb(b, c), 1)),
                  pl.BlockSpec((L, D), lambda b, c: (rb(b, c), (2 * qk) // D)),
                  pl.BlockSpec((L, D), lambda b, c: (rb(b, c), (2 * qk) // D + 1)),
                  pl.BlockSpec((L, LANES), lambda b, c: (rb(b, c), 0)),
                  pl.BlockSpec((2 * M_HEADS, L), lambda b, c: (0, rb(b, c))),
                  pl.BlockSpec((1, D), lambda b, c: (0, 0))],
        out_specs=pl.BlockSpec((L, D), lambda b, c: (rb(b, c), 0)),
        out_shape=jax.ShapeDtypeStruct((T, D), BF16),
        scratch_shapes=[pltpu.VMEM((M_HEADS, dk, dv), F32),
                        pltpu.VMEM((M_HEADS, 1, dk), F32),
                        pltpu.VMEM((M_HEADS, 1, 1), F32)],
        compiler_params=_params("parallel", "arbitrary"),
        name="mlstm",
    )(proj, proj, proj, proj, gcol, grow, head_g.reshape(1, D).astype(F32))


def _compress_kernel(a_ref, pos_ref, w1_ref, w2k_ref, w2vt_ref, gain_ref, ok_ref, ovt_ref):
    half = a_ref.shape[2]
    ncp = a_ref.shape[1]

    def hidden(j):
        a = a_ref[j].astype(F32)
        lo = (a + pos_ref[j, 0:1, :]).astype(BF16)
        hi = (a + pos_ref[j, 1:2, :]).astype(BF16)
        u = _dot(lo, w1_ref[j, 0:half, :])
        v = _dot(hi, w1_ref[j, half:2 * half, :])
        hid = u + pltpu.roll(v, shift=ncp - 1, axis=0)
        c0 = float(np.sqrt(2.0 / np.pi))
        cdf = 0.5 * (1.0 + jnp.tanh(c0 * (hid + 0.044715 * (hid * hid * hid))))
        return (hid * cdf).astype(BF16)

    yk = _dot(hidden(0), w2k_ref[...])
    rstd = lax.rsqrt(jnp.mean(yk * yk, axis=-1, keepdims=True) + EPS)
    ok_ref[...] = (yk * rstd * gain_ref[...]).astype(ok_ref.dtype)
    ovt_ref[...] = _dot_nt(w2vt_ref[...], hidden(1)).astype(ovt_ref.dtype)


def _compress(a2, pos2, w1, w2k, w2vt, gain):
    _, n, ncp, half = a2.shape
    hid = w1.shape[2]
    return pl.pallas_call(
        _compress_kernel,
        grid=(n,),
        in_specs=[pl.BlockSpec((2, None, ncp, half), lambda i: (0, i, 0, 0)),
                  pl.BlockSpec((2, 2, half), lambda i: (0, 0, 0)),
                  pl.BlockSpec((2, 2 * half, hid), lambda i: (0, 0, 0)),
                  pl.BlockSpec((hid, HEAD_DIM), lambda i: (0, 0)),
                  pl.BlockSpec((HEAD_DIM, hid), lambda i: (0, 0)),
                  pl.BlockSpec((1, HEAD_DIM), lambda i: (0, 0))],
        out_specs=[pl.BlockSpec((None, ncp, HEAD_DIM), lambda i: (i, 0, 0)),
                   pl.BlockSpec((None, HEAD_DIM, ncp), lambda i: (i, 0, 0))],
        out_shape=[jax.ShapeDtypeStruct((n, ncp, HEAD_DIM), BF16),
                   jax.ShapeDtypeStruct((n, HEAD_DIM, ncp), BF16)],
        compiler_params=_params("parallel"),
        name="nsa_compress",
    )(a2, pos2, w1, w2k, w2vt, gain)


def _nsa_kernel(q_ref, kc_ref, vct_ref, ks_ref, vst_ref, kw_ref, vwt_ref, gate_ref, mt_ref, o_ref,
                qa_ref, out_acc, acc_ref, m_ref, l_ref, s_ref, *, tq, hpg, rank_w):
    qi = pl.program_id(2)
    q0 = pl.multiple_of(qi * tq, tq)
    hd = HEAD_DIM
    t_lane = q0 + lax.broadcasted_iota(jnp.int32, (1, tq), 1)

    def gate(br, h):
        r = br * hpg + h
        return gate_ref[r:r + 1, :]

    def q_head(h):
        return q_ref[:, h * hd:(h + 1) * hd]

    ncp = kc_ref.shape[0]
    c_sub = lax.broadcasted_iota(jnp.int32, (ncp, 1), 0)
    cmp_valid = (c_sub * CMP_STRIDE + (CMP_BLOCK - 1)) <= t_lane
    kc = kc_ref[...]
    vct = vct_ref[...]
    psum = jnp.zeros((ncp, tq), F32)
    for h in range(hpg):
        s = jnp.where(cmp_valid, _dot_nt(kc, q_head(h)), NEG_BIG)
        e = jnp.exp2(s - jnp.max(s, axis=0, keepdims=True))
        p = jnp.where(cmp_valid, e * (1.0 / jnp.sum(e, axis=0, keepdims=True)), 0.0)
        psum = psum + p
        out_acc[h] = gate(0, h) * _dot(vct, p.astype(BF16))

    mt = mt_ref[...]
    n_sb = mt.shape[0]
    p1, p2, p3 = _split3(psum)
    p_blk = _dot(mt, p1) + _dot(mt, p2) + _dot(mt, p3)
    blk = lax.broadcasted_iota(jnp.int32, (n_sb, 1), 0)
    cur = t_lane // SEL_BLOCK
    forced = jnp.logical_or(blk == cur, blk == 0)
    val = jnp.where(forced, jnp.inf, jnp.where(blk <= cur, p_blk, -jnp.inf))

    sub = lax.broadcasted_iota(jnp.int32, (SUBLANES, 1), 0)
    n_rb = n_sb // SUBLANES
    top = float(min(N_SEL, n_sb))
    bias_cols = []
    for c in range(tq // rank_w):
        v = val[:, c * rank_w:(c + 1) * rank_w]
        rows = [v[rb * SUBLANES:(rb + 1) * SUBLANES, :] for rb in range(n_rb)]
        cnt = [jnp.zeros((SUBLANES, rank_w), F32) for _ in range(n_rb)]
        for i in range(n_sb):
            vi = v[i:i + 1, :]
            for rb in range(n_rb):
                lo = rb * SUBLANES
                ge = lambda: jnp.where(vi >= rows[rb], 1.0, 0.0)
                gt = lambda: jnp.where(vi > rows[rb], 1.0, 0.0)
                if lo > i:
                    beats = ge()
                elif lo + SUBLANES - 1 <= i:
                    beats = gt()
                else:
                    beats = jnp.where(sub > (i - lo), ge(), gt())
                cnt[rb] = cnt[rb] + beats
        cnt = jnp.concatenate(cnt, axis=0)
        cur_c = cur[:, c * rank_w:(c + 1) * rank_w]
        picked = jnp.where(cnt < top, jnp.where(blk <= cur_c, 0.0, MASK_VAL), MASK_VAL)
        bias_cols.append(picked)
    sel_bias = jnp.concatenate(bias_cols, axis=1)
    sel_bias = jnp.concatenate([sel_bias, jnp.zeros((LANES - n_sb, tq), F32)], axis=0)
    sb_t = sel_bias.T.astype(BF16)
    for h in range(hpg):
        qa_ref[h, :, 0:hd] = q_head(h)
        qa_ref[h, :, hd:2 * hd] = sb_t

    def flash_reset():
        m_ref[...] = jnp.full(m_ref.shape, NEG_BIG, F32)
        l_ref[...] = jnp.zeros_like(l_ref)
        acc_ref[...] = jnp.zeros_like(acc_ref)

    def flash_step(k_t, vt_t, qsel, mask):
        tile_max = []
        for h in range(hpg):
            s = _dot_nt(k_t, qsel(h))
            if mask is not None:
                s = jnp.where(mask, s, MASK_VAL)
            s_ref[h] = s
            tile_max.append(jnp.max(s, axis=0, keepdims=True))
        for h in range(hpg):
            m_old = m_ref[h]
            m_new = jnp.maximum(m_old, tile_max[h])
            alpha = jnp.exp2(m_old - m_new)
            p = jnp.exp2(s_ref[h] - m_new)
            l_ref[h] = alpha * l_ref[h] + jnp.sum(p, axis=0, keepdims=True)
            acc_ref[h] = alpha * acc_ref[h] + _dot(vt_t, p.astype(BF16))
            m_ref[h] = m_new

    def flash_out(h):
        return acc_ref[h] * (1.0 / l_ref[h])

    tk = tq
    k_sub = lax.broadcasted_iota(jnp.int32, (tk, 1), 0)
    q_lane = lax.broadcasted_iota(jnp.int32, (1, tq), 1)
    causal = k_sub <= q_lane
    lane_j = lax.broadcasted_iota(jnp.int32, (1, LANES), 1)
    key_blk = k_sub // SEL_BLOCK

    def ks_aug(k0):
        onehot = jnp.where(lane_j == key_blk + k0 // SEL_BLOCK, 1.0, 0.0).astype(BF16)
        return jnp.concatenate([ks_ref[pl.ds(k0, tk), :], onehot], axis=1)

    qa_head = lambda h: qa_ref[h]

    flash_reset()

    def sel_body(kt, carry):
        k0 = pl.multiple_of(kt * tk, tk)
        flash_step(ks_aug(k0), vst_ref[:, pl.ds(k0, tk)], qa_head, None)
        return carry

    lax.fori_loop(0, qi, sel_body, 0)
    flash_step(ks_aug(q0), vst_ref[:, pl.ds(q0, tk)], qa_head, causal)
    for h in range(hpg):
        out_acc[h] = out_acc[h] + gate(1, h) * flash_out(h)

    flash_reset()
    flash_step(kw_ref[pl.ds(q0, tk), :], vwt_ref[:, pl.ds(q0, tk)], q_head, causal)
    d = 1
    while d * tq - (tk - 1) < WINDOW:
        all_valid = d * tq + (tq - 1) < WINDOW
        mask = None if all_valid else (q_lane + d * tq - k_sub) < WINDOW

        @pl.when(qi >= d)
        def _(d=d, mask=mask):
            k0 = pl.multiple_of(q0 - d * tq, tk)
            flash_step(kw_ref[pl.ds(k0, tk), :], vwt_ref[:, pl.ds(k0, tk)], q_head, mask)
        d += 1

    for h in range(hpg):
        o = out_acc[h] + gate(2, h) * flash_out(h)
        o_ref[:, h * hd:(h + 1) * hd] = o.T.astype(o_ref.dtype)


def _cmp_to_sel_t(ncp, n_cmp, n_sb):
    c0 = np.arange(ncp)[None, :] * CMP_STRIDE
    s0 = np.arange(n_sb)[:, None] * SEL_BLOCK
    ov = np.minimum(c0 + CMP_BLOCK, s0 + SEL_BLOCK) - np.maximum(c0, s0)
    w = np.maximum(ov, 0) / CMP_BLOCK
    w = w * (np.arange(ncp)[None, :] < n_cmp)
    return jnp.asarray(w, dtype=BF16)


def _nsa_attention(qhat, k_cmp, v_cmp_t, k_tok, v_t, gates_t, *, B, S, D, tq=512):
    T = B * S
    hpg = N_HEADS // N_KV
    gw = hpg * HEAD_DIM
    tq = min(tq, S)
    nq = S // tq
    ncp = k_cmp.shape[1]
    n_cmp = (S - CMP_BLOCK) // CMP_STRIDE + 1
    n_sb = S // SEL_BLOCK
    assert n_sb % SUBLANES == 0 and n_sb <= LANES
    mt = _cmp_to_sel_t(ncp, n_cmp, n_sb)
    grow = gates_t.shape[0] // N_KV
    return pl.pallas_call(
        functools.partial(_nsa_kernel, tq=tq, hpg=hpg, rank_w=min(256, tq)),
        grid=(B, N_KV, nq),
        in_specs=[pl.BlockSpec((tq, gw), lambda b, g, i: (b * nq + i, g)),
                  pl.BlockSpec((None, ncp, HEAD_DIM), lambda b, g, i: (b * N_KV + g, 0, 0)),
                  pl.BlockSpec((None, HEAD_DIM, ncp), lambda b, g, i: (b * N_KV + g, 0, 0)),
                  pl.BlockSpec((S, HEAD_DIM), lambda b, g, i: (b, 2 * N_KV + g)),
                  pl.BlockSpec((HEAD_DIM, S), lambda b, g, i: (g, b)),
                  pl.BlockSpec((S, HEAD_DIM), lambda b, g, i: (b, 3 * N_KV + g)),
                  pl.BlockSpec((HEAD_DIM, S), lambda b, g, i: (N_KV + g, b)),
                  pl.BlockSpec((grow, tq), lambda b, g, i: (g, b * nq + i)),
                  pl.BlockSpec((n_sb, ncp), lambda b, g, i: (0, 0))],
        out_specs=pl.BlockSpec((tq, gw), lambda b, g, i: (b * nq + i, g)),
        out_shape=jax.ShapeDtypeStruct((T, D), BF16),
        scratch_shapes=[pltpu.VMEM((hpg, tq, 2 * HEAD_DIM), BF16),
                        pltpu.VMEM((hpg, HEAD_DIM, tq), F32),
                        pltpu.VMEM((hpg, HEAD_DIM, tq), F32),
                        pltpu.VMEM((hpg, 1, tq), F32),
                        pltpu.VMEM((hpg, 1, tq), F32),
                        pltpu.VMEM((hpg, tq, tq), F32)],
        compiler_params=_params("parallel", "parallel", "parallel"),
        name="nsa_attention",
    )(qhat, k_cmp, v_cmp_t, k_tok, v_t, k_tok, v_t, gates_t, mt)


def _mlp(x, g, w_up, w_down, layer):
    xn = _rmsnorm(x, g)
    h = _matmul(xn, w_up, layer=layer, epilogue="relu2", name="mlp_up")
    return _matmul(h, w_down, layer=layer, epilogue="resid", extra=x, out_dtype=F32, tm=2048, tn=1024, tk=1024,
                   name="mlp_down")


def kernel(x, attn_norm_g, mlp_norm_g, m_w_in, m_b_gate, m_head_g, m_w_out, kv_norm_g, w_kv, k_norm_g,
           cmp_pos, cmp_w1, cmp_w2, n_w_qg, q_norm_g, n_w_out, mlp_w_up, mlp_w_down):
    B, S, D = x.shape
    T = B * S
    xf = x.reshape(T, D)
    dv = D // M_HEADS
    dk = dv // 2
    qk = M_HEADS * dk
    nq = 2 * qk + 2 * D
    kv_dim = N_KV * HEAD_DIM
    hpg = N_HEADS // N_KV
    k_fold = (HEAD_DIM ** -0.5) * LOG2E

    xn = _rmsnorm(xf, attn_norm_g[0])
    proj = _matmul(xn, jnp.swapaxes(m_w_in, 1, 2), layer=0, nt=True, n_out=nq, epilogue="mlstm_in", tn=min(512, qk),
                   cfg=dict(m_qk=qk, d_model=D, k_scale=float(dk ** -0.5)), name="mlstm_in_proj")
    gcol, grow = _mlstm_gates(xn, m_w_in[0, :, nq:], m_b_gate[0])
    hg = _mlstm(proj, gcol, grow, m_head_g[0], B=B, S=S, D=D)
    xf = _matmul(hg, m_w_out, layer=0, epilogue="resid", extra=xf, out_dtype=F32, name="mlstm_out_proj")
    xf = _mlp(xf, mlp_norm_g[0], mlp_w_up, mlp_w_down, 0)

    qg = q_norm_g[0]
    ones = jnp.ones((kv_dim,), F32)
    kv_gain = jnp.concatenate([
        ones, ones,
        jnp.tile(k_norm_g[1] * qg[1] * k_fold, N_KV),
        jnp.tile(k_norm_g[2] * qg[2] * k_fold, N_KV)]).reshape(1, 4 * kv_dim)
    xh = _rmsnorm(xf, jnp.ones((D,), F32))
    k_tok, a2 = _matmul(xh, w_kv, n_out=4 * kv_dim, col_map=lambda j: j + j // 3, epilogue="kvnorm", extra=kv_gain,
                        gain=kv_norm_g, gain_on="w", tn=kv_dim, cfg=dict(kv_dim=kv_dim, norm_slots=(2, 3), seq=S),
                        name="nsa_kv_proj")
    w_vt = jnp.concatenate([w_kv[:, 3 * kv_dim:4 * kv_dim], w_kv[:, 5 * kv_dim:6 * kv_dim]], axis=1).T
    v_t = _matmul(w_vt, xh, nt=True, gain=kv_norm_g, gain_on="a", tm=kv_dim, tn=1024,
                  name="nsa_vt_proj")
    ncp = S // CMP_STRIDE
    a2 = a2.reshape(2, B * N_KV, ncp, CMP_STRIDE * HEAD_DIM)
    pos2 = cmp_pos.reshape(2, 2, CMP_STRIDE * HEAD_DIM).astype(F32)
    cmp_gain = (k_norm_g[0] * qg[0] * k_fold).reshape(1, HEAD_DIM)
    k_cmp, v_cmp_t = _compress(a2, pos2, cmp_w1.astype(BF16), cmp_w2[0].astype(BF16), cmp_w2[1].T.astype(BF16),
                               cmp_gain)

    w_qg = n_w_qg[0]
    qhat = _matmul(xh, jnp.swapaxes(n_w_qg, 1, 2), layer=0, nt=True, n_out=D, epilogue="headnorm",
                   gain=attn_norm_g[1], gain_on="w", name="nsa_q_proj")
    grow_n = -(-N_BRANCH * hpg // SUBLANES) * SUBLANES
    w_gate = w_qg[:, D:].reshape(D, N_BRANCH, N_KV, hpg).transpose(2, 1, 3, 0).reshape(N_KV, N_BRANCH * hpg, D)
    w_gate = jnp.zeros((N_KV, grow_n, D), F32).at[:, :N_BRANCH * hpg].set(w_gate).reshape(N_KV * grow_n, D)
    gates_t = _matmul(w_gate, xh, nt=True, epilogue="sigmoid", gain=attn_norm_g[1], gain_on="a", out_dtype=F32,
                      tn=1024, name="nsa_gate_proj")
    o = _nsa_attention(qhat, k_cmp, v_cmp_t, k_tok, v_t, gates_t, B=B, S=S, D=D)
    xf = _matmul(o, n_w_out, layer=0, epilogue="resid", extra=xf, out_dtype=F32, name="nsa_out_proj")
    xf = _mlp(xf, mlp_norm_g[1], mlp_w_up, mlp_w_down, 1)
    return xf.reshape(B, S, D)
```

```python
import functools

import numpy as np
import jax
import jax.numpy as jnp
from jax import lax
from jax.experimental import pallas as pl
from jax.experimental.pallas import tpu as pltpu

F32 = jnp.float32
BF16 = jnp.bfloat16

EPS = 1e-6
NEG_BIG = -1e30
MASK_VAL = -(2.0 ** 100)
LOG2E = 1.4426950408889634

M_HEADS = 8
GATE_CAP = 15.0
M_CHUNK = 256

N_HEADS = 32
N_KV = 4
HEAD_DIM = 128
N_BRANCH = 3
CMP_BLOCK = 32
CMP_STRIDE = 16
SEL_BLOCK = 64
N_SEL = 16
WINDOW = 512

LANES = 128
SUBLANES = 8
VMEM_LIMIT = 56 * 1024 * 1024


def _params(*sem):
    return pltpu.CompilerParams(dimension_semantics=sem, vmem_limit_bytes=VMEM_LIMIT)


def _dot(a, b):
    return jnp.dot(a, b, preferred_element_type=F32)


def _dot_nt(a, b):
    return lax.dot_general(a, b, (((1,), (1,)), ((), ())), preferred_element_type=F32)


def _split3(x):
    x1 = x.astype(BF16)
    r1 = x - x1.astype(F32)
    x2 = r1.astype(BF16)
    r2 = r1 - x2.astype(F32)
    return x1, x2, r2.astype(BF16)


def _rmsnorm_kernel(x_ref, g_ref, o_ref):
    x = x_ref[...]
    ms = jnp.mean(x * x, axis=-1, keepdims=True)
    o_ref[...] = (x * lax.rsqrt(ms + EPS) * g_ref[...]).astype(o_ref.dtype)


def _rmsnorm(x, g, tm=256):
    T, D = x.shape
    tm = min(tm, T)
    return pl.pallas_call(
        _rmsnorm_kernel,
        grid=(T // tm,),
        in_specs=[pl.BlockSpec((tm, D), lambda i: (i, 0)),
                  pl.BlockSpec((1, D), lambda i: (0, 0))],
        out_specs=pl.BlockSpec((tm, D), lambda i: (i, 0)),
        out_shape=jax.ShapeDtypeStruct((T, D), BF16),
        compiler_params=_params("parallel"),
        name="rmsnorm",
    )(x, g.reshape(1, D).astype(F32))


def _group_rstd(y):
    out = []
    for g in range(y.shape[1] // HEAD_DIM):
        yg = y[:, g * HEAD_DIM:(g + 1) * HEAD_DIM]
        out.append(lax.rsqrt(jnp.mean(yg * yg, axis=-1, keepdims=True) + EPS))
    return out


def _mm_kernel(*refs, epilogue, nk, tn, nt, gain_on, cfg):
    refs = list(refs)
    a_ref, w_ref = refs[:2]
    n_in = 2 + (gain_on is not None) + (epilogue in ("resid", "kvnorm"))
    g_ref = refs[2] if gain_on is not None else None
    e_ref = refs[n_in - 1] if epilogue in ("resid", "kvnorm") else None
    o_ref = refs[n_in]
    j = pl.program_id(1)
    k = pl.program_id(2)

    def product():
        a = a_ref[...]
        w = w_ref[...]
        if gain_on == "a":
            a = a * g_ref[...]
        elif gain_on == "w" and nt:
            w = w * g_ref[...]
        elif gain_on == "w":
            g = g_ref[...]
            w = jnp.concatenate([w[:, c * LANES:(c + 1) * LANES] * g for c in range(w.shape[1] // LANES)], axis=1)
        a = a.astype(BF16)
        w = w.astype(BF16)
        return _dot_nt(a, w) if nt else _dot(a, w)

    if nk > 1:
        assert epilogue == "resid" and o_ref.dtype == F32

        @pl.when(k == 0)
        def _():
            o_ref[...] = e_ref[...]

        o_ref[...] += product()
        return

    part = product()

    def finish(y):
        if epilogue == "none":
            o_ref[...] = y.astype(o_ref.dtype)
        elif epilogue == "relu2":
            r = jnp.maximum(y, 0.0)
            o_ref[...] = (r * r).astype(o_ref.dtype)
        elif epilogue == "sigmoid":
            o_ref[...] = jax.nn.sigmoid(y).astype(o_ref.dtype)
        elif epilogue == "resid":
            o_ref[...] = (e_ref[...] + y).astype(o_ref.dtype)
        elif epilogue == "headnorm":
            rstd = _group_rstd(y)
            for g, r in enumerate(rstd):
                sl = slice(g * HEAD_DIM, (g + 1) * HEAD_DIM)
                o_ref[:, sl] = (y[:, sl] * r).astype(o_ref.dtype)
        elif epilogue == "kvnorm":
            slot = (j * tn) // cfg["kv_dim"]
            is_norm = functools.reduce(jnp.logical_or, [slot == s for s in cfg["norm_slots"]])
            rstd = _group_rstd(y)
            gain = e_ref[...]
            for g, r in enumerate(rstd):
                sl = slice(g * HEAD_DIM, (g + 1) * HEAD_DIM)
                mult = jnp.where(is_norm, r, 1.0)
                o_ref[:, sl] = (y[:, sl] * mult * gain[:, sl]).astype(o_ref.dtype)

            blk_ref, y_scr = refs[n_in + 1], refs[n_in + 2]
            rows = y.shape[0] // CMP_STRIDE

            @pl.when(slot < 2)
            def _():
                for g in range(y.shape[1] // HEAD_DIM):
                    y_scr[g] = y[:, g * HEAD_DIM:(g + 1) * HEAD_DIM]
                    for r in range(CMP_STRIDE):
                        piece = y_scr[g, pl.ds(r, rows, stride=CMP_STRIDE), :]
                        blk_ref[g, :, r * HEAD_DIM:(r + 1) * HEAD_DIM] = piece.astype(blk_ref.dtype)
        elif epilogue == "mlstm_in":
            col0 = j * tn
            qk, d = cfg["m_qk"], cfg["d_model"]
            is_k = jnp.logical_and(col0 >= qk, col0 < 2 * qk)
            is_o = col0 >= 2 * qk + d
            lin = y * jnp.where(is_k, cfg["k_scale"], 1.0)
            o_ref[...] = jnp.where(is_o, jax.nn.sigmoid(y), lin).astype(o_ref.dtype)
        else:
            raise ValueError(epilogue)

    finish(part)


def _matmul(a, w, *, layer=None, nt=False, n_out=None, col_map=None, epilogue="none", extra=None, gain=None,
            gain_on=None, out_dtype=BF16, tm=2048, tn=512, tk=4096, cfg=None, name="matmul"):
    M, K = a.shape
    wshape = w.shape[1:] if layer is not None else w.shape
    N = n_out if n_out is not None else (wshape[0] if nt else wshape[1])
    tm, tn, tk = min(tm, M), min(tn, N), min(tk, K)
    assert M % tm == 0 and N % tn == 0 and K % tk == 0
    nk = K // tk
    cm = col_map if col_map is not None else (lambda j: j)
    if layer is not None and nt:
        w_spec = pl.BlockSpec((None, tn, tk), lambda i, j, k: (layer, cm(j), k))
    elif layer is not None:
        w_spec = pl.BlockSpec((None, tk, tn), lambda i, j, k: (layer, k, cm(j)))
    elif nt:
        w_spec = pl.BlockSpec((tn, tk), lambda i, j, k: (cm(j), k))
    else:
        w_spec = pl.BlockSpec((tk, tn), lambda i, j, k: (k, cm(j)))
    a_mode = dict(pipeline_mode=pl.Buffered(1)) if (nk == 1 and N // tn > 1) else {}
    in_specs = [pl.BlockSpec((tm, tk), lambda i, j, k: (i, k), **a_mode), w_spec]
    args = [a, w]
    if gain_on is not None:
        if gain_on == "w" and not nt:
            in_specs.append(pl.BlockSpec((tk, LANES), lambda i, j, k: (k, 0)))
            args.append(jnp.broadcast_to(gain.astype(F32).reshape(K, 1), (K, LANES)))
        else:
            in_specs.append(pl.BlockSpec((1, tk), lambda i, j, k: (0, k)))
            args.append(gain.astype(F32).reshape(1, K))
    if epilogue == "resid":
        e_mode = dict(pipeline_mode=pl.Buffered(1)) if nk > 1 else {}
        in_specs.append(pl.BlockSpec((tm, tn), lambda i, j, k: (i, j), **e_mode))
        args.append(extra)
    elif epilogue == "kvnorm":
        in_specs.append(pl.BlockSpec((1, tn), lambda i, j, k: (0, j)))
        args.append(extra)
    out_specs = pl.BlockSpec((tm, tn), lambda i, j, k: (i, j))
    out_shape = jax.ShapeDtypeStruct((M, N), out_dtype)
    scratch = []
    semantics = ("parallel", "parallel", "arbitrary")
    if epilogue == "kvnorm":
        seq, groups = cfg["seq"], tn // HEAD_DIM
        assert seq % tm == 0 and tm % CMP_STRIDE == 0 and tn == cfg["kv_dim"]
        per_seq = seq // tm
        out_specs = [out_specs, pl.BlockSpec((None, None, groups, tm // CMP_STRIDE, CMP_STRIDE * HEAD_DIM),
                                             lambda i, j, k: (jnp.minimum(j, 1), i // per_seq, 0, i % per_seq, 0))]
        out_shape = [out_shape, jax.ShapeDtypeStruct((2, M // seq, groups, seq // CMP_STRIDE, CMP_STRIDE * HEAD_DIM),
                                                     out_dtype)]
        scratch = [pltpu.VMEM((groups, tm, HEAD_DIM), F32)]
        semantics = ("parallel", "arbitrary", "arbitrary")
    return pl.pallas_call(
        functools.partial(_mm_kernel, epilogue=epilogue, nk=nk, tn=tn, nt=nt, gain_on=gain_on, cfg=cfg),
        grid=(M // tm, N // tn, nk),
        in_specs=in_specs,
        out_specs=out_specs,
        out_shape=out_shape,
        scratch_shapes=scratch,
        compiler_params=_params(*semantics),
        name=name,
    )(*args)


def _log_sigmoid(x):
    return jnp.minimum(x, 0.0) - jnp.log(1.0 + jnp.exp(-jnp.abs(x)))


def _gates_kernel(xn_ref, wg_ref, wgt_ref, brow_ref, bcol_ref, gcol_ref, grow_ref):
    xn = xn_ref[...]
    pre_c = _dot(xn, wg_ref[...]) + brow_ref[...]
    g_c = GATE_CAP * jnp.tanh(pre_c / GATE_CAP)
    col = lax.broadcasted_iota(jnp.int32, g_c.shape, 1)
    gcol_ref[...] = jnp.where(col < M_HEADS, g_c, _log_sigmoid(g_c))
    pre_r = _dot_nt(wgt_ref[...], xn) + bcol_ref[...]
    g_r = GATE_CAP * jnp.tanh(pre_r / GATE_CAP)
    row = lax.broadcasted_iota(jnp.int32, g_r.shape, 0)
    grow_ref[...] = jnp.where(row < M_HEADS, g_r, _log_sigmoid(g_r))


def _mlstm_gates(xn, w_g, b_g, tm=512):
    T, D = xn.shape
    ng = 2 * M_HEADS
    tm = min(tm, T)
    wg = jnp.zeros((D, LANES), F32).at[:, :ng].set(w_g).astype(BF16)
    wgt = w_g.T.astype(BF16)
    brow = jnp.zeros((1, LANES), F32).at[0, :ng].set(b_g)
    bcol = b_g.reshape(ng, 1).astype(F32)
    return pl.pallas_call(
        _gates_kernel,
        grid=(T // tm,),
        in_specs=[pl.BlockSpec((tm, D), lambda i: (i, 0)),
                  pl.BlockSpec((D, LANES), lambda i: (0, 0)),
                  pl.BlockSpec((ng, D), lambda i: (0, 0)),
                  pl.BlockSpec((1, LANES), lambda i: (0, 0)),
                  pl.BlockSpec((ng, 1), lambda i: (0, 0))],
        out_specs=[pl.BlockSpec((tm, LANES), lambda i: (i, 0)),
                   pl.BlockSpec((ng, tm), lambda i: (0, i))],
        out_shape=[jax.ShapeDtypeStruct((T, LANES), F32),
                   jax.ShapeDtypeStruct((ng, T), F32)],
        compiler_params=_params("parallel"),
        name="mlstm_gates",
    )(xn, wg, wgt, brow, bcol)


def _mlstm_kernel(q_ref, k_ref, v_ref, o_ref, gc_ref, gr_ref, out_ref, c_ref, m_ref, *, dk, dv):
    L = q_ref.shape[0]

    @pl.when(pl.program_id(1) == 0)
    def _():
        c_ref[...] = jnp.zeros_like(c_ref)
        m_ref[...] = jnp.zeros_like(m_ref)

    row = lax.broadcasted_iota(jnp.int32, (L, L), 0)
    col = lax.broadcasted_iota(jnp.int32, (L, L), 1)
    causal = col <= row
    tril = jnp.where(causal, 1.0, 0.0).astype(BF16)
    triu = jnp.where(row <= col, 1.0, 0.0).astype(BF16)

    gc = gc_ref[...]
    gr = gr_ref[...]
    c1, c2, c3 = _split3(gc)
    b_col_all = _dot(tril, c1) + _dot(tril, c2) + _dot(tril, c3)
    r1, r2, r3 = _split3(gr)
    b_row_all = _dot(r1, triu) + _dot(r2, triu) + _dot(r3, triu)

    ones_blk = jnp.where(lax.broadcasted_iota(jnp.int32, (L, LANES), 1) == 0, 1.0, 0.0).astype(BF16)

    for h in range(M_HEADS):
        li_row = gr[h:h + 1, :]
        li_col = gc[:, h:h + 1]
        b_row = b_row_all[M_HEADS + h:M_HEADS + h + 1, :]
        b_col = b_col_all[:, M_HEADS + h:M_HEADS + h + 1]
        m_prev = m_ref[h]

        log_d = jnp.where(causal, b_col + (li_row - b_row), -jnp.inf)
        log_inter = b_col + m_prev
        m_t = jnp.maximum(log_inter, jnp.max(log_d, axis=1, keepdims=True))
        w_intra = jnp.exp(log_d - m_t)
        w_inter = jnp.exp(log_inter - m_t)

        qh = q_ref[:, h * dk:(h + 1) * dk]
        kh = k_ref[:, h * dk:(h + 1) * dk]
        v_aug = jnp.concatenate([v_ref[:, h * dv:(h + 1) * dv], ones_blk], axis=1)
        c_old = c_ref[h]

        s = (_dot_nt(qh, kh) * w_intra).astype(BF16)
        qs = (qh.astype(F32) * w_inter).astype(BF16)
        numden = _dot(qs, c_old.astype(BF16)) + _dot(s, v_aug)
        num = numden[:, 0:dv]
        den = numden[:, dv:dv + 1]
        inv = 1.0 / jnp.maximum(jnp.abs(den), jnp.exp(-m_t))
        ssq = jnp.sum(num * num, axis=1, keepdims=True)
        scale = inv * lax.rsqrt(inv * inv * ssq * (1.0 / dv) + EPS)
        gate_o = o_ref[:, h * dv:(h + 1) * dv].astype(F32)
        out_ref[:, h * dv:(h + 1) * dv] = (gate_o * (num * scale)).astype(out_ref.dtype)

        b_last = b_col[L - 1:L, :]
        log_w = b_last - b_col + li_col
        m_new = jnp.maximum(b_last + m_prev, jnp.max(log_w, axis=0, keepdims=True))
        wk = jnp.exp(log_w - m_new) * kh.astype(F32)
        decay = jnp.exp(b_last + m_prev - m_new)
        c_ref[h] = decay * c_old + _dot(wk.T.astype(BF16), v_aug)
        m_ref[h] = m_new


def _mlstm(proj, gcol, grow, *, B, S, D):
    T = B * S
    dv = D // M_HEADS
    dk = dv // 2
    qk = M_HEADS * dk
    L = min(M_CHUNK, S)
    nc = S // L
    rb = lambda b, c: b * nc + c
    assert D % qk == 0
    return pl.pallas_call(
        functools.partial(_mlstm_kernel, dk=dk, dv=dv),
        grid=(B, nc),
        in_specs=[pl.BlockSpec((L, qk), lambda b, c: (rb(b, c), 0)),
                  pl.BlockSpec((L, qk), lambda b, c: (rb(b, c), 1)),
                  pl.BlockSpec((L, D), lambda b, c: (rb(b, c), (2 * qk) // D)),
                  pl.BlockSpec((L, D), lambda b, c: (rb(b, c), (2 * qk) // D + 1)),
                  pl.BlockSpec((L, LANES), lambda b, c: (rb(b, c), 0)),
                  pl.BlockSpec((2 * M_HEADS, L), lambda b, c: (0, rb(b, c)))],
        out_specs=pl.BlockSpec((L, D), lambda b, c: (rb(b, c), 0)),
        out_shape=jax.ShapeDtypeStruct((T, D), BF16),
        scratch_shapes=[pltpu.VMEM((M_HEADS, dk, dv + LANES), F32),
                        pltpu.VMEM((M_HEADS, 1, 1), F32)],
        compiler_params=_params("parallel", "arbitrary"),
        name="mlstm",
    )(proj, proj, proj, proj, gcol, grow)


def _compress_kernel(a_ref, pos_ref, w1_ref, w2k_ref, w2vt_ref, gain_ref, ok_ref, ovt_ref):
    half = a_ref.shape[2]
    ncp = a_ref.shape[1]

    def hidden(j):
        a = a_ref[j].astype(F32)
        lo = (a + pos_ref[j, 0:1, :]).astype(BF16)
        hi = (a + pos_ref[j, 1:2, :]).astype(BF16)
        u = _dot(lo, w1_ref[j, 0:half, :])
        v = _dot(hi, w1_ref[j, half:2 * half, :])
        hid = u + pltpu.roll(v, shift=ncp - 1, axis=0)
        c0 = float(np.sqrt(2.0 / np.pi))
        cdf = 0.5 * (1.0 + jnp.tanh(c0 * (hid + 0.044715 * (hid * hid * hid))))
        return (hid * cdf).astype(BF16)

    yk = _dot(hidden(0), w2k_ref[...])
    rstd = lax.rsqrt(jnp.mean(yk * yk, axis=-1, keepdims=True) + EPS)
    ok_ref[...] = (yk * rstd * gain_ref[...]).astype(ok_ref.dtype)
    ovt_ref[...] = _dot_nt(w2vt_ref[...], hidden(1)).astype(ovt_ref.dtype)


def _compress(a2, pos2, w1, w2k, w2vt, gain):
    _, n, ncp, half = a2.shape
    hid = w1.shape[2]
    return pl.pallas_call(
        _compress_kernel,
        grid=(n,),
        in_specs=[pl.BlockSpec((2, None, ncp, half), lambda i: (0, i, 0, 0)),
                  pl.BlockSpec((2, 2, half), lambda i: (0, 0, 0)),
                  pl.BlockSpec((2, 2 * half, hid), lambda i: (0, 0, 0)),
                  pl.BlockSpec((hid, HEAD_DIM), lambda i: (0, 0)),
                  pl.BlockSpec((HEAD_DIM, hid), lambda i: (0, 0)),
                  pl.BlockSpec((1, HEAD_DIM), lambda i: (0, 0))],
        out_specs=[pl.BlockSpec((None, ncp, HEAD_DIM), lambda i: (i, 0, 0)),
                   pl.BlockSpec((None, HEAD_DIM, ncp), lambda i: (i, 0, 0))],
        out_shape=[jax.ShapeDtypeStruct((n, ncp, HEAD_DIM), BF16),
                   jax.ShapeDtypeStruct((n, HEAD_DIM, ncp), BF16)],
        compiler_params=_params("parallel"),
        name="nsa_compress",
    )(a2, pos2, w1, w2k, w2vt, gain)


def _nsa_kernel(q_ref, kc_ref, vct_ref, ks_ref, vst_ref, kw_ref, vwt_ref, gate_ref, mt_ref, o_ref,
                qa_ref, out_acc, acc_ref, m_ref, l_ref, s_ref, *, tq, hpg, rank_w):
    qi = pl.program_id(2)
    q0 = pl.multiple_of(qi * tq, tq)
    hd = HEAD_DIM
    t_lane = q0 + lax.broadcasted_iota(jnp.int32, (1, tq), 1)

    def gate(br, h):
        r = br * hpg + h
        return gate_ref[r:r + 1, :]

    def q_head(h):
        return q_ref[:, h * hd:(h + 1) * hd]

    ncp = kc_ref.shape[0]
    c_sub = lax.broadcasted_iota(jnp.int32, (ncp, 1), 0)
    cmp_valid = (c_sub * CMP_STRIDE + (CMP_BLOCK - 1)) <= t_lane
    kc = kc_ref[...]
    vct = vct_ref[...]
    psum = jnp.zeros((ncp, tq), F32)
    for h in range(hpg):
        s = jnp.where(cmp_valid, _dot_nt(kc, q_head(h)), NEG_BIG)
        e = jnp.exp2(s - jnp.max(s, axis=0, keepdims=True))
        p = jnp.where(cmp_valid, e * (1.0 / jnp.sum(e, axis=0, keepdims=True)), 0.0)
        psum = psum + p
        out_acc[h] = gate(0, h) * _dot(vct, p.astype(BF16))

    mt = mt_ref[...]
    n_sb = mt.shape[0]
    p1, p2, p3 = _split3(psum)
    p_blk = _dot(mt, p1) + _dot(mt, p2) + _dot(mt, p3)
    blk = lax.broadcasted_iota(jnp.int32, (n_sb, 1), 0)
    cur = t_lane // SEL_BLOCK
    forced = jnp.logical_or(blk == cur, blk == 0)
    val = jnp.where(forced, jnp.inf, jnp.where(blk <= cur, p_blk, -jnp.inf))

    sub = lax.broadcasted_iota(jnp.int32, (SUBLANES, 1), 0)
    n_rb = n_sb // SUBLANES
    top = float(min(N_SEL, n_sb))
    bias_cols = []
    for c in range(tq // rank_w):
        v = val[:, c * rank_w:(c + 1) * rank_w]
        rows = [v[rb * SUBLANES:(rb + 1) * SUBLANES, :] for rb in range(n_rb)]
        cnt = [jnp.zeros((SUBLANES, rank_w), F32) for _ in range(n_rb)]
        for i in range(n_sb):
            vi = v[i:i + 1, :]
            for rb in range(n_rb):
                lo = rb * SUBLANES
                ge = lambda: jnp.where(vi >= rows[rb], 1.0, 0.0)
                gt = lambda: jnp.where(vi > rows[rb], 1.0, 0.0)
                if lo > i:
                    beats = ge()
                elif lo + SUBLANES - 1 <= i:
                    beats = gt()
                else:
                    beats = jnp.where(sub > (i - lo), ge(), gt())
                cnt[rb] = cnt[rb] + beats
        cnt = jnp.concatenate(cnt, axis=0)
        cur_c = cur[:, c * rank_w:(c + 1) * rank_w]
        picked = jnp.where(cnt < top, jnp.where(blk <= cur_c, 0.0, MASK_VAL), MASK_VAL)
        bias_cols.append(picked)
    sel_bias = jnp.concatenate(bias_cols, axis=1)
    sel_bias = jnp.concatenate([sel_bias, jnp.zeros((LANES - n_sb, tq), F32)], axis=0)
    sb_t = sel_bias.T.astype(BF16)
    for h in range(hpg):
        qa_ref[h, :, 0:hd] = q_head(h)
        qa_ref[h, :, hd:2 * hd] = sb_t

    def flash_reset():
        m_ref[...] = jnp.full(m_ref.shape, NEG_BIG, F32)
        l_ref[...] = jnp.zeros_like(l_ref)
        acc_ref[...] = jnp.zeros_like(acc_ref)

    def flash_step(k_t, vt_t, qsel, mask):
        tile_max = []
        for h in range(hpg):
            s = _dot_nt(k_t, qsel(h))
            if mask is not None:
                s = jnp.where(mask, s, MASK_VAL)
            s_ref[h] = s
            tile_max.append(jnp.max(s, axis=0, keepdims=True))
        for h in range(hpg):
            m_old = m_ref[h]
            m_new = jnp.maximum(m_old, tile_max[h])
            alpha = jnp.exp2(m_old - m_new)
            p = jnp.exp2(s_ref[h] - m_new)
            l_ref[h] = alpha * l_ref[h] + jnp.sum(p, axis=0, keepdims=True)
            acc_ref[h] = alpha * acc_ref[h] + _dot(vt_t, p.astype(BF16))
            m_ref[h] = m_new

    def flash_out(h):
        return acc_ref[h] * (1.0 / l_ref[h])

    tk = tq
    k_sub = lax.broadcasted_iota(jnp.int32, (tk, 1), 0)
    q_lane = lax.broadcasted_iota(jnp.int32, (1, tq), 1)
    causal = k_sub <= q_lane
    lane_j = lax.broadcasted_iota(jnp.int32, (1, LANES), 1)
    key_blk = k_sub // SEL_BLOCK

    def ks_aug(k0):
        onehot = jnp.where(lane_j == key_blk + k0 // SEL_BLOCK, 1.0, 0.0).astype(BF16)
        return jnp.concatenate([ks_ref[pl.ds(k0, tk), :], onehot], axis=1)

    qa_head = lambda h: qa_ref[h]

    flash_reset()

    def sel_body(kt, carry):
        k0 = pl.multiple_of(kt * tk, tk)
        flash_step(ks_aug(k0), vst_ref[:, pl.ds(k0, tk)], qa_head, None)
        return carry

    lax.fori_loop(0, qi, sel_body, 0)
    flash_step(ks_aug(q0), vst_ref[:, pl.ds(q0, tk)], qa_head, causal)
    for h in range(hpg):
        out_acc[h] = out_acc[h] + gate(1, h) * flash_out(h)

    flash_reset()
    flash_step(kw_ref[pl.ds(q0, tk), :], vwt_ref[:, pl.ds(q0, tk)], q_head, causal)
    d = 1
    while d * tq - (tk - 1) < WINDOW:
        all_valid = d * tq + (tq - 1) < WINDOW
        mask = None if all_valid else (q_lane + d * tq - k_sub) < WINDOW

        @pl.when(qi >= d)
        def _(d=d, mask=mask):
            k0 = pl.multiple_of(q0 - d * tq, tk)
            flash_step(kw_ref[pl.ds(k0, tk), :], vwt_ref[:, pl.ds(k0, tk)], q_head, mask)
        d += 1

    for h in range(hpg):
        o = out_acc[h] + gate(2, h) * flash_out(h)
        o_ref[:, h * hd:(h + 1) * hd] = o.T.astype(o_ref.dtype)


def _cmp_to_sel_t(ncp, n_cmp, n_sb):
    c0 = np.arange(ncp)[None, :] * CMP_STRIDE
    s0 = np.arange(n_sb)[:, None] * SEL_BLOCK
    ov = np.minimum(c0 + CMP_BLOCK, s0 + SEL_BLOCK) - np.maximum(c0, s0)
    w = np.maximum(ov, 0) / CMP_BLOCK
    w = w * (np.arange(ncp)[None, :] < n_cmp)
    return jnp.asarray(w, dtype=BF16)


def _nsa_attention(qhat, k_cmp, v_cmp_t, k_tok, v_t, gates_t, *, B, S, D, tq=512):
    T = B * S
    hpg = N_HEADS // N_KV
    gw = hpg * HEAD_DIM
    tq = min(tq, S)
    nq = S // tq
    ncp = k_cmp.shape[1]
    n_cmp = (S - CMP_BLOCK) // CMP_STRIDE + 1
    n_sb = S // SEL_BLOCK
    assert n_sb % SUBLANES == 0 and n_sb <= LANES
    mt = _cmp_to_sel_t(ncp, n_cmp, n_sb)
    grow = gates_t.shape[0] // N_KV
    return pl.pallas_call(
        functools.partial(_nsa_kernel, tq=tq, hpg=hpg, rank_w=min(256, tq)),
        grid=(B, N_KV, nq),
        in_specs=[pl.BlockSpec((tq, gw), lambda b, g, i: (b * nq + i, g)),
                  pl.BlockSpec((None, ncp, HEAD_DIM), lambda b, g, i: (b * N_KV + g, 0, 0)),
                  pl.BlockSpec((None, HEAD_DIM, ncp), lambda b, g, i: (b * N_KV + g, 0, 0)),
                  pl.BlockSpec((S, HEAD_DIM), lambda b, g, i: (b, 2 * N_KV + g)),
                  pl.BlockSpec((HEAD_DIM, S), lambda b, g, i: (g, b)),
                  pl.BlockSpec((S, HEAD_DIM), lambda b, g, i: (b, 3 * N_KV + g)),
                  pl.BlockSpec((HEAD_DIM, S), lambda b, g, i: (N_KV + g, b)),
                  pl.BlockSpec((grow, tq), lambda b, g, i: (g, b * nq + i)),
                  pl.BlockSpec((n_sb, ncp), lambda b, g, i: (0, 0))],
        out_specs=pl.BlockSpec((tq, gw), lambda b, g, i: (b * nq + i, g)),
        out_shape=jax.ShapeDtypeStruct((T, D), BF16),
        scratch_shapes=[pltpu.VMEM((hpg, tq, 2 * HEAD_DIM), BF16),
                        pltpu.VMEM((hpg, HEAD_DIM, tq), F32),
                        pltpu.VMEM((hpg, HEAD_DIM, tq), F32),
                        pltpu.VMEM((hpg, 1, tq), F32),
                        pltpu.VMEM((hpg, 1, tq), F32),
                        pltpu.VMEM((hpg, tq, tq), F32)],
        compiler_params=_params("parallel", "parallel", "parallel"),
        name="nsa_attention",
    )(qhat, k_cmp, v_cmp_t, k_tok, v_t, k_tok, v_t, gates_t, mt)


def _mlp(x, g, w_up, w_down, layer):
    xn = _rmsnorm(x, g)
    h = _matmul(xn, w_up, layer=layer, epilogue="relu2", name="mlp_up")
    return _matmul(h, w_down, layer=layer, epilogue="resid", extra=x, out_dtype=F32, tm=2048, tn=1024, tk=1024,
                   name="mlp_down")


def kernel(x, attn_norm_g, mlp_norm_g, m_w_in, m_b_gate, m_head_g, m_w_out, kv_norm_g, w_kv, k_norm_g,
           cmp_pos, cmp_w1, cmp_w2, n_w_qg, q_norm_g, n_w_out, mlp_w_up, mlp_w_down):
    B, S, D = x.shape
    T = B * S
    xf = x.reshape(T, D)
    dv = D // M_HEADS
    dk = dv // 2
    qk = M_HEADS * dk
    nq = 2 * qk + 2 * D
    kv_dim = N_KV * HEAD_DIM
    hpg = N_HEADS // N_KV
    k_fold = (HEAD_DIM ** -0.5) * LOG2E

    xn = _rmsnorm(xf, attn_norm_g[0])
    proj = _matmul(xn, jnp.swapaxes(m_w_in, 1, 2), layer=0, nt=True, n_out=nq, epilogue="mlstm_in", tn=min(512, qk),
                   cfg=dict(m_qk=qk, d_model=D, k_scale=float(dk ** -0.5)), name="mlstm_in_proj")
    gcol, grow = _mlstm_gates(xn, m_w_in[0, :, nq:], m_b_gate[0])
    hg = _mlstm(proj, gcol, grow, B=B, S=S, D=D)
    xf = _matmul(hg, m_w_out, layer=0, epilogue="resid", extra=xf, gain=m_head_g[0], gain_on="w", out_dtype=F32,
                 name="mlstm_out_proj")
    xf = _mlp(xf, mlp_norm_g[0], mlp_w_up, mlp_w_down, 0)

    qg = q_norm_g[0]
    ones = jnp.ones((kv_dim,), F32)
    kv_gain = jnp.concatenate([
        ones, ones,
        jnp.tile(k_norm_g[1] * qg[1] * k_fold, N_KV),
        jnp.tile(k_norm_g[2] * qg[2] * k_fold, N_KV)]).reshape(1, 4 * kv_dim)
    xh = _rmsnorm(xf, jnp.ones((D,), F32))
    k_tok, a2 = _matmul(xh, w_kv, n_out=4 * kv_dim, col_map=lambda j: j + j // 3, epilogue="kvnorm", extra=kv_gain,
                        gain=kv_norm_g, gain_on="w", tn=kv_dim, cfg=dict(kv_dim=kv_dim, norm_slots=(2, 3), seq=S),
                        name="nsa_kv_proj")
    w_vt = jnp.concatenate([w_kv[:, 3 * kv_dim:4 * kv_dim], w_kv[:, 5 * kv_dim:6 * kv_dim]], axis=1).T
    v_t = _matmul(w_vt, xh, nt=True, gain=kv_norm_g, gain_on="a", tm=kv_dim, tn=1024,
                  name="nsa_vt_proj")
    ncp = S // CMP_STRIDE
    a2 = a2.reshape(2, B * N_KV, ncp, CMP_STRIDE * HEAD_DIM)
    pos2 = cmp_pos.reshape(2, 2, CMP_STRIDE * HEAD_DIM).astype(F32)
    cmp_gain = (k_norm_g[0] * qg[0] * k_fold).reshape(1, HEAD_DIM)
    k_cmp, v_cmp_t = _compress(a2, pos2, cmp_w1.astype(BF16), cmp_w2[0].astype(BF16), cmp_w2[1].T.astype(BF16),
                               cmp_gain)

    w_qg = n_w_qg[0]
    qhat = _matmul(xh, jnp.swapaxes(n_w_qg, 1, 2), layer=0, nt=True, n_out=D, epilogue="headnorm",
                   gain=attn_norm_g[1], gain_on="w", name="nsa_q_proj")
    grow_n = -(-N_BRANCH * hpg // SUBLANES) * SUBLANES
    w_gate = w_qg[:, D:].reshape(D, N_BRANCH, N_KV, hpg).transpose(2, 1, 3, 0).reshape(N_KV, N_BRANCH * hpg, D)
    w_gate = jnp.zeros((N_KV, grow_n, D), F32).at[:, :N_BRANCH * hpg].set(w_gate).reshape(N_KV * grow_n, D)
    gates_t = _matmul(w_gate, xh, nt=True, epilogue="sigmoid", gain=attn_norm_g[1], gain_on="a", out_dtype=F32,
                      tn=1024, name="nsa_gate_proj")
    o = _nsa_attention(qhat, k_cmp, v_cmp_t, k_tok, v_t, gates_t, B=B, S=S, D=D)
    xf = _matmul(o, n_w_out, layer=0, epilogue="resid", extra=xf, out_dtype=F32, name="nsa_out_proj")
    xf = _mlp(xf, mlp_norm_g[1], mlp_w_up, mlp_w_down, 1)
    return xf.reshape(B, S, D)
```

```python
import functools

import numpy as np
import jax
import jax.numpy as jnp
from jax import lax
from jax.experimental import pallas as pl
from jax.experimental.pallas import tpu as pltpu

F32 = jnp.float32
BF16 = jnp.bfloat16

EPS = 1e-6
NEG_BIG = -1e30
MASK_VAL = -(2.0 ** 100)
LOG2E = 1.4426950408889634

M_HEADS = 8
GATE_CAP = 15.0
M_CHUNK = 256

N_HEADS = 32
N_KV = 4
HEAD_DIM = 128
N_BRANCH = 3
CMP_BLOCK = 32
CMP_STRIDE = 16
SEL_BLOCK = 64
N_SEL = 16
WINDOW = 512

LANES = 128
SUBLANES = 8
VMEM_LIMIT = 56 * 1024 * 1024


def _params(*sem):
    return pltpu.CompilerParams(dimension_semantics=sem, vmem_limit_bytes=VMEM_LIMIT)


def _dot(a, b):
    return jnp.dot(a, b, preferred_element_type=F32)


def _dot_nt(a, b):
    return lax.dot_general(a, b, (((1,), (1,)), ((), ())), preferred_element_type=F32)


def _split3(x):
    x1 = x.astype(BF16)
    r1 = x - x1.astype(F32)
    x2 = r1.astype(BF16)
    r2 = r1 - x2.astype(F32)
    return x1, x2, r2.astype(BF16)


def _rmsnorm_kernel(x_ref, g_ref, o_ref):
    x = x_ref[...]
    ms = jnp.mean(x * x, axis=-1, keepdims=True)
    o_ref[...] = (x * lax.rsqrt(ms + EPS) * g_ref[...]).astype(o_ref.dtype)


def _rmsnorm(x, g, tm=256):
    T, D = x.shape
    tm = min(tm, T)
    return pl.pallas_call(
        _rmsnorm_kernel,
        grid=(T // tm,),
        in_specs=[pl.BlockSpec((tm, D), lambda i: (i, 0)),
                  pl.BlockSpec((1, D), lambda i: (0, 0))],
        out_specs=pl.BlockSpec((tm, D), lambda i: (i, 0)),
        out_shape=jax.ShapeDtypeStruct((T, D), BF16),
        compiler_params=_params("parallel"),
        name="rmsnorm",
    )(x, g.reshape(1, D).astype(F32))


def _group_rstd(y):
    out = []
    for g in range(y.shape[1] // HEAD_DIM):
        yg = y[:, g * HEAD_DIM:(g + 1) * HEAD_DIM]
        out.append(lax.rsqrt(jnp.mean(yg * yg, axis=-1, keepdims=True) + EPS))
    return out


def _mm_kernel(*refs, epilogue, nk, tn, nt, gain_on, cfg):
    refs = list(refs)
    a_ref, w_ref = refs[:2]
    n_in = 2 + (gain_on is not None) + (epilogue in ("resid", "kvnorm"))
    g_ref = refs[2] if gain_on is not None else None
    e_ref = refs[n_in - 1] if epilogue in ("resid", "kvnorm") else None
    o_ref = refs[n_in]
    j = pl.program_id(1)
    k = pl.program_id(2)

    def product():
        a = a_ref[...]
        w = w_ref[...]
        if gain_on == "a":
            a = a * g_ref[...]
        elif gain_on == "w" and nt:
            w = w * g_ref[...]
        elif gain_on == "w":
            g = g_ref[...]
            w = jnp.concatenate([w[:, c * LANES:(c + 1) * LANES] * g for c in range(w.shape[1] // LANES)], axis=1)
        a = a.astype(BF16)
        w = w.astype(BF16)
        return _dot_nt(a, w) if nt else _dot(a, w)

    if nk > 1:
        assert epilogue == "resid" and o_ref.dtype == F32

        @pl.when(k == 0)
        def _():
            o_ref[...] = e_ref[...]

        o_ref[...] += product()
        return

    part = product()

    def finish(y):
        if epilogue == "none":
            o_ref[...] = y.astype(o_ref.dtype)
        elif epilogue == "relu2":
            r = jnp.maximum(y, 0.0)
            o_ref[...] = (r * r).astype(o_ref.dtype)
        elif epilogue == "sigmoid":
            o_ref[...] = jax.nn.sigmoid(y).astype(o_ref.dtype)
        elif epilogue == "resid":
            o_ref[...] = (e_ref[...] + y).astype(o_ref.dtype)
        elif epilogue == "headnorm":
            rstd = _group_rstd(y)
            for g, r in enumerate(rstd):
                sl = slice(g * HEAD_DIM, (g + 1) * HEAD_DIM)
                o_ref[:, sl] = (y[:, sl] * r).astype(o_ref.dtype)
        elif epilogue == "kvnorm":
            slot = (j * tn) // cfg["kv_dim"]
            is_norm = functools.reduce(jnp.logical_or, [slot == s for s in cfg["norm_slots"]])
            rstd = _group_rstd(y)
            gain = e_ref[...]
            for g, r in enumerate(rstd):
                sl = slice(g * HEAD_DIM, (g + 1) * HEAD_DIM)
                mult = jnp.where(is_norm, r, 1.0)
                o_ref[:, sl] = (y[:, sl] * mult * gain[:, sl]).astype(o_ref.dtype)

            blk_ref, y_scr = refs[n_in + 1], refs[n_in + 2]
            rows = y.shape[0] // CMP_STRIDE

            @pl.when(slot < 2)
            def _():
                for g in range(y.shape[1] // HEAD_DIM):
                    y_scr[g] = y[:, g * HEAD_DIM:(g + 1) * HEAD_DIM]
                    for r in range(CMP_STRIDE):
                        piece = y_scr[g, pl.ds(r, rows, stride=CMP_STRIDE), :]
                        blk_ref[g, :, r * HEAD_DIM:(r + 1) * HEAD_DIM] = piece.astype(blk_ref.dtype)
        elif epilogue == "mlstm_in":
            col0 = j * tn
            qk = cfg["m_qk"]
            is_k = jnp.logical_and(col0 >= qk, col0 < 2 * qk)
            o_ref[...] = (y * jnp.where(is_k, cfg["k_scale"], 1.0)).astype(o_ref.dtype)
        else:
            raise ValueError(epilogue)

    finish(part)


def _matmul(a, w, *, layer=None, nt=False, n_out=None, col_map=None, epilogue="none", extra=None, gain=None,
            gain_on=None, out_dtype=BF16, tm=2048, tn=512, tk=4096, cfg=None, name="matmul"):
    M, K = a.shape
    wshape = w.shape[1:] if layer is not None else w.shape
    N = n_out if n_out is not None else (wshape[0] if nt else wshape[1])
    tm, tn, tk = min(tm, M), min(tn, N), min(tk, K)
    assert M % tm == 0 and N % tn == 0 and K % tk == 0
    nk = K // tk
    cm = col_map if col_map is not None else (lambda j: j)
    if layer is not None and nt:
        w_spec = pl.BlockSpec((None, tn, tk), lambda i, j, k: (layer, cm(j), k))
    elif layer is not None:
        w_spec = pl.BlockSpec((None, tk, tn), lambda i, j, k: (layer, k, cm(j)))
    elif nt:
        w_spec = pl.BlockSpec((tn, tk), lambda i, j, k: (cm(j), k))
    else:
        w_spec = pl.BlockSpec((tk, tn), lambda i, j, k: (k, cm(j)))
    a_mode = dict(pipeline_mode=pl.Buffered(1)) if (nk == 1 and N // tn > 1) else {}
    in_specs = [pl.BlockSpec((tm, tk), lambda i, j, k: (i, k), **a_mode), w_spec]
    args = [a, w]
    if gain_on is not None:
        if gain_on == "w" and not nt:
            in_specs.append(pl.BlockSpec((tk, LANES), lambda i, j, k: (k, 0)))
            args.append(jnp.broadcast_to(gain.astype(F32).reshape(K, 1), (K, LANES)))
        else:
            in_specs.append(pl.BlockSpec((1, tk), lambda i, j, k: (0, k)))
            args.append(gain.astype(F32).reshape(1, K))
    if epilogue == "resid":
        e_mode = dict(pipeline_mode=pl.Buffered(1)) if nk > 1 else {}
        in_specs.append(pl.BlockSpec((tm, tn), lambda i, j, k: (i, j), **e_mode))
        args.append(extra)
    elif epilogue == "kvnorm":
        in_specs.append(pl.BlockSpec((1, tn), lambda i, j, k: (0, j)))
        args.append(extra)
    out_specs = pl.BlockSpec((tm, tn), lambda i, j, k: (i, j))
    out_shape = jax.ShapeDtypeStruct((M, N), out_dtype)
    scratch = []
    semantics = ("parallel", "parallel", "arbitrary")
    if epilogue == "kvnorm":
        seq, groups = cfg["seq"], tn // HEAD_DIM
        assert seq % tm == 0 and tm % CMP_STRIDE == 0 and tn == cfg["kv_dim"]
        per_seq = seq // tm
        out_specs = [out_specs, pl.BlockSpec((None, None, groups, tm // CMP_STRIDE, CMP_STRIDE * HEAD_DIM),
                                             lambda i, j, k: (jnp.minimum(j, 1), i // per_seq, 0, i % per_seq, 0))]
        out_shape = [out_shape, jax.ShapeDtypeStruct((2, M // seq, groups, seq // CMP_STRIDE, CMP_STRIDE * HEAD_DIM),
                                                     out_dtype)]
        scratch = [pltpu.VMEM((groups, tm, HEAD_DIM), F32)]
        semantics = ("parallel", "arbitrary", "arbitrary")
    return pl.pallas_call(
        functools.partial(_mm_kernel, epilogue=epilogue, nk=nk, tn=tn, nt=nt, gain_on=gain_on, cfg=cfg),
        grid=(M // tm, N // tn, nk),
        in_specs=in_specs,
        out_specs=out_specs,
        out_shape=out_shape,
        scratch_shapes=scratch,
        compiler_params=_params(*semantics),
        name=name,
    )(*args)


def _log_sigmoid(x):
    return jnp.minimum(x, 0.0) - jnp.log(1.0 + jnp.exp(-jnp.abs(x)))


def _gates_kernel(xn_ref, wg_ref, wgt_ref, brow_ref, bcol_ref, gcol_ref, grow_ref):
    xn = xn_ref[...]
    pre_c = _dot(xn, wg_ref[...]) + brow_ref[...]
    g_c = GATE_CAP * jnp.tanh(pre_c / GATE_CAP)
    col = lax.broadcasted_iota(jnp.int32, g_c.shape, 1)
    gcol_ref[...] = jnp.where(col < M_HEADS, g_c, _log_sigmoid(g_c))
    pre_r = _dot_nt(wgt_ref[...], xn) + bcol_ref[...]
    g_r = GATE_CAP * jnp.tanh(pre_r / GATE_CAP)
    row = lax.broadcasted_iota(jnp.int32, g_r.shape, 0)
    grow_ref[...] = jnp.where(row < M_HEADS, g_r, _log_sigmoid(g_r))


def _mlstm_gates(xn, w_g, b_g, tm=512):
    T, D = xn.shape
    ng = 2 * M_HEADS
    tm = min(tm, T)
    wg = jnp.zeros((D, LANES), F32).at[:, :ng].set(w_g).astype(BF16)
    wgt = w_g.T.astype(BF16)
    brow = jnp.zeros((1, LANES), F32).at[0, :ng].set(b_g)
    bcol = b_g.reshape(ng, 1).astype(F32)
    return pl.pallas_call(
        _gates_kernel,
        grid=(T // tm,),
        in_specs=[pl.BlockSpec((tm, D), lambda i: (i, 0)),
                  pl.BlockSpec((D, LANES), lambda i: (0, 0)),
                  pl.BlockSpec((ng, D), lambda i: (0, 0)),
                  pl.BlockSpec((1, LANES), lambda i: (0, 0)),
                  pl.BlockSpec((ng, 1), lambda i: (0, 0))],
        out_specs=[pl.BlockSpec((tm, LANES), lambda i: (i, 0)),
                   pl.BlockSpec((ng, tm), lambda i: (0, i))],
        out_shape=[jax.ShapeDtypeStruct((T, LANES), F32),
                   jax.ShapeDtypeStruct((ng, T), F32)],
        compiler_params=_params("parallel"),
        name="mlstm_gates",
    )(xn, wg, wgt, brow, bcol)


def _mlstm_kernel(q_ref, k_ref, v_ref, o_ref, gc_ref, gr_ref, out_ref, c_ref, m_ref, *, dk, dv):
    L = q_ref.shape[0]

    @pl.when(pl.program_id(1) == 0)
    def _():
        c_ref[...] = jnp.zeros_like(c_ref)
        m_ref[...] = jnp.zeros_like(m_ref)

    row = lax.broadcasted_iota(jnp.int32, (L, L), 0)
    col = lax.broadcasted_iota(jnp.int32, (L, L), 1)
    causal = col <= row
    tril = jnp.where(causal, 1.0, 0.0).astype(BF16)
    triu = jnp.where(row <= col, 1.0, 0.0).astype(BF16)

    gc = gc_ref[...]
    gr = gr_ref[...]
    c1, c2, c3 = _split3(gc)
    b_col_all = _dot(tril, c1) + _dot(tril, c2) + _dot(tril, c3)
    r1, r2, r3 = _split3(gr)
    b_row_all = _dot(r1, triu) + _dot(r2, triu) + _dot(r3, triu)

    ones_blk = jnp.where(lax.broadcasted_iota(jnp.int32, (L, LANES), 1) == 0, 1.0, 0.0).astype(BF16)

    for h in range(M_HEADS):
        li_row = gr[h:h + 1, :]
        li_col = gc[:, h:h + 1]
        b_row = b_row_all[M_HEADS + h:M_HEADS + h + 1, :]
        b_col = b_col_all[:, M_HEADS + h:M_HEADS + h + 1]
        m_prev = m_ref[h]

        log_d = jnp.where(causal, b_col + (li_row - b_row), -jnp.inf)
        log_inter = b_col + m_prev
        m_t = jnp.maximum(log_inter, jnp.max(log_d, axis=1, keepdims=True))
        w_intra = jnp.exp(log_d - m_t)
        w_inter = jnp.exp(log_inter - m_t)

        qh = q_ref[:, h * dk:(h + 1) * dk]
        kh = k_ref[:, h * dk:(h + 1) * dk]
        v_aug = jnp.concatenate([v_ref[:, h * dv:(h + 1) * dv], ones_blk], axis=1)
        c_old = c_ref[h]

        s = (_dot_nt(qh, kh) * w_intra).astype(BF16)
        qs = (qh.astype(F32) * w_inter).astype(BF16)
        numden = _dot(qs, c_old.astype(BF16)) + _dot(s, v_aug)
        num = numden[:, 0:dv]
        den = numden[:, dv:dv + 1]
        inv = 1.0 / jnp.maximum(jnp.abs(den), jnp.exp(-m_t))
        ssq = jnp.sum(num * num, axis=1, keepdims=True)
        scale = inv * lax.rsqrt(inv * inv * ssq * (1.0 / dv) + EPS)
        gate_o = jax.nn.sigmoid(o_ref[:, h * dv:(h + 1) * dv].astype(F32))
        out_ref[:, h * dv:(h + 1) * dv] = (gate_o * (num * scale)).astype(out_ref.dtype)

        b_last = b_col[L - 1:L, :]
        log_w = b_last - b_col + li_col
        m_new = jnp.maximum(b_last + m_prev, jnp.max(log_w, axis=0, keepdims=True))
        wk = jnp.exp(log_w - m_new) * kh.astype(F32)
        decay = jnp.exp(b_last + m_prev - m_new)
        c_ref[h] = decay * c_old + _dot(wk.T.astype(BF16), v_aug)
        m_ref[h] = m_new


def _mlstm(proj, gcol, grow, *, B, S, D):
    T = B * S
    dv = D // M_HEADS
    dk = dv // 2
    qk = M_HEADS * dk
    L = min(M_CHUNK, S)
    nc = S // L
    rb = lambda b, c: b * nc + c
    assert D % qk == 0
    return pl.pallas_call(
        functools.partial(_mlstm_kernel, dk=dk, dv=dv),
        grid=(B, nc),
        in_specs=[pl.BlockSpec((L, qk), lambda b, c: (rb(b, c), 0)),
                  pl.BlockSpec((L, qk), lambda b, c: (rb(b, c), 1)),
                  pl.BlockSpec((L, D), lambda b, c: (rb(b, c), (2 * qk) // D)),
                  pl.BlockSpec((L, D), lambda b, c: (rb(b, c), (2 * qk) // D + 1)),
                  pl.BlockSpec((L, LANES), lambda b, c: (rb(b, c), 0)),
                  pl.BlockSpec((2 * M_HEADS, L), lambda b, c: (0, rb(b, c)))],
        out_specs=pl.BlockSpec((L, D), lambda b, c: (rb(b, c), 0)),
        out_shape=jax.ShapeDtypeStruct((T, D), BF16),
        scratch_shapes=[pltpu.VMEM((M_HEADS, dk, dv + LANES), F32),
                        pltpu.VMEM((M_HEADS, 1, 1), F32)],
        compiler_params=_params("parallel", "arbitrary"),
        name="mlstm",
    )(proj, proj, proj, proj, gcol, grow)


def _compress_kernel(a_ref, pos_ref, w1_ref, w2k_ref, w2vt_ref, gain_ref, ok_ref, ovt_ref):
    half = a_ref.shape[2]
    ncp = a_ref.shape[1]

    def hidden(j):
        a = a_ref[j].astype(F32)
        lo = (a + pos_ref[j, 0:1, :]).astype(BF16)
        hi = (a + pos_ref[j, 1:2, :]).astype(BF16)
        u = _dot(lo, w1_ref[j, 0:half, :])
        v = _dot(hi, w1_ref[j, half:2 * half, :])
        hid = u + pltpu.roll(v, shift=ncp - 1, axis=0)
        c0 = float(np.sqrt(2.0 / np.pi))
        cdf = 0.5 * (1.0 + jnp.tanh(c0 * (hid + 0.044715 * (hid * hid * hid))))
        return (hid * cdf).astype(BF16)

    yk = _dot(hidden(0), w2k_ref[...])
    rstd = lax.rsqrt(jnp.mean(yk * yk, axis=-1, keepdims=True) + EPS)
    ok_ref[...] = (yk * rstd * gain_ref[...]).astype(ok_ref.dtype)
    ovt_ref[...] = _dot_nt(w2vt_ref[...], hidden(1)).astype(ovt_ref.dtype)


def _compress(a2, pos2, w1, w2k, w2vt, gain):
    _, n, ncp, half = a2.shape
    hid = w1.shape[2]
    return pl.pallas_call(
        _compress_kernel,
        grid=(n,),
        in_specs=[pl.BlockSpec((2, None, ncp, half), lambda i: (0, i, 0, 0)),
                  pl.BlockSpec((2, 2, half), lambda i: (0, 0, 0)),
                  pl.BlockSpec((2, 2 * half, hid), lambda i: (0, 0, 0)),
                  pl.BlockSpec((hid, HEAD_DIM), lambda i: (0, 0)),
                  pl.BlockSpec((HEAD_DIM, hid), lambda i: (0, 0)),
                  pl.BlockSpec((1, HEAD_DIM), lambda i: (0, 0))],
        out_specs=[pl.BlockSpec((None, ncp, HEAD_DIM), lambda i: (i, 0, 0)),
                   pl.BlockSpec((None, HEAD_DIM, ncp), lambda i: (i, 0, 0))],
        out_shape=[jax.ShapeDtypeStruct((n, ncp, HEAD_DIM), BF16),
                   jax.ShapeDtypeStruct((n, HEAD_DIM, ncp), BF16)],
        compiler_params=_params("parallel"),
        name="nsa_compress",
    )(a2, pos2, w1, w2k, w2vt, gain)


def _nsa_kernel(q_ref, kc_ref, vct_ref, ks_ref, vst_ref, kw_ref, vwt_ref, gate_ref, mt_ref, o_ref,
                qa_ref, out_acc, acc_ref, m_ref, l_ref, s_ref, *, tq, hpg, rank_w):
    qi = pl.program_id(2)
    q0 = pl.multiple_of(qi * tq, tq)
    hd = HEAD_DIM
    t_lane = q0 + lax.broadcasted_iota(jnp.int32, (1, tq), 1)

    def gate(br, h):
        r = br * hpg + h
        return gate_ref[r:r + 1, :]

    def q_head(h):
        return q_ref[:, h * hd:(h + 1) * hd]

    ncp = kc_ref.shape[0]
    c_sub = lax.broadcasted_iota(jnp.int32, (ncp, 1), 0)
    cmp_valid = (c_sub * CMP_STRIDE + (CMP_BLOCK - 1)) <= t_lane
    kc = kc_ref[...]
    vct = vct_ref[...]
    psum = jnp.zeros((ncp, tq), F32)
    for h in range(hpg):
        s = jnp.where(cmp_valid, _dot_nt(kc, q_head(h)), NEG_BIG)
        e = jnp.exp2(s - jnp.max(s, axis=0, keepdims=True))
        p = jnp.where(cmp_valid, e * (1.0 / jnp.sum(e, axis=0, keepdims=True)), 0.0)
        psum = psum + p
        out_acc[h] = gate(0, h) * _dot(vct, p.astype(BF16))

    mt = mt_ref[...]
    n_sb = mt.shape[0]
    p1, p2, p3 = _split3(psum)
    p_blk = _dot(mt, p1) + _dot(mt, p2) + _dot(mt, p3)
    blk = lax.broadcasted_iota(jnp.int32, (n_sb, 1), 0)
    cur = t_lane // SEL_BLOCK
    forced = jnp.logical_or(blk == cur, blk == 0)
    val = jnp.where(forced, jnp.inf, jnp.where(blk <= cur, p_blk, -jnp.inf))

    sub = lax.broadcasted_iota(jnp.int32, (SUBLANES, 1), 0)
    n_rb = n_sb // SUBLANES
    top = float(min(N_SEL, n_sb))
    bias_cols = []
    for c in range(tq // rank_w):
        v = val[:, c * rank_w:(c + 1) * rank_w]
        rows = [v[rb * SUBLANES:(rb + 1) * SUBLANES, :] for rb in range(n_rb)]
        cnt = [jnp.zeros((SUBLANES, rank_w), F32) for _ in range(n_rb)]
        for i in range(n_sb):
            vi = v[i:i + 1, :]
            for rb in range(n_rb):
                lo = rb * SUBLANES
                ge = lambda: jnp.where(vi >= rows[rb], 1.0, 0.0)
                gt = lambda: jnp.where(vi > rows[rb], 1.0, 0.0)
                if lo > i:
                    beats = ge()
                elif lo + SUBLANES - 1 <= i:
                    beats = gt()
                else:
                    beats = jnp.where(sub > (i - lo), ge(), gt())
                cnt[rb] = cnt[rb] + beats
        cnt = jnp.concatenate(cnt, axis=0)
        cur_c = cur[:, c * rank_w:(c + 1) * rank_w]
        picked = jnp.where(cnt < top, jnp.where(blk <= cur_c, 0.0, MASK_VAL), MASK_VAL)
        bias_cols.append(picked)
    sel_bias = jnp.concatenate(bias_cols, axis=1)
    sel_bias = jnp.concatenate([sel_bias, jnp.zeros((LANES - n_sb, tq), F32)], axis=0)
    sb_t = sel_bias.T.astype(BF16)
    for h in range(hpg):
        qa_ref[h, :, 0:hd] = q_head(h)
        qa_ref[h, :, hd:2 * hd] = sb_t

    def flash_reset():
        m_ref[...] = jnp.full(m_ref.shape, NEG_BIG, F32)
        l_ref[...] = jnp.zeros_like(l_ref)
        acc_ref[...] = jnp.zeros_like(acc_ref)

    def flash_step(k_t, vt_t, qsel, mask):
        tile_max = []
        for h in range(hpg):
            s = _dot_nt(k_t, qsel(h))
            if mask is not None:
                s = jnp.where(mask, s, MASK_VAL)
            s_ref[h] = s
            tile_max.append(jnp.max(s, axis=0, keepdims=True))
        for h in range(hpg):
            m_old = m_ref[h]
            m_new = jnp.maximum(m_old, tile_max[h])
            alpha = jnp.exp2(m_old - m_new)
            p = jnp.exp2(s_ref[h] - m_new)
            l_ref[h] = alpha * l_ref[h] + jnp.sum(p, axis=0, keepdims=True)
            acc_ref[h] = alpha * acc_ref[h] + _dot(vt_t, p.astype(BF16))
            m_ref[h] = m_new

    def flash_out(h):
        return acc_ref[h] * (1.0 / l_ref[h])

    tk = tq
    k_sub = lax.broadcasted_iota(jnp.int32, (tk, 1), 0)
    q_lane = lax.broadcasted_iota(jnp.int32, (1, tq), 1)
    causal = k_sub <= q_lane
    lane_j = lax.broadcasted_iota(jnp.int32, (1, LANES), 1)
    key_blk = k_sub // SEL_BLOCK

    def ks_aug(k0):
        onehot = jnp.where(lane_j == key_blk + k0 // SEL_BLOCK, 1.0, 0.0).astype(BF16)
        return jnp.concatenate([ks_ref[pl.ds(k0, tk), :], onehot], axis=1)

    qa_head = lambda h: qa_ref[h]

    flash_reset()

    def sel_body(kt, carry):
        k0 = pl.multiple_of(kt * tk, tk)
        flash_step(ks_aug(k0), vst_ref[:, pl.ds(k0, tk)], qa_head, None)
        return carry

    lax.fori_loop(0, qi, sel_body, 0)
    flash_step(ks_aug(q0), vst_ref[:, pl.ds(q0, tk)], qa_head, causal)
    for h in range(hpg):
        out_acc[h] = out_acc[h] + gate(1, h) * flash_out(h)

    flash_reset()
    flash_step(kw_ref[pl.ds(q0, tk), :], vwt_ref[:, pl.ds(q0, tk)], q_head, causal)
    d = 1
    while d * tq - (tk - 1) < WINDOW:
        all_valid = d * tq + (tq - 1) < WINDOW
        mask = None if all_valid else (q_lane + d * tq - k_sub) < WINDOW

        @pl.when(qi >= d)
        def _(d=d, mask=mask):
            k0 = pl.multiple_of(q0 - d * tq, tk)
            flash_step(kw_ref[pl.ds(k0, tk), :], vwt_ref[:, pl.ds(k0, tk)], q_head, mask)
        d += 1

    for h in range(hpg):
        o = out_acc[h] + gate(2, h) * flash_out(h)
        o_ref[:, h * hd:(h + 1) * hd] = o.T.astype(o_ref.dtype)


def _cmp_to_sel_t(ncp, n_cmp, n_sb):
    c0 = np.arange(ncp)[None, :] * CMP_STRIDE
    s0 = np.arange(n_sb)[:, None] * SEL_BLOCK
    ov = np.minimum(c0 + CMP_BLOCK, s0 + SEL_BLOCK) - np.maximum(c0, s0)
    w = np.maximum(ov, 0) / CMP_BLOCK
    w = w * (np.arange(ncp)[None, :] < n_cmp)
    return jnp.asarray(w, dtype=BF16)


def _nsa_attention(qhat, k_cmp, v_cmp_t, k_tok, v_t, gates_t, *, B, S, D, tq=512):
    T = B * S
    hpg = N_HEADS // N_KV
    gw = hpg * HEAD_DIM
    tq = min(tq, S)
    nq = S // tq
    ncp = k_cmp.shape[1]
    n_cmp = (S - CMP_BLOCK) // CMP_STRIDE + 1
    n_sb = S // SEL_BLOCK
    assert n_sb % SUBLANES == 0 and n_sb <= LANES
    mt = _cmp_to_sel_t(ncp, n_cmp, n_sb)
    grow = gates_t.shape[0] // N_KV
    return pl.pallas_call(
        functools.partial(_nsa_kernel, tq=tq, hpg=hpg, rank_w=min(256, tq)),
        grid=(B, N_KV, nq),
        in_specs=[pl.BlockSpec((tq, gw), lambda b, g, i: (b * nq + i, g)),
                  pl.BlockSpec((None, ncp, HEAD_DIM), lambda b, g, i: (b * N_KV + g, 0, 0)),
                  pl.BlockSpec((None, HEAD_DIM, ncp), lambda b, g, i: (b * N_KV + g, 0, 0)),
                  pl.BlockSpec((S, HEAD_DIM), lambda b, g, i: (b, 2 * N_KV + g)),
                  pl.BlockSpec((HEAD_DIM, S), lambda b, g, i: (g, b)),
                  pl.BlockSpec((S, HEAD_DIM), lambda b, g, i: (b, 3 * N_KV + g)),
                  pl.BlockSpec((HEAD_DIM, S), lambda b, g, i: (N_KV + g, b)),
                  pl.BlockSpec((grow, tq), lambda b, g, i: (g, b * nq + i)),
                  pl.BlockSpec((n_sb, ncp), lambda b, g, i: (0, 0))],
        out_specs=pl.BlockSpec((tq, gw), lambda b, g, i: (b * nq + i, g)),
        out_shape=jax.ShapeDtypeStruct((T, D), BF16),
        scratch_shapes=[pltpu.VMEM((hpg, tq, 2 * HEAD_DIM), BF16),
                        pltpu.VMEM((hpg, HEAD_DIM, tq), F32),
                        pltpu.VMEM((hpg, HEAD_DIM, tq), F32),
                        pltpu.VMEM((hpg, 1, tq), F32),
                        pltpu.VMEM((hpg, 1, tq), F32),
                        pltpu.VMEM((hpg, tq, tq), F32)],
        compiler_params=_params("parallel", "parallel", "parallel"),
        name="nsa_attention",
    )(qhat, k_cmp, v_cmp_t, k_tok, v_t, k_tok, v_t, gates_t, mt)


def _mlp(x, g, w_up, w_down, layer):
    xn = _rmsnorm(x, g)
    h = _matmul(xn, w_up, layer=layer, epilogue="relu2", name="mlp_up")
    return _matmul(h, w_down, layer=layer, epilogue="resid", extra=x, out_dtype=F32, tm=2048, tn=1024, tk=1024,
                   name="mlp_down")


def kernel(x, attn_norm_g, mlp_norm_g, m_w_in, m_b_gate, m_head_g, m_w_out, kv_norm_g, w_kv, k_norm_g,
           cmp_pos, cmp_w1, cmp_w2, n_w_qg, q_norm_g, n_w_out, mlp_w_up, mlp_w_down):
    B, S, D = x.shape
    T = B * S
    xf = x.reshape(T, D)
    dv = D // M_HEADS
    dk = dv // 2
    qk = M_HEADS * dk
    nq = 2 * qk + 2 * D
    kv_dim = N_KV * HEAD_DIM
    hpg = N_HEADS // N_KV
    k_fold = (HEAD_DIM ** -0.5) * LOG2E

    xn = _rmsnorm(xf, attn_norm_g[0])
    proj = _matmul(xn, jnp.swapaxes(m_w_in, 1, 2), layer=0, nt=True, n_out=nq, epilogue="mlstm_in", tn=min(512, qk),
                   cfg=dict(m_qk=qk, k_scale=float(dk ** -0.5)), name="mlstm_in_proj")
    gcol, grow = _mlstm_gates(xn, m_w_in[0, :, nq:], m_b_gate[0])
    hg = _mlstm(proj, gcol, grow, B=B, S=S, D=D)
    xf = _matmul(hg, m_w_out, layer=0, epilogue="resid", extra=xf, gain=m_head_g[0], gain_on="w", out_dtype=F32,
                 name="mlstm_out_proj")
    xf = _mlp(xf, mlp_norm_g[0], mlp_w_up, mlp_w_down, 0)

    qg = q_norm_g[0]
    ones = jnp.ones((kv_dim,), F32)
    kv_gain = jnp.concatenate([
        ones, ones,
        jnp.tile(k_norm_g[1] * qg[1] * k_fold, N_KV),
        jnp.tile(k_norm_g[2] * qg[2] * k_fold, N_KV)]).reshape(1, 4 * kv_dim)
    xh = _rmsnorm(xf, jnp.ones((D,), F32))
    k_tok, a2 = _matmul(xh, w_kv, n_out=4 * kv_dim, col_map=lambda j: j + j // 3, epilogue="kvnorm", extra=kv_gain,
                        gain=kv_norm_g, gain_on="w", tn=kv_dim, cfg=dict(kv_dim=kv_dim, norm_slots=(2, 3), seq=S),
                        name="nsa_kv_proj")
    w_vt = jnp.concatenate([w_kv[:, 3 * kv_dim:4 * kv_dim], w_kv[:, 5 * kv_dim:6 * kv_dim]], axis=1).T
    v_t = _matmul(w_vt, xh, nt=True, gain=kv_norm_g, gain_on="a", tm=kv_dim, tn=1024,
                  name="nsa_vt_proj")
    ncp = S // CMP_STRIDE
    a2 = a2.reshape(2, B * N_KV, ncp, CMP_STRIDE * HEAD_DIM)
    pos2 = cmp_pos.reshape(2, 2, CMP_STRIDE * HEAD_DIM).astype(F32)
    cmp_gain = (k_norm_g[0] * qg[0] * k_fold).reshape(1, HEAD_DIM)
    k_cmp, v_cmp_t = _compress(a2, pos2, cmp_w1.astype(BF16), cmp_w2[0].astype(BF16), cmp_w2[1].T.astype(BF16),
                               cmp_gain)

    w_qg = n_w_qg[0]
    qhat = _matmul(xh, jnp.swapaxes(n_w_qg, 1, 2), layer=0, nt=True, n_out=D, epilogue="headnorm",
                   gain=attn_norm_g[1], gain_on="w", name="nsa_q_proj")
    grow_n = -(-N_BRANCH * hpg // SUBLANES) * SUBLANES
    w_gate = w_qg[:, D:].reshape(D, N_BRANCH, N_KV, hpg).transpose(2, 1, 3, 0).reshape(N_KV, N_BRANCH * hpg, D)
    w_gate = jnp.zeros((N_KV, grow_n, D), F32).at[:, :N_BRANCH * hpg].set(w_gate).reshape(N_KV * grow_n, D)
    gates_t = _matmul(w_gate, xh, nt=True, epilogue="sigmoid", gain=attn_norm_g[1], gain_on="a", out_dtype=F32,
                      tn=1024, name="nsa_gate_proj")
    o = _nsa_attention(qhat, k_cmp, v_cmp_t, k_tok, v_t, gates_t, B=B, S=S, D=D)
    xf = _matmul(o, n_w_out, layer=0, epilogue="resid", extra=xf, out_dtype=F32, name="nsa_out_proj")
    xf = _mlp(xf, mlp_norm_g[1], mlp_w_up, mlp_w_down, 1)
    return xf.reshape(B, S, D)
```

```python
import functools

import numpy as np
import jax
import jax.numpy as jnp
from jax import lax
from jax.experimental import pallas as pl
from jax.experimental.pallas import tpu as pltpu

F32 = jnp.float32
BF16 = jnp.bfloat16

EPS = 1e-6
NEG_BIG = -1e30
MASK_VAL = -(2.0 ** 100)
LOG2E = 1.4426950408889634

M_HEADS = 8
GATE_CAP = 15.0
M_CHUNK = 256

N_HEADS = 32
N_KV = 4
HEAD_DIM = 128
N_BRANCH = 3
CMP_BLOCK = 32
CMP_STRIDE = 16
SEL_BLOCK = 64
N_SEL = 16
WINDOW = 512

LANES = 128
SUBLANES = 8
VMEM_LIMIT = 56 * 1024 * 1024


def _params(*sem):
    return pltpu.CompilerParams(dimension_semantics=sem, vmem_limit_bytes=VMEM_LIMIT)


def _dot(a, b):
    return jnp.dot(a, b, preferred_element_type=F32)


def _dot_nt(a, b):
    return lax.dot_general(a, b, (((1,), (1,)), ((), ())), preferred_element_type=F32)


def _split3(x):
    x1 = x.astype(BF16)
    r1 = x - x1.astype(F32)
    x2 = r1.astype(BF16)
    r2 = r1 - x2.astype(F32)
    return x1, x2, r2.astype(BF16)


def _rmsnorm_kernel(x_ref, g_ref, o_ref):
    x = x_ref[...]
    ms = jnp.mean(x * x, axis=-1, keepdims=True)
    o_ref[...] = (x * lax.rsqrt(ms + EPS) * g_ref[...]).astype(o_ref.dtype)


def _rmsnorm(x, g, tm=256):
    T, D = x.shape
    tm = min(tm, T)
    return pl.pallas_call(
        _rmsnorm_kernel,
        grid=(T // tm,),
        in_specs=[pl.BlockSpec((tm, D), lambda i: (i, 0)),
                  pl.BlockSpec((1, D), lambda i: (0, 0))],
        out_specs=pl.BlockSpec((tm, D), lambda i: (i, 0)),
        out_shape=jax.ShapeDtypeStruct((T, D), BF16),
        compiler_params=_params("parallel"),
        name="rmsnorm",
    )(x, g.reshape(1, D).astype(F32))


def _group_rstd(y):
    out = []
    for g in range(y.shape[1] // HEAD_DIM):
        yg = y[:, g * HEAD_DIM:(g + 1) * HEAD_DIM]
        out.append(lax.rsqrt(jnp.mean(yg * yg, axis=-1, keepdims=True) + EPS))
    return out


def _mm_kernel(*refs, epilogue, nk, tn, nt, gain_on, cfg):
    refs = list(refs)
    a_ref, w_ref = refs[:2]
    n_in = 2 + (gain_on is not None) + (epilogue in ("resid", "kvnorm"))
    g_ref = refs[2] if gain_on is not None else None
    e_ref = refs[n_in - 1] if epilogue in ("resid", "kvnorm") else None
    o_ref = refs[n_in]
    j = pl.program_id(1)
    k = pl.program_id(2)

    def product():
        a = a_ref[...]
        w = w_ref[...]
        if gain_on == "a":
            a = a * g_ref[...]
        elif gain_on == "w" and nt:
            w = w * g_ref[...]
        elif gain_on == "w":
            g = g_ref[...]
            w = jnp.concatenate([w[:, c * LANES:(c + 1) * LANES] * g for c in range(w.shape[1] // LANES)], axis=1)
        a = a.astype(BF16)
        w = w.astype(BF16)
        return _dot_nt(a, w) if nt else _dot(a, w)

    if nk > 1:
        assert epilogue == "resid" and o_ref.dtype == F32 and gain_on is None and not nt

        @pl.when(k == 0)
        def _():
            o_ref[...] = e_ref[...]

        half = o_ref.shape[0] // 2
        w = w_ref[...].astype(BF16)
        for r in range(2):
            rows = slice(r * half, (r + 1) * half)
            o_ref[rows, :] += _dot(a_ref[rows, :], w)
        return

    part = product()

    def finish(y):
        if epilogue == "none":
            o_ref[...] = y.astype(o_ref.dtype)
        elif epilogue == "relu2":
            r = jnp.maximum(y, 0.0)
            o_ref[...] = (r * r).astype(o_ref.dtype)
        elif epilogue == "sigmoid":
            o_ref[...] = jax.nn.sigmoid(y).astype(o_ref.dtype)
        elif epilogue == "resid":
            o_ref[...] = (e_ref[...] + y).astype(o_ref.dtype)
        elif epilogue == "headnorm":
            rstd = _group_rstd(y)
            for g, r in enumerate(rstd):
                sl = slice(g * HEAD_DIM, (g + 1) * HEAD_DIM)
                o_ref[:, sl] = (y[:, sl] * r).astype(o_ref.dtype)
        elif epilogue == "kvnorm":
            slot = (j * tn) // cfg["kv_dim"]
            is_norm = functools.reduce(jnp.logical_or, [slot == s for s in cfg["norm_slots"]])
            rstd = _group_rstd(y)
            gain = e_ref[...]
            for g, r in enumerate(rstd):
                sl = slice(g * HEAD_DIM, (g + 1) * HEAD_DIM)
                mult = jnp.where(is_norm, r, 1.0)
                o_ref[:, sl] = (y[:, sl] * mult * gain[:, sl]).astype(o_ref.dtype)

            blk_ref, y_scr = refs[n_in + 1], refs[n_in + 2]
            rows = y.shape[0] // CMP_STRIDE

            @pl.when(slot < 2)
            def _():
                for g in range(y.shape[1] // HEAD_DIM):
                    y_scr[g] = y[:, g * HEAD_DIM:(g + 1) * HEAD_DIM]
                    for r in range(CMP_STRIDE):
                        piece = y_scr[g, pl.ds(r, rows, stride=CMP_STRIDE), :]
                        blk_ref[g, :, r * HEAD_DIM:(r + 1) * HEAD_DIM] = piece.astype(blk_ref.dtype)
        elif epilogue == "mlstm_in":
            col0 = j * tn
            qk = cfg["m_qk"]
            is_k = jnp.logical_and(col0 >= qk, col0 < 2 * qk)
            o_ref[...] = (y * jnp.where(is_k, cfg["k_scale"], 1.0)).astype(o_ref.dtype)
        else:
            raise ValueError(epilogue)

    finish(part)


def _matmul(a, w, *, layer=None, nt=False, n_out=None, col_map=None, epilogue="none", extra=None, gain=None,
            gain_on=None, out_dtype=BF16, tm=2048, tn=512, tk=4096, cfg=None, name="matmul"):
    M, K = a.shape
    wshape = w.shape[1:] if layer is not None else w.shape
    N = n_out if n_out is not None else (wshape[0] if nt else wshape[1])
    tm, tn, tk = min(tm, M), min(tn, N), min(tk, K)
    assert M % tm == 0 and N % tn == 0 and K % tk == 0
    nk = K // tk
    cm = col_map if col_map is not None else (lambda j: j)
    if layer is not None and nt:
        w_spec = pl.BlockSpec((None, tn, tk), lambda i, j, k: (layer, cm(j), k))
    elif layer is not None:
        w_spec = pl.BlockSpec((None, tk, tn), lambda i, j, k: (layer, k, cm(j)))
    elif nt:
        w_spec = pl.BlockSpec((tn, tk), lambda i, j, k: (cm(j), k))
    else:
        w_spec = pl.BlockSpec((tk, tn), lambda i, j, k: (k, cm(j)))
    a_mode = dict(pipeline_mode=pl.Buffered(1)) if (nk == 1 and N // tn > 1) else {}
    in_specs = [pl.BlockSpec((tm, tk), lambda i, j, k: (i, k), **a_mode), w_spec]
    args = [a, w]
    if gain_on is not None:
        if gain_on == "w" and not nt:
            in_specs.append(pl.BlockSpec((tk, LANES), lambda i, j, k: (k, 0)))
            args.append(jnp.broadcast_to(gain.astype(F32).reshape(K, 1), (K, LANES)))
        else:
            in_specs.append(pl.BlockSpec((1, tk), lambda i, j, k: (0, k)))
            args.append(gain.astype(F32).reshape(1, K))
    if epilogue == "resid":
        e_mode = dict(pipeline_mode=pl.Buffered(1)) if nk > 1 else {}
        in_specs.append(pl.BlockSpec((tm, tn), lambda i, j, k: (i, j), **e_mode))
        args.append(extra)
    elif epilogue == "kvnorm":
        in_specs.append(pl.BlockSpec((1, tn), lambda i, j, k: (0, j)))
        args.append(extra)
    o_mode = dict(pipeline_mode=pl.Buffered(1)) if nk > 1 else {}
    out_specs = pl.BlockSpec((tm, tn), lambda i, j, k: (i, j), **o_mode)
    out_shape = jax.ShapeDtypeStruct((M, N), out_dtype)
    scratch = []
    semantics = ("parallel", "parallel", "arbitrary")
    if epilogue == "kvnorm":
        seq, groups = cfg["seq"], tn // HEAD_DIM
        assert seq % tm == 0 and tm % CMP_STRIDE == 0 and tn == cfg["kv_dim"]
        per_seq = seq // tm
        out_specs = [out_specs, pl.BlockSpec((None, None, groups, tm // CMP_STRIDE, CMP_STRIDE * HEAD_DIM),
                                             lambda i, j, k: (jnp.minimum(j, 1), i // per_seq, 0, i % per_seq, 0))]
        out_shape = [out_shape, jax.ShapeDtypeStruct((2, M // seq, groups, seq // CMP_STRIDE, CMP_STRIDE * HEAD_DIM),
                                                     out_dtype)]
        scratch = [pltpu.VMEM((groups, tm, HEAD_DIM), F32)]
        semantics = ("parallel", "arbitrary", "arbitrary")
    return pl.pallas_call(
        functools.partial(_mm_kernel, epilogue=epilogue, nk=nk, tn=tn, nt=nt, gain_on=gain_on, cfg=cfg),
        grid=(M // tm, N // tn, nk),
        in_specs=in_specs,
        out_specs=out_specs,
        out_shape=out_shape,
        scratch_shapes=scratch,
        compiler_params=_params(*semantics),
        name=name,
    )(*args)


def _log_sigmoid(x):
    return jnp.minimum(x, 0.0) - jnp.log(1.0 + jnp.exp(-jnp.abs(x)))


def _gates_kernel(xn_ref, wg_ref, wgt_ref, brow_ref, bcol_ref, gcol_ref, grow_ref):
    xn = xn_ref[...]
    pre_c = _dot(xn, wg_ref[...]) + brow_ref[...]
    g_c = GATE_CAP * jnp.tanh(pre_c / GATE_CAP)
    col = lax.broadcasted_iota(jnp.int32, g_c.shape, 1)
    gcol_ref[...] = jnp.where(col < M_HEADS, g_c, _log_sigmoid(g_c))
    pre_r = _dot_nt(wgt_ref[...], xn) + bcol_ref[...]
    g_r = GATE_CAP * jnp.tanh(pre_r / GATE_CAP)
    row = lax.broadcasted_iota(jnp.int32, g_r.shape, 0)
    grow_ref[...] = jnp.where(row < M_HEADS, g_r, _log_sigmoid(g_r))


def _mlstm_gates(xn, w_g, b_g, tm=512):
    T, D = xn.shape
    ng = 2 * M_HEADS
    tm = min(tm, T)
    wg = jnp.zeros((D, LANES), F32).at[:, :ng].set(w_g).astype(BF16)
    wgt = w_g.T.astype(BF16)
    brow = jnp.zeros((1, LANES), F32).at[0, :ng].set(b_g)
    bcol = b_g.reshape(ng, 1).astype(F32)
    return pl.pallas_call(
        _gates_kernel,
        grid=(T // tm,),
        in_specs=[pl.BlockSpec((tm, D), lambda i: (i, 0)),
                  pl.BlockSpec((D, LANES), lambda i: (0, 0)),
                  pl.BlockSpec((ng, D), lambda i: (0, 0)),
                  pl.BlockSpec((1, LANES), lambda i: (0, 0)),
                  pl.BlockSpec((ng, 1), lambda i: (0, 0))],
        out_specs=[pl.BlockSpec((tm, LANES), lambda i: (i, 0)),
                   pl.BlockSpec((ng, tm), lambda i: (0, i))],
        out_shape=[jax.ShapeDtypeStruct((T, LANES), F32),
                   jax.ShapeDtypeStruct((ng, T), F32)],
        compiler_params=_params("parallel"),
        name="mlstm_gates",
    )(xn, wg, wgt, brow, bcol)


def _mlstm_kernel(q_ref, k_ref, v_ref, o_ref, gc_ref, gr_ref, out_ref, c_ref, m_ref, *, dk, dv):
    L = q_ref.shape[0]

    @pl.when(pl.program_id(1) == 0)
    def _():
        c_ref[...] = jnp.zeros_like(c_ref)
        m_ref[...] = jnp.zeros_like(m_ref)

    row = lax.broadcasted_iota(jnp.int32, (L, L), 0)
    col = lax.broadcasted_iota(jnp.int32, (L, L), 1)
    causal = col <= row
    tril = jnp.where(causal, 1.0, 0.0).astype(BF16)
    triu = jnp.where(row <= col, 1.0, 0.0).astype(BF16)

    gc = gc_ref[...]
    gr = gr_ref[...]
    c1, c2, c3 = _split3(gc)
    b_col_all = _dot(tril, c1) + _dot(tril, c2) + _dot(tril, c3)
    r1, r2, r3 = _split3(gr)
    b_row_all = _dot(r1, triu) + _dot(r2, triu) + _dot(r3, triu)

    ones_blk = jnp.where(lax.broadcasted_iota(jnp.int32, (L, LANES), 1) == 0, 1.0, 0.0).astype(BF16)

    for h in range(M_HEADS):
        li_row = gr[h:h + 1, :]
        li_col = gc[:, h:h + 1]
        b_row = b_row_all[M_HEADS + h:M_HEADS + h + 1, :]
        b_col = b_col_all[:, M_HEADS + h:M_HEADS + h + 1]
        m_prev = m_ref[h]

        log_d = jnp.where(causal, b_col + (li_row - b_row), -jnp.inf)
        log_inter = b_col + m_prev
        m_t = jnp.maximum(log_inter, jnp.max(log_d, axis=1, keepdims=True))
        w_intra = jnp.exp(log_d - m_t)
        w_inter = jnp.exp(log_inter - m_t)

        qh = q_ref[:, h * dk:(h + 1) * dk]
        kh = k_ref[:, h * dk:(h + 1) * dk]
        v_aug = jnp.concatenate([v_ref[:, h * dv:(h + 1) * dv], ones_blk], axis=1)
        c_old = c_ref[h]

        s = (_dot_nt(qh, kh) * w_intra).astype(BF16)
        qs = (qh.astype(F32) * w_inter).astype(BF16)
        numden = _dot(qs, c_old.astype(BF16)) + _dot(s, v_aug)
        num = numden[:, 0:dv]
        den = numden[:, dv:dv + 1]
        inv = 1.0 / jnp.maximum(jnp.abs(den), jnp.exp(-m_t))
        ssq = jnp.sum(num * num, axis=1, keepdims=True)
        scale = inv * lax.rsqrt(inv * inv * ssq * (1.0 / dv) + EPS)
        gate_o = jax.nn.sigmoid(o_ref[:, h * dv:(h + 1) * dv].astype(F32))
        out_ref[:, h * dv:(h + 1) * dv] = (gate_o * (num * scale)).astype(out_ref.dtype)

        b_last = b_col[L - 1:L, :]
        log_w = b_last - b_col + li_col
        m_new = jnp.maximum(b_last + m_prev, jnp.max(log_w, axis=0, keepdims=True))
        wk = jnp.exp(log_w - m_new) * kh.astype(F32)
        decay = jnp.exp(b_last + m_prev - m_new)
        c_ref[h] = decay * c_old + _dot(wk.T.astype(BF16), v_aug)
        m_ref[h] = m_new


def _mlstm(proj, gcol, grow, *, B, S, D):
    T = B * S
    dv = D // M_HEADS
    dk = dv // 2
    qk = M_HEADS * dk
    L = min(M_CHUNK, S)
    nc = S // L
    rb = lambda b, c: b * nc + c
    assert D % qk == 0
    return pl.pallas_call(
        functools.partial(_mlstm_kernel, dk=dk, dv=dv),
        grid=(B, nc),
        in_specs=[pl.BlockSpec((L, qk), lambda b, c: (rb(b, c), 0)),
                  pl.BlockSpec((L, qk), lambda b, c: (rb(b, c), 1)),
                  pl.BlockSpec((L, D), lambda b, c: (rb(b, c), (2 * qk) // D)),
                  pl.BlockSpec((L, D), lambda b, c: (rb(b, c), (2 * qk) // D + 1)),
                  pl.BlockSpec((L, LANES), lambda b, c: (rb(b, c), 0)),
                  pl.BlockSpec((2 * M_HEADS, L), lambda b, c: (0, rb(b, c)))],
        out_specs=pl.BlockSpec((L, D), lambda b, c: (rb(b, c), 0)),
        out_shape=jax.ShapeDtypeStruct((T, D), BF16),
        scratch_shapes=[pltpu.VMEM((M_HEADS, dk, dv + LANES), F32),
                        pltpu.VMEM((M_HEADS, 1, 1), F32)],
        compiler_params=_params("parallel", "arbitrary"),
        name="mlstm",
    )(proj, proj, proj, proj, gcol, grow)


def _compress_kernel(a_ref, pos_ref, w1_ref, w2k_ref, w2vt_ref, gain_ref, ok_ref, ovt_ref):
    half = a_ref.shape[2]
    ncp = a_ref.shape[1]

    def hidden(j):
        a = a_ref[j].astype(F32)
        lo = (a + pos_ref[j, 0:1, :]).astype(BF16)
        hi = (a + pos_ref[j, 1:2, :]).astype(BF16)
        u = _dot(lo, w1_ref[j, 0:half, :])
        v = _dot(hi, w1_ref[j, half:2 * half, :])
        hid = u + pltpu.roll(v, shift=ncp - 1, axis=0)
        c0 = float(np.sqrt(2.0 / np.pi))
        cdf = 0.5 * (1.0 + jnp.tanh(c0 * (hid + 0.044715 * (hid * hid * hid))))
        return (hid * cdf).astype(BF16)

    yk = _dot(hidden(0), w2k_ref[...])
    rstd = lax.rsqrt(jnp.mean(yk * yk, axis=-1, keepdims=True) + EPS)
    ok_ref[...] = (yk * rstd * gain_ref[...]).astype(ok_ref.dtype)
    ovt_ref[...] = _dot_nt(w2vt_ref[...], hidden(1)).astype(ovt_ref.dtype)


def _compress(a2, pos2, w1, w2k, w2vt, gain):
    _, n, ncp, half = a2.shape
    hid = w1.shape[2]
    return pl.pallas_call(
        _compress_kernel,
        grid=(n,),
        in_specs=[pl.BlockSpec((2, None, ncp, half), lambda i: (0, i, 0, 0)),
                  pl.BlockSpec((2, 2, half), lambda i: (0, 0, 0)),
                  pl.BlockSpec((2, 2 * half, hid), lambda i: (0, 0, 0)),
                  pl.BlockSpec((hid, HEAD_DIM), lambda i: (0, 0)),
                  pl.BlockSpec((HEAD_DIM, hid), lambda i: (0, 0)),
                  pl.BlockSpec((1, HEAD_DIM), lambda i: (0, 0))],
        out_specs=[pl.BlockSpec((None, ncp, HEAD_DIM), lambda i: (i, 0, 0)),
                   pl.BlockSpec((None, HEAD_DIM, ncp), lambda i: (i, 0, 0))],
        out_shape=[jax.ShapeDtypeStruct((n, ncp, HEAD_DIM), BF16),
                   jax.ShapeDtypeStruct((n, HEAD_DIM, ncp), BF16)],
        compiler_params=_params("parallel"),
        name="nsa_compress",
    )(a2, pos2, w1, w2k, w2vt, gain)


def _nsa_kernel(q_ref, kc_ref, vct_ref, ks_ref, vst_ref, kw_ref, vwt_ref, gate_ref, mt_ref, o_ref,
                qa_ref, out_acc, acc_ref, m_ref, l_ref, s_ref, *, tq, hpg, rank_w):
    qi = pl.program_id(2)
    q0 = pl.multiple_of(qi * tq, tq)
    hd = HEAD_DIM
    t_lane = q0 + lax.broadcasted_iota(jnp.int32, (1, tq), 1)

    def gate(br, h):
        r = br * hpg + h
        return gate_ref[r:r + 1, :]

    def q_head(h):
        return q_ref[:, h * hd:(h + 1) * hd]

    ncp = kc_ref.shape[0]
    c_sub = lax.broadcasted_iota(jnp.int32, (ncp, 1), 0)
    cmp_valid = (c_sub * CMP_STRIDE + (CMP_BLOCK - 1)) <= t_lane
    kc = kc_ref[...]
    vct = vct_ref[...]
    psum = jnp.zeros((ncp, tq), F32)
    for h in range(hpg):
        s = jnp.where(cmp_valid, _dot_nt(kc, q_head(h)), NEG_BIG)
        e = jnp.exp2(s - jnp.max(s, axis=0, keepdims=True))
        p = jnp.where(cmp_valid, e * (1.0 / jnp.sum(e, axis=0, keepdims=True)), 0.0)
        psum = psum + p
        out_acc[h] = gate(0, h) * _dot(vct, p.astype(BF16))

    mt = mt_ref[...]
    n_sb = mt.shape[0]
    p1, p2, p3 = _split3(psum)
    p_blk = _dot(mt, p1) + _dot(mt, p2) + _dot(mt, p3)
    blk = lax.broadcasted_iota(jnp.int32, (n_sb, 1), 0)
    cur = t_lane // SEL_BLOCK
    forced = jnp.logical_or(blk == cur, blk == 0)
    val = jnp.where(forced, jnp.inf, jnp.where(blk <= cur, p_blk, -jnp.inf))

    sub = lax.broadcasted_iota(jnp.int32, (SUBLANES, 1), 0)
    n_rb = n_sb // SUBLANES
    top = float(min(N_SEL, n_sb))
    bias_cols = []
    for c in range(tq // rank_w):
        v = val[:, c * rank_w:(c + 1) * rank_w]
        rows = [v[rb * SUBLANES:(rb + 1) * SUBLANES, :] for rb in range(n_rb)]
        cnt = [jnp.zeros((SUBLANES, rank_w), F32) for _ in range(n_rb)]
        for i in range(n_sb):
            vi = v[i:i + 1, :]
            for rb in range(n_rb):
                lo = rb * SUBLANES
                ge = lambda: jnp.where(vi >= rows[rb], 1.0, 0.0)
                gt = lambda: jnp.where(vi > rows[rb], 1.0, 0.0)
                if lo > i:
                    beats = ge()
                elif lo + SUBLANES - 1 <= i:
                    beats = gt()
                else:
                    beats = jnp.where(sub > (i - lo), ge(), gt())
                cnt[rb] = cnt[rb] + beats
        cnt = jnp.concatenate(cnt, axis=0)
        cur_c = cur[:, c * rank_w:(c + 1) * rank_w]
        picked = jnp.where(cnt < top, jnp.where(blk <= cur_c, 0.0, MASK_VAL), MASK_VAL)
        bias_cols.append(picked)
    sel_bias = jnp.concatenate(bias_cols, axis=1)
    sel_bias = jnp.concatenate([sel_bias, jnp.zeros((LANES - n_sb, tq), F32)], axis=0)
    sb_t = sel_bias.T.astype(BF16)
    for h in range(hpg):
        qa_ref[h, :, 0:hd] = q_head(h)
        qa_ref[h, :, hd:2 * hd] = sb_t

    def flash_reset():
        m_ref[...] = jnp.full(m_ref.shape, NEG_BIG, F32)
        l_ref[...] = jnp.zeros_like(l_ref)
        acc_ref[...] = jnp.zeros_like(acc_ref)

    def flash_step(k_t, vt_t, qsel, mask):
        tile_max = []
        for h in range(hpg):
            s = _dot_nt(k_t, qsel(h))
            if mask is not None:
                s = jnp.where(mask, s, MASK_VAL)
            s_ref[h] = s
            tile_max.append(jnp.max(s, axis=0, keepdims=True))
        for h in range(hpg):
            m_old = m_ref[h]
            m_new = jnp.maximum(m_old, tile_max[h])
            alpha = jnp.exp2(m_old - m_new)
            p = jnp.exp2(s_ref[h] - m_new)
            l_ref[h] = alpha * l_ref[h] + jnp.sum(p, axis=0, keepdims=True)
            acc_ref[h] = alpha * acc_ref[h] + _dot(vt_t, p.astype(BF16))
            m_ref[h] = m_new

    def flash_out(h):
        return acc_ref[h] * (1.0 / l_ref[h])

    tk = tq
    k_sub = lax.broadcasted_iota(jnp.int32, (tk, 1), 0)
    q_lane = lax.broadcasted_iota(jnp.int32, (1, tq), 1)
    causal = k_sub <= q_lane
    lane_j = lax.broadcasted_iota(jnp.int32, (1, LANES), 1)
    key_blk = k_sub // SEL_BLOCK

    def ks_aug(k0):
        onehot = jnp.where(lane_j == key_blk + k0 // SEL_BLOCK, 1.0, 0.0).astype(BF16)
        return jnp.concatenate([ks_ref[pl.ds(k0, tk), :], onehot], axis=1)

    qa_head = lambda h: qa_ref[h]

    flash_reset()

    def sel_body(kt, carry):
        k0 = pl.multiple_of(kt * tk, tk)
        flash_step(ks_aug(k0), vst_ref[:, pl.ds(k0, tk)], qa_head, None)
        return carry

    lax.fori_loop(0, qi, sel_body, 0)
    flash_step(ks_aug(q0), vst_ref[:, pl.ds(q0, tk)], qa_head, causal)
    for h in range(hpg):
        out_acc[h] = out_acc[h] + gate(1, h) * flash_out(h)

    flash_reset()
    flash_step(kw_ref[pl.ds(q0, tk), :], vwt_ref[:, pl.ds(q0, tk)], q_head, causal)
    d = 1
    while d * tq - (tk - 1) < WINDOW:
        all_valid = d * tq + (tq - 1) < WINDOW
        mask = None if all_valid else (q_lane + d * tq - k_sub) < WINDOW

        @pl.when(qi >= d)
        def _(d=d, mask=mask):
            k0 = pl.multiple_of(q0 - d * tq, tk)
            flash_step(kw_ref[pl.ds(k0, tk), :], vwt_ref[:, pl.ds(k0, tk)], q_head, mask)
        d += 1

    for h in range(hpg):
        o = out_acc[h] + gate(2, h) * flash_out(h)
        o_ref[:, h * hd:(h + 1) * hd] = o.T.astype(o_ref.dtype)


def _cmp_to_sel_t(ncp, n_cmp, n_sb):
    c0 = np.arange(ncp)[None, :] * CMP_STRIDE
    s0 = np.arange(n_sb)[:, None] * SEL_BLOCK
    ov = np.minimum(c0 + CMP_BLOCK, s0 + SEL_BLOCK) - np.maximum(c0, s0)
    w = np.maximum(ov, 0) / CMP_BLOCK
    w = w * (np.arange(ncp)[None, :] < n_cmp)
    return jnp.asarray(w, dtype=BF16)


def _nsa_attention(qhat, k_cmp, v_cmp_t, k_tok, v_t, gates_t, *, B, S, D, tq=512):
    T = B * S
    hpg = N_HEADS // N_KV
    gw = hpg * HEAD_DIM
    tq = min(tq, S)
    nq = S // tq
    ncp = k_cmp.shape[1]
    n_cmp = (S - CMP_BLOCK) // CMP_STRIDE + 1
    n_sb = S // SEL_BLOCK
    assert n_sb % SUBLANES == 0 and n_sb <= LANES
    mt = _cmp_to_sel_t(ncp, n_cmp, n_sb)
    grow = gates_t.shape[0] // N_KV
    return pl.pallas_call(
        functools.partial(_nsa_kernel, tq=tq, hpg=hpg, rank_w=min(256, tq)),
        grid=(B, N_KV, nq),
        in_specs=[pl.BlockSpec((tq, gw), lambda b, g, i: (b * nq + i, g)),
                  pl.BlockSpec((None, ncp, HEAD_DIM), lambda b, g, i: (b * N_KV + g, 0, 0)),
                  pl.BlockSpec((None, HEAD_DIM, ncp), lambda b, g, i: (b * N_KV + g, 0, 0)),
                  pl.BlockSpec((S, HEAD_DIM), lambda b, g, i: (b, 2 * N_KV + g)),
                  pl.BlockSpec((HEAD_DIM, S), lambda b, g, i: (g, b)),
                  pl.BlockSpec((S, HEAD_DIM), lambda b, g, i: (b, 3 * N_KV + g)),
                  pl.BlockSpec((HEAD_DIM, S), lambda b, g, i: (N_KV + g, b)),
                  pl.BlockSpec((grow, tq), lambda b, g, i: (g, b * nq + i)),
                  pl.BlockSpec((n_sb, ncp), lambda b, g, i: (0, 0))],
        out_specs=pl.BlockSpec((tq, gw), lambda b, g, i: (b * nq + i, g)),
        out_shape=jax.ShapeDtypeStruct((T, D), BF16),
        scratch_shapes=[pltpu.VMEM((hpg, tq, 2 * HEAD_DIM), BF16),
                        pltpu.VMEM((hpg, HEAD_DIM, tq), F32),
                        pltpu.VMEM((hpg, HEAD_DIM, tq), F32),
                        pltpu.VMEM((hpg, 1, tq), F32),
                        pltpu.VMEM((hpg, 1, tq), F32),
                        pltpu.VMEM((hpg, tq, tq), F32)],
        compiler_params=_params("parallel", "parallel", "parallel"),
        name="nsa_attention",
    )(qhat, k_cmp, v_cmp_t, k_tok, v_t, k_tok, v_t, gates_t, mt)


def _mlp(x, g, w_up, w_down, layer):
    xn = _rmsnorm(x, g)
    h = _matmul(xn, w_up, layer=layer, epilogue="relu2", name="mlp_up")
    return _matmul(h, w_down, layer=layer, epilogue="resid", extra=x, out_dtype=F32, tm=2048, tn=1024, tk=2048,
                   name="mlp_down")


def kernel(x, attn_norm_g, mlp_norm_g, m_w_in, m_b_gate, m_head_g, m_w_out, kv_norm_g, w_kv, k_norm_g,
           cmp_pos, cmp_w1, cmp_w2, n_w_qg, q_norm_g, n_w_out, mlp_w_up, mlp_w_down):
    B, S, D = x.shape
    T = B * S
    xf = x.reshape(T, D)
    dv = D // M_HEADS
    dk = dv // 2
    qk = M_HEADS * dk
    nq = 2 * qk + 2 * D
    kv_dim = N_KV * HEAD_DIM
    hpg = N_HEADS // N_KV
    k_fold = (HEAD_DIM ** -0.5) * LOG2E

    xn = _rmsnorm(xf, attn_norm_g[0])
    proj = _matmul(xn, jnp.swapaxes(m_w_in, 1, 2), layer=0, nt=True, n_out=nq, epilogue="mlstm_in", tn=min(512, qk),
                   cfg=dict(m_qk=qk, k_scale=float(dk ** -0.5)), name="mlstm_in_proj")
    gcol, grow = _mlstm_gates(xn, m_w_in[0, :, nq:], m_b_gate[0])
    hg = _mlstm(proj, gcol, grow, B=B, S=S, D=D)
    xf = _matmul(hg, m_w_out, layer=0, epilogue="resid", extra=xf, gain=m_head_g[0], gain_on="w", out_dtype=F32,
                 name="mlstm_out_proj")
    xf = _mlp(xf, mlp_norm_g[0], mlp_w_up, mlp_w_down, 0)

    qg = q_norm_g[0]
    ones = jnp.ones((kv_dim,), F32)
    kv_gain = jnp.concatenate([
        ones, ones,
        jnp.tile(k_norm_g[1] * qg[1] * k_fold, N_KV),
        jnp.tile(k_norm_g[2] * qg[2] * k_fold, N_KV)]).reshape(1, 4 * kv_dim)
    xh = _rmsnorm(xf, jnp.ones((D,), F32))
    k_tok, a2 = _matmul(xh, w_kv, n_out=4 * kv_dim, col_map=lambda j: j + j // 3, epilogue="kvnorm", extra=kv_gain,
                        gain=kv_norm_g, gain_on="w", tn=kv_dim, cfg=dict(kv_dim=kv_dim, norm_slots=(2, 3), seq=S),
                        name="nsa_kv_proj")
    w_vt = jnp.concatenate([w_kv[:, 3 * kv_dim:4 * kv_dim], w_kv[:, 5 * kv_dim:6 * kv_dim]], axis=1).T
    v_t = _matmul(w_vt, xh, nt=True, gain=kv_norm_g, gain_on="a", tm=kv_dim, tn=1024,
                  name="nsa_vt_proj")
    ncp = S // CMP_STRIDE
    a2 = a2.reshape(2, B * N_KV, ncp, CMP_STRIDE * HEAD_DIM)
    pos2 = cmp_pos.reshape(2, 2, CMP_STRIDE * HEAD_DIM).astype(F32)
    cmp_gain = (k_norm_g[0] * qg[0] * k_fold).reshape(1, HEAD_DIM)
    k_cmp, v_cmp_t = _compress(a2, pos2, cmp_w1.astype(BF16), cmp_w2[0].astype(BF16), cmp_w2[1].T.astype(BF16),
                               cmp_gain)

    w_qg = n_w_qg[0]
    qhat = _matmul(xh, jnp.swapaxes(n_w_qg, 1, 2), layer=0, nt=True, n_out=D, epilogue="headnorm",
                   gain=attn_norm_g[1], gain_on="w", name="nsa_q_proj")
    grow_n = -(-N_BRANCH * hpg // SUBLANES) * SUBLANES
    w_gate = w_qg[:, D:].reshape(D, N_BRANCH, N_KV, hpg).transpose(2, 1, 3, 0).reshape(N_KV, N_BRANCH * hpg, D)
    w_gate = jnp.zeros((N_KV, grow_n, D), F32).at[:, :N_BRANCH * hpg].set(w_gate).reshape(N_KV * grow_n, D)
    gates_t = _matmul(w_gate, xh, nt=True, epilogue="sigmoid", gain=attn_norm_g[1], gain_on="a", out_dtype=F32,
                      tn=1024, name="nsa_gate_proj")
    o = _nsa_attention(qhat, k_cmp, v_cmp_t, k_tok, v_t, gates_t, B=B, S=S, D=D)
    xf = _matmul(o, n_w_out, layer=0, epilogue="resid", extra=xf, out_dtype=F32, name="nsa_out_proj")
    xf = _mlp(xf, mlp_norm_g[1], mlp_w_up, mlp_w_down, 1)
    return xf.reshape(B, S, D)
```

```python
import functools

import numpy as np
import jax
import jax.numpy as jnp
from jax import lax
from jax.experimental import pallas as pl
from jax.experimental.pallas import tpu as pltpu

F32 = jnp.float32
BF16 = jnp.bfloat16

EPS = 1e-6
NEG_BIG = -1e30
MASK_VAL = -(2.0 ** 100)
LOG2E = 1.4426950408889634

M_HEADS = 8
GATE_CAP = 15.0
M_CHUNK = 256

N_HEADS = 32
N_KV = 4
HEAD_DIM = 128
N_BRANCH = 3
CMP_BLOCK = 32
CMP_STRIDE = 16
SEL_BLOCK = 64
N_SEL = 16
WINDOW = 512

LANES = 128
SUBLANES = 8
VMEM_LIMIT = 56 * 1024 * 1024


def _params(*sem):
    return pltpu.CompilerParams(dimension_semantics=sem, vmem_limit_bytes=VMEM_LIMIT)


def _dot(a, b):
    return jnp.dot(a, b, preferred_element_type=F32)


def _dot_nt(a, b):
    return lax.dot_general(a, b, (((1,), (1,)), ((), ())), preferred_element_type=F32)


def _split3(x):
    x1 = x.astype(BF16)
    r1 = x - x1.astype(F32)
    x2 = r1.astype(BF16)
    r2 = r1 - x2.astype(F32)
    return x1, x2, r2.astype(BF16)


def _rmsnorm_kernel(x_ref, g_ref, o_ref):
    x = x_ref[...]
    ms = jnp.mean(x * x, axis=-1, keepdims=True)
    o_ref[...] = (x * lax.rsqrt(ms + EPS) * g_ref[...]).astype(o_ref.dtype)


def _rmsnorm(x, g, tm=512):
    T, D = x.shape
    tm = min(tm, T)
    return pl.pallas_call(
        _rmsnorm_kernel,
        grid=(T // tm,),
        in_specs=[pl.BlockSpec((tm, D), lambda i: (i, 0)),
                  pl.BlockSpec((1, D), lambda i: (0, 0))],
        out_specs=pl.BlockSpec((tm, D), lambda i: (i, 0)),
        out_shape=jax.ShapeDtypeStruct((T, D), BF16),
        compiler_params=_params("parallel"),
        name="rmsnorm",
    )(x, g.reshape(1, D).astype(F32))


def _group_rstd(y):
    out = []
    for g in range(y.shape[1] // HEAD_DIM):
        yg = y[:, g * HEAD_DIM:(g + 1) * HEAD_DIM]
        out.append(lax.rsqrt(jnp.mean(yg * yg, axis=-1, keepdims=True) + EPS))
    return out


def _mm_kernel(*refs, epilogue, nk, tn, nt, gain_on, cfg):
    refs = list(refs)
    a_ref, w_ref = refs[:2]
    n_in = 2 + (gain_on is not None) + (epilogue in ("resid", "kvnorm"))
    g_ref = refs[2] if gain_on is not None else None
    e_ref = refs[n_in - 1] if epilogue in ("resid", "kvnorm") else None
    o_ref = refs[n_in]
    j = pl.program_id(1)
    k = pl.program_id(2)

    def product():
        a = a_ref[...]
        w = w_ref[...]
        if gain_on == "a":
            a = a * g_ref[...]
        elif gain_on == "w" and nt:
            w = w * g_ref[...]
        elif gain_on == "w":
            g = g_ref[...]
            w = jnp.concatenate([w[:, c * LANES:(c + 1) * LANES] * g for c in range(w.shape[1] // LANES)], axis=1)
        a = a.astype(BF16)
        w = w.astype(BF16)
        return _dot_nt(a, w) if nt else _dot(a, w)

    if nk > 1:
        assert epilogue == "resid" and o_ref.dtype == F32 and gain_on is None and not nt

        @pl.when(k == 0)
        def _():
            o_ref[...] = e_ref[...]

        half = o_ref.shape[0] // 2
        w = w_ref[...].astype(BF16)
        for r in range(2):
            rows = slice(r * half, (r + 1) * half)
            o_ref[rows, :] += _dot(a_ref[rows, :], w)
        return

    part = product()

    def finish(y):
        if epilogue == "none":
            o_ref[...] = y.astype(o_ref.dtype)
        elif epilogue == "relu2":
            r = jnp.maximum(y, 0.0)
            o_ref[...] = (r * r).astype(o_ref.dtype)
        elif epilogue == "sigmoid":
            o_ref[...] = jax.nn.sigmoid(y).astype(o_ref.dtype)
        elif epilogue == "resid":
            o_ref[...] = (e_ref[...] + y).astype(o_ref.dtype)
        elif epilogue == "headnorm":
            rstd = _group_rstd(y)
            for g, r in enumerate(rstd):
                sl = slice(g * HEAD_DIM, (g + 1) * HEAD_DIM)
                o_ref[:, sl] = (y[:, sl] * r).astype(o_ref.dtype)
        elif epilogue == "kvnorm":
            slot = (j * tn) // cfg["kv_dim"]
            is_norm = functools.reduce(jnp.logical_or, [slot == s for s in cfg["norm_slots"]])
            rstd = _group_rstd(y)
            gain = e_ref[...]
            for g, r in enumerate(rstd):
                sl = slice(g * HEAD_DIM, (g + 1) * HEAD_DIM)
                mult = jnp.where(is_norm, r, 1.0)
                o_ref[:, sl] = (y[:, sl] * mult * gain[:, sl]).astype(o_ref.dtype)

            blk_ref, y_scr = refs[n_in + 1], refs[n_in + 2]
            rows = y.shape[0] // CMP_STRIDE

            @pl.when(slot < 2)
            def _():
                for g in range(y.shape[1] // HEAD_DIM):
                    y_scr[g] = y[:, g * HEAD_DIM:(g + 1) * HEAD_DIM]
                    for r in range(CMP_STRIDE):
                        piece = y_scr[g, pl.ds(r, rows, stride=CMP_STRIDE), :]
                        blk_ref[g, :, r * HEAD_DIM:(r + 1) * HEAD_DIM] = piece.astype(blk_ref.dtype)
        elif epilogue == "mlstm_in":
            col0 = j * tn
            qk = cfg["m_qk"]
            is_k = jnp.logical_and(col0 >= qk, col0 < 2 * qk)
            o_ref[...] = (y * jnp.where(is_k, cfg["k_scale"], 1.0)).astype(o_ref.dtype)
        else:
            raise ValueError(epilogue)

    finish(part)


def _matmul(a, w, *, layer=None, nt=False, n_out=None, col_map=None, epilogue="none", extra=None, gain=None,
            gain_on=None, out_dtype=BF16, tm=2048, tn=512, tk=4096, cfg=None, name="matmul"):
    M, K = a.shape
    wshape = w.shape[1:] if layer is not None else w.shape
    N = n_out if n_out is not None else (wshape[0] if nt else wshape[1])
    tm, tn, tk = min(tm, M), min(tn, N), min(tk, K)
    assert M % tm == 0 and N % tn == 0 and K % tk == 0
    nk = K // tk
    cm = col_map if col_map is not None else (lambda j: j)
    if layer is not None and nt:
        w_spec = pl.BlockSpec((None, tn, tk), lambda i, j, k: (layer, cm(j), k))
    elif layer is not None:
        w_spec = pl.BlockSpec((None, tk, tn), lambda i, j, k: (layer, k, cm(j)))
    elif nt:
        w_spec = pl.BlockSpec((tn, tk), lambda i, j, k: (cm(j), k))
    else:
        w_spec = pl.BlockSpec((tk, tn), lambda i, j, k: (k, cm(j)))
    a_mode = dict(pipeline_mode=pl.Buffered(1)) if (nk == 1 and N // tn > 1) else {}
    in_specs = [pl.BlockSpec((tm, tk), lambda i, j, k: (i, k), **a_mode), w_spec]
    args = [a, w]
    if gain_on is not None:
        if gain_on == "w" and not nt:
            in_specs.append(pl.BlockSpec((tk, LANES), lambda i, j, k: (k, 0)))
            args.append(jnp.broadcast_to(gain.astype(F32).reshape(K, 1), (K, LANES)))
        else:
            in_specs.append(pl.BlockSpec((1, tk), lambda i, j, k: (0, k)))
            args.append(gain.astype(F32).reshape(1, K))
    if epilogue == "resid":
        e_mode = dict(pipeline_mode=pl.Buffered(1)) if nk > 1 else {}
        in_specs.append(pl.BlockSpec((tm, tn), lambda i, j, k: (i, j), **e_mode))
        args.append(extra)
    elif epilogue == "kvnorm":
        in_specs.append(pl.BlockSpec((1, tn), lambda i, j, k: (0, j)))
        args.append(extra)
    o_mode = dict(pipeline_mode=pl.Buffered(1)) if nk > 1 else {}
    out_specs = pl.BlockSpec((tm, tn), lambda i, j, k: (i, j), **o_mode)
    out_shape = jax.ShapeDtypeStruct((M, N), out_dtype)
    scratch = []
    semantics = ("parallel", "parallel", "arbitrary")
    if epilogue == "kvnorm":
        seq, groups = cfg["seq"], tn // HEAD_DIM
        assert seq % tm == 0 and tm % CMP_STRIDE == 0 and tn == cfg["kv_dim"]
        per_seq = seq // tm
        out_specs = [out_specs, pl.BlockSpec((None, None, groups, tm // CMP_STRIDE, CMP_STRIDE * HEAD_DIM),
                                             lambda i, j, k: (jnp.minimum(j, 1), i // per_seq, 0, i % per_seq, 0))]
        out_shape = [out_shape, jax.ShapeDtypeStruct((2, M // seq, groups, seq // CMP_STRIDE, CMP_STRIDE * HEAD_DIM),
                                                     out_dtype)]
        scratch = [pltpu.VMEM((groups, tm, HEAD_DIM), F32)]
        semantics = ("parallel", "arbitrary", "arbitrary")
    return pl.pallas_call(
        functools.partial(_mm_kernel, epilogue=epilogue, nk=nk, tn=tn, nt=nt, gain_on=gain_on, cfg=cfg),
        grid=(M // tm, N // tn, nk),
        in_specs=in_specs,
        out_specs=out_specs,
        out_shape=out_shape,
        scratch_shapes=scratch,
        compiler_params=_params(*semantics),
        name=name,
    )(*args)


def _log_sigmoid(x):
    return jnp.minimum(x, 0.0) - jnp.log(1.0 + jnp.exp(-jnp.abs(x)))


def _gates_kernel(xn_ref, wg_ref, wgt_ref, brow_ref, bcol_ref, gcol_ref, grow_ref):
    xn = xn_ref[...]
    pre_c = _dot(xn, wg_ref[...]) + brow_ref[...]
    g_c = GATE_CAP * jnp.tanh(pre_c / GATE_CAP)
    col = lax.broadcasted_iota(jnp.int32, g_c.shape, 1)
    gcol_ref[...] = jnp.where(col < M_HEADS, g_c, _log_sigmoid(g_c))
    pre_r = _dot_nt(wgt_ref[...], xn) + bcol_ref[...]
    g_r = GATE_CAP * jnp.tanh(pre_r / GATE_CAP)
    row = lax.broadcasted_iota(jnp.int32, g_r.shape, 0)
    grow_ref[...] = jnp.where(row < M_HEADS, g_r, _log_sigmoid(g_r))


def _mlstm_gates(xn, w_g, b_g, tm=2048):
    T, D = xn.shape
    ng = 2 * M_HEADS
    tm = min(tm, T)
    wg = jnp.zeros((D, LANES), F32).at[:, :ng].set(w_g).astype(BF16)
    wgt = w_g.T.astype(BF16)
    brow = jnp.zeros((1, LANES), F32).at[0, :ng].set(b_g)
    bcol = b_g.reshape(ng, 1).astype(F32)
    return pl.pallas_call(
        _gates_kernel,
        grid=(T // tm,),
        in_specs=[pl.BlockSpec((tm, D), lambda i: (i, 0)),
                  pl.BlockSpec((D, LANES), lambda i: (0, 0)),
                  pl.BlockSpec((ng, D), lambda i: (0, 0)),
                  pl.BlockSpec((1, LANES), lambda i: (0, 0)),
                  pl.BlockSpec((ng, 1), lambda i: (0, 0))],
        out_specs=[pl.BlockSpec((tm, LANES), lambda i: (i, 0)),
                   pl.BlockSpec((ng, tm), lambda i: (0, i))],
        out_shape=[jax.ShapeDtypeStruct((T, LANES), F32),
                   jax.ShapeDtypeStruct((ng, T), F32)],
        compiler_params=_params("parallel"),
        name="mlstm_gates",
    )(xn, wg, wgt, brow, bcol)


def _mlstm_kernel(q_ref, k_ref, v_ref, o_ref, gc_ref, gr_ref, out_ref, c_ref, m_ref, *, dk, dv):
    L = q_ref.shape[0]

    @pl.when(pl.program_id(1) == 0)
    def _():
        c_ref[...] = jnp.zeros_like(c_ref)
        m_ref[...] = jnp.zeros_like(m_ref)

    row = lax.broadcasted_iota(jnp.int32, (L, L), 0)
    col = lax.broadcasted_iota(jnp.int32, (L, L), 1)
    causal = col <= row
    tril = jnp.where(causal, 1.0, 0.0).astype(BF16)
    triu = jnp.where(row <= col, 1.0, 0.0).astype(BF16)

    gc = gc_ref[...]
    gr = gr_ref[...]
    c1, c2, c3 = _split3(gc)
    b_col_all = _dot(tril, c1) + _dot(tril, c2) + _dot(tril, c3)
    r1, r2, r3 = _split3(gr)
    b_row_all = _dot(r1, triu) + _dot(r2, triu) + _dot(r3, triu)

    ones_blk = jnp.where(lax.broadcasted_iota(jnp.int32, (L, LANES), 1) == 0, 1.0, 0.0).astype(BF16)

    for h in range(M_HEADS):
        li_row = gr[h:h + 1, :]
        li_col = gc[:, h:h + 1]
        b_row = b_row_all[M_HEADS + h:M_HEADS + h + 1, :]
        b_col = b_col_all[:, M_HEADS + h:M_HEADS + h + 1]
        m_prev = m_ref[h]

        log_d = jnp.where(causal, b_col + (li_row - b_row), -jnp.inf)
        log_inter = b_col + m_prev
        m_t = jnp.maximum(log_inter, jnp.max(log_d, axis=1, keepdims=True))
        w_intra = jnp.exp(log_d - m_t)
        w_inter = jnp.exp(log_inter - m_t)

        qh = q_ref[:, h * dk:(h + 1) * dk]
        kh = k_ref[:, h * dk:(h + 1) * dk]
        v_aug = jnp.concatenate([v_ref[:, h * dv:(h + 1) * dv], ones_blk], axis=1)
        c_old = c_ref[h]

        s = (_dot_nt(qh, kh) * w_intra).astype(BF16)
        qs = (qh.astype(F32) * w_inter).astype(BF16)
        numden = _dot(qs, c_old.astype(BF16)) + _dot(s, v_aug)
        num = numden[:, 0:dv]
        den = numden[:, dv:dv + 1]
        inv = 1.0 / jnp.maximum(jnp.abs(den), jnp.exp(-m_t))
        ssq = jnp.sum(num * num, axis=1, keepdims=True)
        scale = inv * lax.rsqrt(inv * inv * ssq * (1.0 / dv) + EPS)
        gate_o = jax.nn.sigmoid(o_ref[:, h * dv:(h + 1) * dv].astype(F32))
        out_ref[:, h * dv:(h + 1) * dv] = (gate_o * (num * scale)).astype(out_ref.dtype)

        b_last = b_col[L - 1:L, :]
        log_w = b_last - b_col + li_col
        m_new = jnp.maximum(b_last + m_prev, jnp.max(log_w, axis=0, keepdims=True))
        wk = jnp.exp(log_w - m_new) * kh.astype(F32)
        decay = jnp.exp(b_last + m_prev - m_new)
        c_ref[h] = decay * c_old + _dot(wk.T.astype(BF16), v_aug)
        m_ref[h] = m_new


def _mlstm(proj, gcol, grow, *, B, S, D):
    T = B * S
    dv = D // M_HEADS
    dk = dv // 2
    qk = M_HEADS * dk
    L = min(M_CHUNK, S)
    nc = S // L
    rb = lambda b, c: b * nc + c
    assert D % qk == 0
    return pl.pallas_call(
        functools.partial(_mlstm_kernel, dk=dk, dv=dv),
        grid=(B, nc),
        in_specs=[pl.BlockSpec((L, qk), lambda b, c: (rb(b, c), 0)),
                  pl.BlockSpec((L, qk), lambda b, c: (rb(b, c), 1)),
                  pl.BlockSpec((L, D), lambda b, c: (rb(b, c), (2 * qk) // D)),
                  pl.BlockSpec((L, D), lambda b, c: (rb(b, c), (2 * qk) // D + 1)),
                  pl.BlockSpec((L, LANES), lambda b, c: (rb(b, c), 0)),
                  pl.BlockSpec((2 * M_HEADS, L), lambda b, c: (0, rb(b, c)))],
        out_specs=pl.BlockSpec((L, D), lambda b, c: (rb(b, c), 0)),
        out_shape=jax.ShapeDtypeStruct((T, D), BF16),
        scratch_shapes=[pltpu.VMEM((M_HEADS, dk, dv + LANES), F32),
                        pltpu.VMEM((M_HEADS, 1, 1), F32)],
        compiler_params=_params("parallel", "arbitrary"),
        name="mlstm",
    )(proj, proj, proj, proj, gcol, grow)


def _compress_kernel(a_ref, pos_ref, w1_ref, w2k_ref, w2vt_ref, gain_ref, ok_ref, ovt_ref):
    half = a_ref.shape[2]
    ncp = a_ref.shape[1]

    def hidden(j):
        a = a_ref[j].astype(F32)
        lo = (a + pos_ref[j, 0:1, :]).astype(BF16)
        hi = (a + pos_ref[j, 1:2, :]).astype(BF16)
        u = _dot(lo, w1_ref[j, 0:half, :])
        v = _dot(hi, w1_ref[j, half:2 * half, :])
        hid = u + pltpu.roll(v, shift=ncp - 1, axis=0)
        c0 = float(np.sqrt(2.0 / np.pi))
        cdf = 0.5 * (1.0 + jnp.tanh(c0 * (hid + 0.044715 * (hid * hid * hid))))
        return (hid * cdf).astype(BF16)

    yk = _dot(hidden(0), w2k_ref[...])
    rstd = lax.rsqrt(jnp.mean(yk * yk, axis=-1, keepdims=True) + EPS)
    ok_ref[...] = (yk * rstd * gain_ref[...]).astype(ok_ref.dtype)
    ovt_ref[...] = _dot_nt(w2vt_ref[...], hidden(1)).astype(ovt_ref.dtype)


def _compress(a2, pos2, w1, w2k, w2vt, gain):
    _, n, ncp, half = a2.shape
    hid = w1.shape[2]
    return pl.pallas_call(
        _compress_kernel,
        grid=(n,),
        in_specs=[pl.BlockSpec((2, None, ncp, half), lambda i: (0, i, 0, 0)),
                  pl.BlockSpec((2, 2, half), lambda i: (0, 0, 0)),
                  pl.BlockSpec((2, 2 * half, hid), lambda i: (0, 0, 0)),
                  pl.BlockSpec((hid, HEAD_DIM), lambda i: (0, 0)),
                  pl.BlockSpec((HEAD_DIM, hid), lambda i: (0, 0)),
                  pl.BlockSpec((1, HEAD_DIM), lambda i: (0, 0))],
        out_specs=[pl.BlockSpec((None, ncp, HEAD_DIM), lambda i: (i, 0, 0)),
                   pl.BlockSpec((None, HEAD_DIM, ncp), lambda i: (i, 0, 0))],
        out_shape=[jax.ShapeDtypeStruct((n, ncp, HEAD_DIM), BF16),
                   jax.ShapeDtypeStruct((n, HEAD_DIM, ncp), BF16)],
        compiler_params=_params("parallel"),
        name="nsa_compress",
    )(a2, pos2, w1, w2k, w2vt, gain)


def _nsa_kernel(q_ref, kc_ref, vct_ref, ks_ref, vst_ref, kw_ref, vwt_ref, gate_ref, mt_ref, o_ref,
                qa_ref, out_acc, acc_ref, m_ref, l_ref, s_ref, *, tq, hpg, rank_w):
    qi = pl.program_id(2)
    q0 = pl.multiple_of(qi * tq, tq)
    hd = HEAD_DIM
    t_lane = q0 + lax.broadcasted_iota(jnp.int32, (1, tq), 1)

    def gate(br, h):
        r = br * hpg + h
        return gate_ref[r:r + 1, :]

    def q_head(h):
        return q_ref[:, h * hd:(h + 1) * hd]

    ncp = kc_ref.shape[0]
    c_sub = lax.broadcasted_iota(jnp.int32, (ncp, 1), 0)
    cmp_valid = (c_sub * CMP_STRIDE + (CMP_BLOCK - 1)) <= t_lane
    kc = kc_ref[...]
    vct = vct_ref[...]
    psum = jnp.zeros((ncp, tq), F32)
    for h in range(hpg):
        s = jnp.where(cmp_valid, _dot_nt(kc, q_head(h)), NEG_BIG)
        e = jnp.exp2(s - jnp.max(s, axis=0, keepdims=True))
        p = jnp.where(cmp_valid, e * (1.0 / jnp.sum(e, axis=0, keepdims=True)), 0.0)
        psum = psum + p
        out_acc[h] = gate(0, h) * _dot(vct, p.astype(BF16))

    mt = mt_ref[...]
    n_sb = mt.shape[0]
    p1, p2, p3 = _split3(psum)
    p_blk = _dot(mt, p1) + _dot(mt, p2) + _dot(mt, p3)
    blk = lax.broadcasted_iota(jnp.int32, (n_sb, 1), 0)
    cur = t_lane // SEL_BLOCK
    forced = jnp.logical_or(blk == cur, blk == 0)
    val = jnp.where(forced, jnp.inf, jnp.where(blk <= cur, p_blk, -jnp.inf))

    sub = lax.broadcasted_iota(jnp.int32, (SUBLANES, 1), 0)
    n_rb = n_sb // SUBLANES
    top = float(min(N_SEL, n_sb))
    bias_cols = []
    for c in range(tq // rank_w):
        v = val[:, c * rank_w:(c + 1) * rank_w]
        rows = [v[rb * SUBLANES:(rb + 1) * SUBLANES, :] for rb in range(n_rb)]
        cnt = [jnp.zeros((SUBLANES, rank_w), F32) for _ in range(n_rb)]
        for i in range(n_sb):
            vi = v[i:i + 1, :]
            for rb in range(n_rb):
                lo = rb * SUBLANES
                ge = lambda: jnp.where(vi >= rows[rb], 1.0, 0.0)
                gt = lambda: jnp.where(vi > rows[rb], 1.0, 0.0)
                if lo > i:
                    beats = ge()
                elif lo + SUBLANES - 1 <= i:
                    beats = gt()
                else:
                    beats = jnp.where(sub > (i - lo), ge(), gt())
                cnt[rb] = cnt[rb] + beats
        cnt = jnp.concatenate(cnt, axis=0)
        cur_c = cur[:, c * rank_w:(c + 1) * rank_w]
        picked = jnp.where(cnt < top, jnp.where(blk <= cur_c, 0.0, MASK_VAL), MASK_VAL)
        bias_cols.append(picked)
    sel_bias = jnp.concatenate(bias_cols, axis=1)
    sel_bias = jnp.concatenate([sel_bias, jnp.zeros((LANES - n_sb, tq), F32)], axis=0)
    sb_t = sel_bias.T.astype(BF16)
    for h in range(hpg):
        qa_ref[h, :, 0:hd] = q_head(h)
        qa_ref[h, :, hd:2 * hd] = sb_t

    def flash_reset():
        m_ref[...] = jnp.full(m_ref.shape, NEG_BIG, F32)
        l_ref[...] = jnp.zeros_like(l_ref)
        acc_ref[...] = jnp.zeros_like(acc_ref)

    def flash_step(k_t, vt_t, qsel, mask):
        tile_max = []
        for h in range(hpg):
            s = _dot_nt(k_t, qsel(h))
            if mask is not None:
                s = jnp.where(mask, s, MASK_VAL)
            s_ref[h] = s
            tile_max.append(jnp.max(s, axis=0, keepdims=True))
        for h in range(hpg):
            m_old = m_ref[h]
            m_new = jnp.maximum(m_old, tile_max[h])
            alpha = jnp.exp2(m_old - m_new)
            p = jnp.exp2(s_ref[h] - m_new)
            l_ref[h] = alpha * l_ref[h] + jnp.sum(p, axis=0, keepdims=True)
            acc_ref[h] = alpha * acc_ref[h] + _dot(vt_t, p.astype(BF16))
            m_ref[h] = m_new

    def flash_out(h):
        return acc_ref[h] * (1.0 / l_ref[h])

    tk = tq
    k_sub = lax.broadcasted_iota(jnp.int32, (tk, 1), 0)
    q_lane = lax.broadcasted_iota(jnp.int32, (1, tq), 1)
    causal = k_sub <= q_lane
    lane_j = lax.broadcasted_iota(jnp.int32, (1, LANES), 1)
    key_blk = k_sub // SEL_BLOCK

    def ks_aug(k0):
        onehot = jnp.where(lane_j == key_blk + k0 // SEL_BLOCK, 1.0, 0.0).astype(BF16)
        return jnp.concatenate([ks_ref[pl.ds(k0, tk), :], onehot], axis=1)

    qa_head = lambda h: qa_ref[h]

    flash_reset()

    def sel_body(kt, carry):
        k0 = pl.multiple_of(kt * tk, tk)
        flash_step(ks_aug(k0), vst_ref[:, pl.ds(k0, tk)], qa_head, None)
        return carry

    lax.fori_loop(0, qi, sel_body, 0)
    flash_step(ks_aug(q0), vst_ref[:, pl.ds(q0, tk)], qa_head, causal)
    for h in range(hpg):
        out_acc[h] = out_acc[h] + gate(1, h) * flash_out(h)

    flash_reset()
    flash_step(kw_ref[pl.ds(q0, tk), :], vwt_ref[:, pl.ds(q0, tk)], q_head, causal)
    d = 1
    while d * tq - (tk - 1) < WINDOW:
        all_valid = d * tq + (tq - 1) < WINDOW
        mask = None if all_valid else (q_lane + d * tq - k_sub) < WINDOW

        @pl.when(qi >= d)
        def _(d=d, mask=mask):
            k0 = pl.multiple_of(q0 - d * tq, tk)
            flash_step(kw_ref[pl.ds(k0, tk), :], vwt_ref[:, pl.ds(k0, tk)], q_head, mask)
        d += 1

    for h in range(hpg):
        o = out_acc[h] + gate(2, h) * flash_out(h)
        o_ref[:, h * hd:(h + 1) * hd] = o.T.astype(o_ref.dtype)


def _cmp_to_sel_t(ncp, n_cmp, n_sb):
    c0 = np.arange(ncp)[None, :] * CMP_STRIDE
    s0 = np.arange(n_sb)[:, None] * SEL_BLOCK
    ov = np.minimum(c0 + CMP_BLOCK, s0 + SEL_BLOCK) - np.maximum(c0, s0)
    w = np.maximum(ov, 0) / CMP_BLOCK
    w = w * (np.arange(ncp)[None, :] < n_cmp)
    return jnp.asarray(w, dtype=BF16)


def _nsa_attention(qhat, k_cmp, v_cmp_t, k_tok, v_t, gates_t, *, B, S, D, tq=512):
    T = B * S
    hpg = N_HEADS // N_KV
    gw = hpg * HEAD_DIM
    tq = min(tq, S)
    nq = S // tq
    ncp = k_cmp.shape[1]
    n_cmp = (S - CMP_BLOCK) // CMP_STRIDE + 1
    n_sb = S // SEL_BLOCK
    assert n_sb % SUBLANES == 0 and n_sb <= LANES
    mt = _cmp_to_sel_t(ncp, n_cmp, n_sb)
    grow = gates_t.shape[0] // N_KV
    return pl.pallas_call(
        functools.partial(_nsa_kernel, tq=tq, hpg=hpg, rank_w=min(256, tq)),
        grid=(B, N_KV, nq),
        in_specs=[pl.BlockSpec((tq, gw), lambda b, g, i: (b * nq + i, g)),
                  pl.BlockSpec((None, ncp, HEAD_DIM), lambda b, g, i: (b * N_KV + g, 0, 0)),
                  pl.BlockSpec((None, HEAD_DIM, ncp), lambda b, g, i: (b * N_KV + g, 0, 0)),
                  pl.BlockSpec((S, HEAD_DIM), lambda b, g, i: (b, 2 * N_KV + g)),
                  pl.BlockSpec((HEAD_DIM, S), lambda b, g, i: (g, b)),
                  pl.BlockSpec((S, HEAD_DIM), lambda b, g, i: (b, 3 * N_KV + g)),
                  pl.BlockSpec((HEAD_DIM, S), lambda b, g, i: (N_KV + g, b)),
                  pl.BlockSpec((grow, tq), lambda b, g, i: (g, b * nq + i)),
                  pl.BlockSpec((n_sb, ncp), lambda b, g, i: (0, 0))],
        out_specs=pl.BlockSpec((tq, gw), lambda b, g, i: (b * nq + i, g)),
        out_shape=jax.ShapeDtypeStruct((T, D), BF16),
        scratch_shapes=[pltpu.VMEM((hpg, tq, 2 * HEAD_DIM), BF16),
                        pltpu.VMEM((hpg, HEAD_DIM, tq), F32),
                        pltpu.VMEM((hpg, HEAD_DIM, tq), F32),
                        pltpu.VMEM((hpg, 1, tq), F32),
                        pltpu.VMEM((hpg, 1, tq), F32),
                        pltpu.VMEM((hpg, tq, tq), F32)],
        compiler_params=_params("parallel", "parallel", "parallel"),
        name="nsa_attention",
    )(qhat, k_cmp, v_cmp_t, k_tok, v_t, k_tok, v_t, gates_t, mt)


def _mlp(x, g, w_up, w_down, layer):
    xn = _rmsnorm(x, g)
    h = _matmul(xn, w_up, layer=layer, epilogue="relu2", name="mlp_up")
    return _matmul(h, w_down, layer=layer, epilogue="resid", extra=x, out_dtype=F32, tm=2048, tn=1024, tk=2048,
                   name="mlp_down")


def kernel(x, attn_norm_g, mlp_norm_g, m_w_in, m_b_gate, m_head_g, m_w_out, kv_norm_g, w_kv, k_norm_g,
           cmp_pos, cmp_w1, cmp_w2, n_w_qg, q_norm_g, n_w_out, mlp_w_up, mlp_w_down):
    B, S, D = x.shape
    T = B * S
    xf = x.reshape(T, D)
    dv = D // M_HEADS
    dk = dv // 2
    qk = M_HEADS * dk
    nq = 2 * qk + 2 * D
    kv_dim = N_KV * HEAD_DIM
    hpg = N_HEADS // N_KV
    k_fold = (HEAD_DIM ** -0.5) * LOG2E

    xn = _rmsnorm(xf, attn_norm_g[0])
    proj = _matmul(xn, jnp.swapaxes(m_w_in, 1, 2), layer=0, nt=True, n_out=nq, epilogue="mlstm_in", tn=min(512, qk),
                   cfg=dict(m_qk=qk, k_scale=float(dk ** -0.5)), name="mlstm_in_proj")
    gcol, grow = _mlstm_gates(xn, m_w_in[0, :, nq:], m_b_gate[0])
    hg = _mlstm(proj, gcol, grow, B=B, S=S, D=D)
    xf = _matmul(hg, m_w_out, layer=0, epilogue="resid", extra=xf, gain=m_head_g[0], gain_on="w", out_dtype=F32,
                 name="mlstm_out_proj")
    xf = _mlp(xf, mlp_norm_g[0], mlp_w_up, mlp_w_down, 0)

    qg = q_norm_g[0]
    ones = jnp.ones((kv_dim,), F32)
    kv_gain = jnp.concatenate([
        ones, ones,
        jnp.tile(k_norm_g[1] * qg[1] * k_fold, N_KV),
        jnp.tile(k_norm_g[2] * qg[2] * k_fold, N_KV)]).reshape(1, 4 * kv_dim)
    xh = _rmsnorm(xf, jnp.ones((D,), F32))
    k_tok, a2 = _matmul(xh, w_kv, n_out=4 * kv_dim, col_map=lambda j: j + j // 3, epilogue="kvnorm", extra=kv_gain,
                        gain=kv_norm_g, gain_on="w", tn=kv_dim, cfg=dict(kv_dim=kv_dim, norm_slots=(2, 3), seq=S),
                        name="nsa_kv_proj")
    w_vt = jnp.concatenate([w_kv[:, 3 * kv_dim:4 * kv_dim], w_kv[:, 5 * kv_dim:6 * kv_dim]], axis=1).T
    v_t = _matmul(w_vt, xh, nt=True, gain=kv_norm_g, gain_on="a", tm=kv_dim, tn=2048,
                  name="nsa_vt_proj")
    ncp = S // CMP_STRIDE
    a2 = a2.reshape(2, B * N_KV, ncp, CMP_STRIDE * HEAD_DIM)
    pos2 = cmp_pos.reshape(2, 2, CMP_STRIDE * HEAD_DIM).astype(F32)
    cmp_gain = (k_norm_g[0] * qg[0] * k_fold).reshape(1, HEAD_DIM)
    k_cmp, v_cmp_t = _compress(a2, pos2, cmp_w1.astype(BF16), cmp_w2[0].astype(BF16), cmp_w2[1].T.astype(BF16),
                               cmp_gain)

    w_qg = n_w_qg[0]
    qhat = _matmul(xh, jnp.swapaxes(n_w_qg, 1, 2), layer=0, nt=True, n_out=D, epilogue="headnorm",
                   gain=attn_norm_g[1], gain_on="w", name="nsa_q_proj")
    grow_n = -(-N_BRANCH * hpg // SUBLANES) * SUBLANES
    w_gate = w_qg[:, D:].reshape(D, N_BRANCH, N_KV, hpg).transpose(2, 1, 3, 0).reshape(N_KV, N_BRANCH * hpg, D)
    w_gate = jnp.zeros((N_KV, grow_n, D), F32).at[:, :N_BRANCH * hpg].set(w_gate).reshape(N_KV * grow_n, D)
    gates_t = _matmul(w_gate, xh, nt=True, epilogue="sigmoid", gain=attn_norm_g[1], gain_on="a", out_dtype=F32,
                      tn=2048, name="nsa_gate_proj")
    o = _nsa_attention(qhat, k_cmp, v_cmp_t, k_tok, v_t, gates_t, B=B, S=S, D=D)
    xf = _matmul(o, n_w_out, layer=0, epilogue="resid", extra=xf, out_dtype=F32, name="nsa_out_proj")
    xf = _mlp(xf, mlp_norm_g[1], mlp_w_up, mlp_w_down, 1)
    return xf.reshape(B, S, D)
```

```python
import functools

import numpy as np
import jax
import jax.numpy as jnp
from jax import lax
from jax.experimental import pallas as pl
from jax.experimental.pallas import tpu as pltpu

F32 = jnp.float32
BF16 = jnp.bfloat16

EPS = 1e-6
NEG_BIG = -1e30
MASK_VAL = -(2.0 ** 100)
LOG2E = 1.4426950408889634

M_HEADS = 8
GATE_CAP = 15.0
M_CHUNK = 256

N_HEADS = 32
N_KV = 4
HEAD_DIM = 128
N_BRANCH = 3
CMP_BLOCK = 32
CMP_STRIDE = 16
SEL_BLOCK = 64
N_SEL = 16
WINDOW = 512

LANES = 128
SUBLANES = 8
VMEM_LIMIT = 56 * 1024 * 1024


def _params(*sem):
    return pltpu.CompilerParams(dimension_semantics=sem, vmem_limit_bytes=VMEM_LIMIT)


def _dot(a, b):
    return jnp.dot(a, b, preferred_element_type=F32)


def _dot_nt(a, b):
    return lax.dot_general(a, b, (((1,), (1,)), ((), ())), preferred_element_type=F32)


def _split3(x):
    x1 = x.astype(BF16)
    r1 = x - x1.astype(F32)
    x2 = r1.astype(BF16)
    r2 = r1 - x2.astype(F32)
    return x1, x2, r2.astype(BF16)


def _rmsnorm_kernel(x_ref, g_ref, o_ref):
    x = x_ref[...]
    ms = jnp.mean(x * x, axis=-1, keepdims=True)
    o_ref[...] = (x * lax.rsqrt(ms + EPS) * g_ref[...]).astype(o_ref.dtype)


def _rmsnorm(x, g, tm=256):
    T, D = x.shape
    tm = min(tm, T)
    return pl.pallas_call(
        _rmsnorm_kernel,
        grid=(T // tm,),
        in_specs=[pl.BlockSpec((tm, D), lambda i: (i, 0)),
                  pl.BlockSpec((1, D), lambda i: (0, 0))],
        out_specs=pl.BlockSpec((tm, D), lambda i: (i, 0)),
        out_shape=jax.ShapeDtypeStruct((T, D), BF16),
        compiler_params=_params("parallel"),
        name="rmsnorm",
    )(x, g.reshape(1, D).astype(F32))


def _group_rstd(y):
    out = []
    for g in range(y.shape[1] // HEAD_DIM):
        yg = y[:, g * HEAD_DIM:(g + 1) * HEAD_DIM]
        out.append(lax.rsqrt(jnp.mean(yg * yg, axis=-1, keepdims=True) + EPS))
    return out


def _mm_kernel(*refs, epilogue, nk, tn, nt, gain_on, cfg):
    refs = list(refs)
    a_ref, w_ref = refs[:2]
    n_in = 2 + (gain_on is not None) + (epilogue in ("resid", "kvnorm"))
    g_ref = refs[2] if gain_on is not None else None
    e_ref = refs[n_in - 1] if epilogue in ("resid", "kvnorm") else None
    o_ref = refs[n_in]
    j = pl.program_id(1)
    k = pl.program_id(2)

    def product():
        a = a_ref[...]
        w = w_ref[...]
        if gain_on == "a":
            a = a * g_ref[...]
        elif gain_on == "w" and nt:
            w = w * g_ref[...]
        elif gain_on == "w":
            g = g_ref[...]
            w = jnp.concatenate([w[:, c * LANES:(c + 1) * LANES] * g for c in range(w.shape[1] // LANES)], axis=1)
        a = a.astype(BF16)
        w = w.astype(BF16)
        return _dot_nt(a, w) if nt else _dot(a, w)

    if nk > 1:
        assert epilogue == "resid" and o_ref.dtype == F32 and gain_on is None and not nt

        half = o_ref.shape[0] // 2

        @pl.when(k == 0)
        def _():
            o_ref[0:half, :] = e_ref[...]
            o_ref[half:, :] = jnp.zeros((half, o_ref.shape[1]), F32)

        @pl.when(k == nk // 2)
        def _():
            o_ref[half:, :] += e_ref[...]

        w = w_ref[...].astype(BF16)
        for r in range(2):
            rows = slice(r * half, (r + 1) * half)
            o_ref[rows, :] += _dot(a_ref[rows, :], w)
        return

    part = product()

    def finish(y):
        if epilogue == "none":
            o_ref[...] = y.astype(o_ref.dtype)
        elif epilogue == "relu2":
            r = jnp.maximum(y, 0.0)
            o_ref[...] = (r * r).astype(o_ref.dtype)
        elif epilogue == "sigmoid":
            o_ref[...] = jax.nn.sigmoid(y).astype(o_ref.dtype)
        elif epilogue == "resid":
            o_ref[...] = (e_ref[...] + y).astype(o_ref.dtype)
        elif epilogue == "headnorm":
            rstd = _group_rstd(y)
            for g, r in enumerate(rstd):
                sl = slice(g * HEAD_DIM, (g + 1) * HEAD_DIM)
                o_ref[:, sl] = (y[:, sl] * r).astype(o_ref.dtype)
        elif epilogue == "kvnorm":
            slot = (j * tn) // cfg["kv_dim"]
            is_norm = functools.reduce(jnp.logical_or, [slot == s for s in cfg["norm_slots"]])
            rstd = _group_rstd(y)
            gain = e_ref[...]
            for g, r in enumerate(rstd):
                sl = slice(g * HEAD_DIM, (g + 1) * HEAD_DIM)
                mult = jnp.where(is_norm, r, 1.0)
                o_ref[:, sl] = (y[:, sl] * mult * gain[:, sl]).astype(o_ref.dtype)

            blk_ref, y_scr = refs[n_in + 1], refs[n_in + 2]
            rows = y.shape[0] // CMP_STRIDE

            @pl.when(slot < 2)
            def _():
                for g in range(y.shape[1] // HEAD_DIM):
                    y_scr[g] = y[:, g * HEAD_DIM:(g + 1) * HEAD_DIM]
                    for r in range(CMP_STRIDE):
                        piece = y_scr[g, pl.ds(r, rows, stride=CMP_STRIDE), :]
                        blk_ref[g, :, r * HEAD_DIM:(r + 1) * HEAD_DIM] = piece.astype(blk_ref.dtype)
        elif epilogue == "mlstm_in":
            col0 = j * tn
            qk = cfg["m_qk"]
            is_k = jnp.logical_and(col0 >= qk, col0 < 2 * qk)
            o_ref[...] = (y * jnp.where(is_k, cfg["k_scale"], 1.0)).astype(o_ref.dtype)
        else:
            raise ValueError(epilogue)

    finish(part)


def _matmul(a, w, *, layer=None, nt=False, n_out=None, col_map=None, epilogue="none", extra=None, gain=None,
            gain_on=None, out_dtype=BF16, tm=2048, tn=512, tk=4096, cfg=None, name="matmul"):
    M, K = a.shape
    wshape = w.shape[1:] if layer is not None else w.shape
    N = n_out if n_out is not None else (wshape[0] if nt else wshape[1])
    tm, tn, tk = min(tm, M), min(tn, N), min(tk, K)
    assert M % tm == 0 and N % tn == 0 and K % tk == 0
    nk = K // tk
    cm = col_map if col_map is not None else (lambda j: j)
    if layer is not None and nt:
        w_spec = pl.BlockSpec((None, tn, tk), lambda i, j, k: (layer, cm(j), k))
    elif layer is not None:
        w_spec = pl.BlockSpec((None, tk, tn), lambda i, j, k: (layer, k, cm(j)))
    elif nt:
        w_spec = pl.BlockSpec((tn, tk), lambda i, j, k: (cm(j), k))
    else:
        w_spec = pl.BlockSpec((tk, tn), lambda i, j, k: (k, cm(j)))
    a_mode = dict(pipeline_mode=pl.Buffered(1)) if (nk == 1 and N // tn > 1) else {}
    in_specs = [pl.BlockSpec((tm, tk), lambda i, j, k: (i, k), **a_mode), w_spec]
    args = [a, w]
    if gain_on is not None:
        if gain_on == "w" and not nt:
            in_specs.append(pl.BlockSpec((tk, LANES), lambda i, j, k: (k, 0)))
            args.append(jnp.broadcast_to(gain.astype(F32).reshape(K, 1), (K, LANES)))
        else:
            in_specs.append(pl.BlockSpec((1, tk), lambda i, j, k: (0, k)))
            args.append(gain.astype(F32).reshape(1, K))
    if epilogue == "resid":
        if nk > 1:
            assert nk % 2 == 0 and tm % (2 * SUBLANES) == 0
            in_specs.append(pl.BlockSpec((tm // 2, tn), lambda i, j, k: (2 * i + k // (nk // 2), j)))
        else:
            in_specs.append(pl.BlockSpec((tm, tn), lambda i, j, k: (i, j)))
        args.append(extra)
    elif epilogue == "kvnorm":
        in_specs.append(pl.BlockSpec((1, tn), lambda i, j, k: (0, j)))
        args.append(extra)
    o_mode = dict(pipeline_mode=pl.Buffered(1)) if nk > 1 else {}
    out_specs = pl.BlockSpec((tm, tn), lambda i, j, k: (i, j), **o_mode)
    out_shape = jax.ShapeDtypeStruct((M, N), out_dtype)
    scratch = []
    semantics = ("parallel", "parallel", "arbitrary")
    if epilogue == "kvnorm":
        seq, groups = cfg["seq"], tn // HEAD_DIM
        assert seq % tm == 0 and tm % CMP_STRIDE == 0 and tn == cfg["kv_dim"]
        per_seq = seq // tm
        out_specs = [out_specs, pl.BlockSpec((None, None, groups, tm // CMP_STRIDE, CMP_STRIDE * HEAD_DIM),
                                             lambda i, j, k: (jnp.minimum(j, 1), i // per_seq, 0, i % per_seq, 0))]
        out_shape = [out_shape, jax.ShapeDtypeStruct((2, M // seq, groups, seq // CMP_STRIDE, CMP_STRIDE * HEAD_DIM),
                                                     out_dtype)]
        scratch = [pltpu.VMEM((groups, tm, HEAD_DIM), F32)]
        semantics = ("parallel", "arbitrary", "arbitrary")
    return pl.pallas_call(
        functools.partial(_mm_kernel, epilogue=epilogue, nk=nk, tn=tn, nt=nt, gain_on=gain_on, cfg=cfg),
        grid=(M // tm, N // tn, nk),
        in_specs=in_specs,
        out_specs=out_specs,
        out_shape=out_shape,
        scratch_shapes=scratch,
        compiler_params=_params(*semantics),
        name=name,
    )(*args)


def _log_sigmoid(x):
    return jnp.minimum(x, 0.0) - jnp.log(1.0 + jnp.exp(-jnp.abs(x)))


def _gates_kernel(xn_ref, wg_ref, wgt_ref, brow_ref, bcol_ref, gcol_ref, grow_ref):
    xn = xn_ref[...]
    pre_c = _dot(xn, wg_ref[...]) + brow_ref[...]
    g_c = GATE_CAP * jnp.tanh(pre_c / GATE_CAP)
    col = lax.broadcasted_iota(jnp.int32, g_c.shape, 1)
    gcol_ref[...] = jnp.where(col < M_HEADS, g_c, _log_sigmoid(g_c))
    pre_r = _dot_nt(wgt_ref[...], xn) + bcol_ref[...]
    g_r = GATE_CAP * jnp.tanh(pre_r / GATE_CAP)
    row = lax.broadcasted_iota(jnp.int32, g_r.shape, 0)
    grow_ref[...] = jnp.where(row < M_HEADS, g_r, _log_sigmoid(g_r))


def _mlstm_gates(xn, w_g, b_g, tm=512):
    T, D = xn.shape
    ng = 2 * M_HEADS
    tm = min(tm, T)
    wg = jnp.zeros((D, LANES), F32).at[:, :ng].set(w_g).astype(BF16)
    wgt = w_g.T.astype(BF16)
    brow = jnp.zeros((1, LANES), F32).at[0, :ng].set(b_g)
    bcol = b_g.reshape(ng, 1).astype(F32)
    return pl.pallas_call(
        _gates_kernel,
        grid=(T // tm,),
        in_specs=[pl.BlockSpec((tm, D), lambda i: (i, 0)),
                  pl.BlockSpec((D, LANES), lambda i: (0, 0)),
                  pl.BlockSpec((ng, D), lambda i: (0, 0)),
                  pl.BlockSpec((1, LANES), lambda i: (0, 0)),
                  pl.BlockSpec((ng, 1), lambda i: (0, 0))],
        out_specs=[pl.BlockSpec((tm, LANES), lambda i: (i, 0)),
                   pl.BlockSpec((ng, tm), lambda i: (0, i))],
        out_shape=[jax.ShapeDtypeStruct((T, LANES), F32),
                   jax.ShapeDtypeStruct((ng, T), F32)],
        compiler_params=_params("parallel"),
        name="mlstm_gates",
    )(xn, wg, wgt, brow, bcol)


def _mlstm_kernel(q_ref, k_ref, v_ref, o_ref, gc_ref, gr_ref, out_ref, c_ref, m_ref, *, dk, dv):
    L = q_ref.shape[0]

    @pl.when(pl.program_id(1) == 0)
    def _():
        c_ref[...] = jnp.zeros_like(c_ref)
        m_ref[...] = jnp.zeros_like(m_ref)

    row = lax.broadcasted_iota(jnp.int32, (L, L), 0)
    col = lax.broadcasted_iota(jnp.int32, (L, L), 1)
    causal = col <= row
    tril = jnp.where(causal, 1.0, 0.0).astype(BF16)
    triu = jnp.where(row <= col, 1.0, 0.0).astype(BF16)

    gc = gc_ref[...]
    gr = gr_ref[...]
    c1, c2, c3 = _split3(gc)
    b_col_all = _dot(tril, c1) + _dot(tril, c2) + _dot(tril, c3)
    r1, r2, r3 = _split3(gr)
    b_row_all = _dot(r1, triu) + _dot(r2, triu) + _dot(r3, triu)

    ones_blk = jnp.where(lax.broadcasted_iota(jnp.int32, (L, LANES), 1) == 0, 1.0, 0.0).astype(BF16)

    for h in range(M_HEADS):
        li_row = gr[h:h + 1, :]
        li_col = gc[:, h:h + 1]
        b_row = b_row_all[M_HEADS + h:M_HEADS + h + 1, :]
        b_col = b_col_all[:, M_HEADS + h:M_HEADS + h + 1]
        m_prev = m_ref[h]

        log_d = jnp.where(causal, b_col + (li_row - b_row), -jnp.inf)
        log_inter = b_col + m_prev
        m_t = jnp.maximum(log_inter, jnp.max(log_d, axis=1, keepdims=True))
        w_intra = jnp.exp(log_d - m_t)
        w_inter = jnp.exp(log_inter - m_t)

        qh = q_ref[:, h * dk:(h + 1) * dk]
        kh = k_ref[:, h * dk:(h + 1) * dk]
        v_aug = jnp.concatenate([v_ref[:, h * dv:(h + 1) * dv], ones_blk], axis=1)
        c_old = c_ref[h]

        s = (_dot_nt(qh, kh) * w_intra).astype(BF16)
        qs = (qh.astype(F32) * w_inter).astype(BF16)
        numden = _dot(qs, c_old.astype(BF16)) + _dot(s, v_aug)
        num = numden[:, 0:dv]
        den = numden[:, dv:dv + 1]
        inv = 1.0 / jnp.maximum(jnp.abs(den), jnp.exp(-m_t))
        ssq = jnp.sum(num * num, axis=1, keepdims=True)
        scale = inv * lax.rsqrt(inv * inv * ssq * (1.0 / dv) + EPS)
        gate_o = jax.nn.sigmoid(o_ref[:, h * dv:(h + 1) * dv].astype(F32))
        out_ref[:, h * dv:(h + 1) * dv] = (gate_o * (num * scale)).astype(out_ref.dtype)

        b_last = b_col[L - 1:L, :]
        log_w = b_last - b_col + li_col
        m_new = jnp.maximum(b_last + m_prev, jnp.max(log_w, axis=0, keepdims=True))
        wk = jnp.exp(log_w - m_new) * kh.astype(F32)
        decay = jnp.exp(b_last + m_prev - m_new)
        c_ref[h] = decay * c_old + _dot(wk.T.astype(BF16), v_aug)
        m_ref[h] = m_new


def _mlstm(proj, gcol, grow, *, B, S, D):
    T = B * S
    dv = D // M_HEADS
    dk = dv // 2
    qk = M_HEADS * dk
    L = min(M_CHUNK, S)
    nc = S // L
    rb = lambda b, c: b * nc + c
    assert D % qk == 0
    return pl.pallas_call(
        functools.partial(_mlstm_kernel, dk=dk, dv=dv),
        grid=(B, nc),
        in_specs=[pl.BlockSpec((L, qk), lambda b, c: (rb(b, c), 0)),
                  pl.BlockSpec((L, qk), lambda b, c: (rb(b, c), 1)),
                  pl.BlockSpec((L, D), lambda b, c: (rb(b, c), (2 * qk) // D)),
                  pl.BlockSpec((L, D), lambda b, c: (rb(b, c), (2 * qk) // D + 1)),
                  pl.BlockSpec((L, LANES), lambda b, c: (rb(b, c), 0)),
                  pl.BlockSpec((2 * M_HEADS, L), lambda b, c: (0, rb(b, c)))],
        out_specs=pl.BlockSpec((L, D), lambda b, c: (rb(b, c), 0)),
        out_shape=jax.ShapeDtypeStruct((T, D), BF16),
        scratch_shapes=[pltpu.VMEM((M_HEADS, dk, dv + LANES), F32),
                        pltpu.VMEM((M_HEADS, 1, 1), F32)],
        compiler_params=_params("parallel", "arbitrary"),
        name="mlstm",
    )(proj, proj, proj, proj, gcol, grow)


def _compress_kernel(a_ref, pos_ref, w1_ref, w2k_ref, w2vt_ref, gain_ref, ok_ref, ovt_ref):
    half = a_ref.shape[2]
    ncp = a_ref.shape[1]

    def hidden(j):
        a = a_ref[j].astype(F32)
        lo = (a + pos_ref[j, 0:1, :]).astype(BF16)
        hi = (a + pos_ref[j, 1:2, :]).astype(BF16)
        u = _dot(lo, w1_ref[j, 0:half, :])
        v = _dot(hi, w1_ref[j, half:2 * half, :])
        hid = u + pltpu.roll(v, shift=ncp - 1, axis=0)
        c0 = float(np.sqrt(2.0 / np.pi))
        cdf = 0.5 * (1.0 + jnp.tanh(c0 * (hid + 0.044715 * (hid * hid * hid))))
        return (hid * cdf).astype(BF16)

    yk = _dot(hidden(0), w2k_ref[...])
    rstd = lax.rsqrt(jnp.mean(yk * yk, axis=-1, keepdims=True) + EPS)
    ok_ref[...] = (yk * rstd * gain_ref[...]).astype(ok_ref.dtype)
    ovt_ref[...] = _dot_nt(w2vt_ref[...], hidden(1)).astype(ovt_ref.dtype)


def _compress(a2, pos2, w1, w2k, w2vt, gain):
    _, n, ncp, half = a2.shape
    hid = w1.shape[2]
    return pl.pallas_call(
        _compress_kernel,
        grid=(n,),
        in_specs=[pl.BlockSpec((2, None, ncp, half), lambda i: (0, i, 0, 0)),
                  pl.BlockSpec((2, 2, half), lambda i: (0, 0, 0)),
                  pl.BlockSpec((2, 2 * half, hid), lambda i: (0, 0, 0)),
                  pl.BlockSpec((hid, HEAD_DIM), lambda i: (0, 0)),
                  pl.BlockSpec((HEAD_DIM, hid), lambda i: (0, 0)),
                  pl.BlockSpec((1, HEAD_DIM), lambda i: (0, 0))],
        out_specs=[pl.BlockSpec((None, ncp, HEAD_DIM), lambda i: (i, 0, 0)),
                   pl.BlockSpec((None, HEAD_DIM, ncp), lambda i: (i, 0, 0))],
        out_shape=[jax.ShapeDtypeStruct((n, ncp, HEAD_DIM), BF16),
                   jax.ShapeDtypeStruct((n, HEAD_DIM, ncp), BF16)],
        compiler_params=_params("parallel"),
        name="nsa_compress",
    )(a2, pos2, w1, w2k, w2vt, gain)


def _nsa_kernel(q_ref, kc_ref, vct_ref, ks_ref, vst_ref, kw_ref, vwt_ref, gate_ref, mt_ref, o_ref,
                qa_ref, out_acc, acc_ref, m_ref, l_ref, s_ref, *, tq, hpg, rank_w):
    qi = pl.program_id(2)
    q0 = pl.multiple_of(qi * tq, tq)
    hd = HEAD_DIM
    t_lane = q0 + lax.broadcasted_iota(jnp.int32, (1, tq), 1)

    def gate(br, h):
        r = br * hpg + h
        return gate_ref[r:r + 1, :]

    def q_head(h):
        return q_ref[:, h * hd:(h + 1) * hd]

    ncp = kc_ref.shape[0]
    c_sub = lax.broadcasted_iota(jnp.int32, (ncp, 1), 0)
    cmp_valid = (c_sub * CMP_STRIDE + (CMP_BLOCK - 1)) <= t_lane
    kc = kc_ref[...]
    vct = vct_ref[...]
    psum = jnp.zeros((ncp, tq), F32)
    for h in range(hpg):
        s = jnp.where(cmp_valid, _dot_nt(kc, q_head(h)), NEG_BIG)
        e = jnp.exp2(s - jnp.max(s, axis=0, keepdims=True))
        p = jnp.where(cmp_valid, e * (1.0 / jnp.sum(e, axis=0, keepdims=True)), 0.0)
        psum = psum + p
        out_acc[h] = gate(0, h) * _dot(vct, p.astype(BF16))

    mt = mt_ref[...]
    n_sb = mt.shape[0]
    p1, p2, p3 = _split3(psum)
    p_blk = _dot(mt, p1) + _dot(mt, p2) + _dot(mt, p3)
    blk = lax.broadcasted_iota(jnp.int32, (n_sb, 1), 0)
    cur = t_lane // SEL_BLOCK
    forced = jnp.logical_or(blk == cur, blk == 0)
    val = jnp.where(forced, jnp.inf, jnp.where(blk <= cur, p_blk, -jnp.inf))

    sub = lax.broadcasted_iota(jnp.int32, (SUBLANES, 1), 0)
    n_rb = n_sb // SUBLANES
    top = float(min(N_SEL, n_sb))
    bias_cols = []
    for c in range(tq // rank_w):
        v = val[:, c * rank_w:(c + 1) * rank_w]
        rows = [v[rb * SUBLANES:(rb + 1) * SUBLANES, :] for rb in range(n_rb)]
        cnt = [jnp.zeros((SUBLANES, rank_w), F32) for _ in range(n_rb)]
        for i in range(n_sb):
            vi = v[i:i + 1, :]
            for rb in range(n_rb):
                lo = rb * SUBLANES
                ge = lambda: jnp.where(vi >= rows[rb], 1.0, 0.0)
                gt = lambda: jnp.where(vi > rows[rb], 1.0, 0.0)
                if lo > i:
                    beats = ge()
                elif lo + SUBLANES - 1 <= i:
                    beats = gt()
                else:
                    beats = jnp.where(sub > (i - lo), ge(), gt())
                cnt[rb] = cnt[rb] + beats
        cnt = jnp.concatenate(cnt, axis=0)
        cur_c = cur[:, c * rank_w:(c + 1) * rank_w]
        picked = jnp.where(cnt < top, jnp.where(blk <= cur_c, 0.0, MASK_VAL), MASK_VAL)
        bias_cols.append(picked)
    sel_bias = jnp.concatenate(bias_cols, axis=1)
    sel_bias = jnp.concatenate([sel_bias, jnp.zeros((LANES - n_sb, tq), F32)], axis=0)
    sb_t = sel_bias.T.astype(BF16)
    for h in range(hpg):
        qa_ref[h, :, 0:hd] = q_head(h)
        qa_ref[h, :, hd:2 * hd] = sb_t

    def flash_reset():
        m_ref[...] = jnp.full(m_ref.shape, NEG_BIG, F32)
        l_ref[...] = jnp.zeros_like(l_ref)
        acc_ref[...] = jnp.zeros_like(acc_ref)

    def flash_step(k_t, vt_t, qsel, mask):
        tile_max = []
        for h in range(hpg):
            s = _dot_nt(k_t, qsel(h))
            if mask is not None:
                s = jnp.where(mask, s, MASK_VAL)
            s_ref[h] = s
            tile_max.append(jnp.max(s, axis=0, keepdims=True))
        for h in range(hpg):
            m_old = m_ref[h]
            m_new = jnp.maximum(m_old, tile_max[h])
            alpha = jnp.exp2(m_old - m_new)
            p = jnp.exp2(s_ref[h] - m_new)
            l_ref[h] = alpha * l_ref[h] + jnp.sum(p, axis=0, keepdims=True)
            acc_ref[h] = alpha * acc_ref[h] + _dot(vt_t, p.astype(BF16))
            m_ref[h] = m_new

    def flash_out(h):
        return acc_ref[h] * (1.0 / l_ref[h])

    tk = tq
    k_sub = lax.broadcasted_iota(jnp.int32, (tk, 1), 0)
    q_lane = lax.broadcasted_iota(jnp.int32, (1, tq), 1)
    causal = k_sub <= q_lane
    lane_j = lax.broadcasted_iota(jnp.int32, (1, LANES), 1)
    key_blk = k_sub // SEL_BLOCK

    def ks_aug(k0):
        onehot = jnp.where(lane_j == key_blk + k0 // SEL_BLOCK, 1.0, 0.0).astype(BF16)
        return jnp.concatenate([ks_ref[pl.ds(k0, tk), :], onehot], axis=1)

    qa_head = lambda h: qa_ref[h]

    flash_reset()

    def sel_body(kt, carry):
        k0 = pl.multiple_of(kt * tk, tk)
        flash_step(ks_aug(k0), vst_ref[:, pl.ds(k0, tk)], qa_head, None)
        return carry

    lax.fori_loop(0, qi, sel_body, 0)
    flash_step(ks_aug(q0), vst_ref[:, pl.ds(q0, tk)], qa_head, causal)
    for h in range(hpg):
        out_acc[h] = out_acc[h] + gate(1, h) * flash_out(h)

    flash_reset()
    flash_step(kw_ref[pl.ds(q0, tk), :], vwt_ref[:, pl.ds(q0, tk)], q_head, causal)
    d = 1
    while d * tq - (tk - 1) < WINDOW:
        all_valid = d * tq + (tq - 1) < WINDOW
        mask = None if all_valid else (q_lane + d * tq - k_sub) < WINDOW

        @pl.when(qi >= d)
        def _(d=d, mask=mask):
            k0 = pl.multiple_of(q0 - d * tq, tk)
            flash_step(kw_ref[pl.ds(k0, tk), :], vwt_ref[:, pl.ds(k0, tk)], q_head, mask)
        d += 1

    for h in range(hpg):
        o = out_acc[h] + gate(2, h) * flash_out(h)
        o_ref[:, h * hd:(h + 1) * hd] = o.T.astype(o_ref.dtype)


def _cmp_to_sel_t(ncp, n_cmp, n_sb):
    c0 = np.arange(ncp)[None, :] * CMP_STRIDE
    s0 = np.arange(n_sb)[:, None] * SEL_BLOCK
    ov = np.minimum(c0 + CMP_BLOCK, s0 + SEL_BLOCK) - np.maximum(c0, s0)
    w = np.maximum(ov, 0) / CMP_BLOCK
    w = w * (np.arange(ncp)[None, :] < n_cmp)
    return jnp.asarray(w, dtype=BF16)


def _nsa_attention(qhat, k_cmp, v_cmp_t, k_tok, v_t, gates_t, *, B, S, D, tq=512):
    T = B * S
    hpg = N_HEADS // N_KV
    gw = hpg * HEAD_DIM
    tq = min(tq, S)
    nq = S // tq
    ncp = k_cmp.shape[1]
    n_cmp = (S - CMP_BLOCK) // CMP_STRIDE + 1
    n_sb = S // SEL_BLOCK
    assert n_sb % SUBLANES == 0 and n_sb <= LANES
    mt = _cmp_to_sel_t(ncp, n_cmp, n_sb)
    grow = gates_t.shape[0] // N_KV
    return pl.pallas_call(
        functools.partial(_nsa_kernel, tq=tq, hpg=hpg, rank_w=min(256, tq)),
        grid=(B, N_KV, nq),
        in_specs=[pl.BlockSpec((tq, gw), lambda b, g, i: (b * nq + i, g)),
                  pl.BlockSpec((None, ncp, HEAD_DIM), lambda b, g, i: (b * N_KV + g, 0, 0)),
                  pl.BlockSpec((None, HEAD_DIM, ncp), lambda b, g, i: (b * N_KV + g, 0, 0)),
                  pl.BlockSpec((S, HEAD_DIM), lambda b, g, i: (b, 2 * N_KV + g)),
                  pl.BlockSpec((HEAD_DIM, S), lambda b, g, i: (g, b)),
                  pl.BlockSpec((S, HEAD_DIM), lambda b, g, i: (b, 3 * N_KV + g)),
                  pl.BlockSpec((HEAD_DIM, S), lambda b, g, i: (N_KV + g, b)),
                  pl.BlockSpec((grow, tq), lambda b, g, i: (g, b * nq + i)),
                  pl.BlockSpec((n_sb, ncp), lambda b, g, i: (0, 0))],
        out_specs=pl.BlockSpec((tq, gw), lambda b, g, i: (b * nq + i, g)),
        out_shape=jax.ShapeDtypeStruct((T, D), BF16),
        scratch_shapes=[pltpu.VMEM((hpg, tq, 2 * HEAD_DIM), BF16),
                        pltpu.VMEM((hpg, HEAD_DIM, tq), F32),
                        pltpu.VMEM((hpg, HEAD_DIM, tq), F32),
                        pltpu.VMEM((hpg, 1, tq), F32),
                        pltpu.VMEM((hpg, 1, tq), F32),
                        pltpu.VMEM((hpg, tq, tq), F32)],
        compiler_params=_params("parallel", "parallel", "parallel"),
        name="nsa_attention",
    )(qhat, k_cmp, v_cmp_t, k_tok, v_t, k_tok, v_t, gates_t, mt)


def _mlp(x, g, w_up, w_down, layer):
    xn = _rmsnorm(x, g)
    h = _matmul(xn, w_up, layer=layer, epilogue="relu2", name="mlp_up")
    return _matmul(h, w_down, layer=layer, epilogue="resid", extra=x, out_dtype=F32, tm=2048, tn=1024, tk=2048,
                   name="mlp_down")


def kernel(x, attn_norm_g, mlp_norm_g, m_w_in, m_b_gate, m_head_g, m_w_out, kv_norm_g, w_kv, k_norm_g,
           cmp_pos, cmp_w1, cmp_w2, n_w_qg, q_norm_g, n_w_out, mlp_w_up, mlp_w_down):
    B, S, D = x.shape
    T = B * S
    xf = x.reshape(T, D)
    dv = D // M_HEADS
    dk = dv // 2
    qk = M_HEADS * dk
    nq = 2 * qk + 2 * D
    kv_dim = N_KV * HEAD_DIM
    hpg = N_HEADS // N_KV
    k_fold = (HEAD_DIM ** -0.5) * LOG2E

    xn = _rmsnorm(xf, attn_norm_g[0])
    proj = _matmul(xn, jnp.swapaxes(m_w_in, 1, 2), layer=0, nt=True, n_out=nq, epilogue="mlstm_in", tn=min(512, qk),
                   cfg=dict(m_qk=qk, k_scale=float(dk ** -0.5)), name="mlstm_in_proj")
    gcol, grow = _mlstm_gates(xn, m_w_in[0, :, nq:], m_b_gate[0])
    hg = _mlstm(proj, gcol, grow, B=B, S=S, D=D)
    xf = _matmul(hg, m_w_out, layer=0, epilogue="resid", extra=xf, gain=m_head_g[0], gain_on="w", out_dtype=F32,
                 name="mlstm_out_proj")
    xf = _mlp(xf, mlp_norm_g[0], mlp_w_up, mlp_w_down, 0)

    qg = q_norm_g[0]
    ones = jnp.ones((kv_dim,), F32)
    kv_gain = jnp.concatenate([
        ones, ones,
        jnp.tile(k_norm_g[1] * qg[1] * k_fold, N_KV),
        jnp.tile(k_norm_g[2] * qg[2] * k_fold, N_KV)]).reshape(1, 4 * kv_dim)
    xh = _rmsnorm(xf, jnp.ones((D,), F32))
    k_tok, a2 = _matmul(xh, w_kv, n_out=4 * kv_dim, col_map=lambda j: j + j // 3, epilogue="kvnorm", extra=kv_gain,
                        gain=kv_norm_g, gain_on="w", tn=kv_dim, cfg=dict(kv_dim=kv_dim, norm_slots=(2, 3), seq=S),
                        name="nsa_kv_proj")
    w_vt = jnp.concatenate([w_kv[:, 3 * kv_dim:4 * kv_dim], w_kv[:, 5 * kv_dim:6 * kv_dim]], axis=1).T
    v_t = _matmul(w_vt, xh, nt=True, gain=kv_norm_g, gain_on="a", tm=kv_dim, tn=1024,
                  name="nsa_vt_proj")
    ncp = S // CMP_STRIDE
    a2 = a2.reshape(2, B * N_KV, ncp, CMP_STRIDE * HEAD_DIM)
    pos2 = cmp_pos.reshape(2, 2, CMP_STRIDE * HEAD_DIM).astype(F32)
    cmp_gain = (k_norm_g[0] * qg[0] * k_fold).reshape(1, HEAD_DIM)
    k_cmp, v_cmp_t = _compress(a2, pos2, cmp_w1.astype(BF16), cmp_w2[0].astype(BF16), cmp_w2[1].T.astype(BF16),
                               cmp_gain)

    w_qg = n_w_qg[0]
    qhat = _matmul(xh, jnp.swapaxes(n_w_qg, 1, 2), layer=0, nt=True, n_out=D, epilogue="headnorm",
                   gain=attn_norm_g[1], gain_on="w", name="nsa_q_proj")
    grow_n = -(-N_BRANCH * hpg // SUBLANES) * SUBLANES
    w_gate = w_qg[:, D:].reshape(D, N_BRANCH, N_KV, hpg).transpose(2, 1, 3, 0).reshape(N_KV, N_BRANCH * hpg, D)
    w_gate = jnp.zeros((N_KV, grow_n, D), F32).at[:, :N_BRANCH * hpg].set(w_gate).reshape(N_KV * grow_n, D)
    gates_t = _matmul(w_gate, xh, nt=True, epilogue="sigmoid", gain=attn_norm_g[1], gain_on="a", out_dtype=F32,
                      tn=1024, name="nsa_gate_proj")
    o = _nsa_attention(qhat, k_cmp, v_cmp_t, k_tok, v_t, gates_t, B=B, S=S, D=D)
    xf = _matmul(o, n_w_out, layer=0, epilogue="resid", extra=xf, out_dtype=F32, name="nsa_out_proj")
    xf = _mlp(xf, mlp_norm_g[1], mlp_w_up, mlp_w_down, 1)
    return xf.reshape(B, S, D)
```

```python
import functools

import numpy as np
import jax
import jax.numpy as jnp
from jax import lax
from jax.experimental import pallas as pl
from jax.experimental.pallas import tpu as pltpu

F32 = jnp.float32
BF16 = jnp.bfloat16

EPS = 1e-6
NEG_BIG = -1e30
MASK_VAL = -(2.0 ** 100)
LOG2E = 1.4426950408889634

M_HEADS = 8
GATE_CAP = 15.0
M_CHUNK = 256

N_HEADS = 32
N_KV = 4
HEAD_DIM = 128
N_BRANCH = 3
CMP_BLOCK = 32
CMP_STRIDE = 16
SEL_BLOCK = 64
N_SEL = 16
WINDOW = 512

LANES = 128
SUBLANES = 8
VMEM_LIMIT = 56 * 1024 * 1024


def _params(*sem):
    return pltpu.CompilerParams(dimension_semantics=sem, vmem_limit_bytes=VMEM_LIMIT)


def _dot(a, b):
    return jnp.dot(a, b, preferred_element_type=F32)


def _dot_nt(a, b):
    return lax.dot_general(a, b, (((1,), (1,)), ((), ())), preferred_element_type=F32)


def _split3(x):
    x1 = x.astype(BF16)
    r1 = x - x1.astype(F32)
    x2 = r1.astype(BF16)
    r2 = r1 - x2.astype(F32)
    return x1, x2, r2.astype(BF16)


def _rmsnorm_kernel(x_ref, g_ref, o_ref):
    x = x_ref[...]
    ms = jnp.mean(x * x, axis=-1, keepdims=True)
    o_ref[...] = (x * lax.rsqrt(ms + EPS) * g_ref[...]).astype(o_ref.dtype)


def _rmsnorm(x, g, tm=256):
    T, D = x.shape
    tm = min(tm, T)
    return pl.pallas_call(
        _rmsnorm_kernel,
        grid=(T // tm,),
        in_specs=[pl.BlockSpec((tm, D), lambda i: (i, 0)),
                  pl.BlockSpec((1, D), lambda i: (0, 0))],
        out_specs=pl.BlockSpec((tm, D), lambda i: (i, 0)),
        out_shape=jax.ShapeDtypeStruct((T, D), BF16),
        compiler_params=_params("parallel"),
        name="rmsnorm",
    )(x, g.reshape(1, D).astype(F32))


def _group_rstd(y):
    out = []
    for g in range(y.shape[1] // HEAD_DIM):
        yg = y[:, g * HEAD_DIM:(g + 1) * HEAD_DIM]
        out.append(lax.rsqrt(jnp.mean(yg * yg, axis=-1, keepdims=True) + EPS))
    return out


def _mm_kernel(*refs, epilogue, nk, tn, nt, gain_on, cfg):
    refs = list(refs)
    a_ref, w_ref = refs[:2]
    n_in = 2 + (gain_on is not None) + (epilogue in ("resid", "kvnorm"))
    g_ref = refs[2] if gain_on is not None else None
    e_ref = refs[n_in - 1] if epilogue in ("resid", "kvnorm") else None
    o_ref = refs[n_in]
    j = pl.program_id(1)
    k = pl.program_id(2)

    def product():
        a = a_ref[...]
        w = w_ref[...]
        if gain_on == "a":
            a = a * g_ref[...]
        elif gain_on == "w" and nt:
            w = w * g_ref[...]
        elif gain_on == "w":
            g = g_ref[...]
            w = jnp.concatenate([w[:, c * LANES:(c + 1) * LANES] * g for c in range(w.shape[1] // LANES)], axis=1)
        a = a.astype(BF16)
        w = w.astype(BF16)
        return _dot_nt(a, w) if nt else _dot(a, w)

    if nk > 1:
        assert epilogue == "resid" and o_ref.dtype == F32 and gain_on is None and not nt

        half = o_ref.shape[0] // 2

        @pl.when(k == 0)
        def _():
            o_ref[0:half, :] = e_ref[...]
            o_ref[half:, :] = jnp.zeros((half, o_ref.shape[1]), F32)

        @pl.when(k == nk // 2)
        def _():
            o_ref[half:, :] += e_ref[...]

        w = w_ref[...].astype(BF16)
        for r in range(2):
            rows = slice(r * half, (r + 1) * half)
            o_ref[rows, :] += _dot(a_ref[rows, :], w)
        return

    part = product()

    def finish(y):
        if epilogue == "none":
            o_ref[...] = y.astype(o_ref.dtype)
        elif epilogue == "relu2":
            r = jnp.maximum(y, 0.0)
            o_ref[...] = (r * r).astype(o_ref.dtype)
        elif epilogue == "sigmoid":
            o_ref[...] = jax.nn.sigmoid(y).astype(o_ref.dtype)
        elif epilogue == "resid":
            o_ref[...] = (e_ref[...] + y).astype(o_ref.dtype)
        elif epilogue == "headnorm":
            rstd = _group_rstd(y)
            for g, r in enumerate(rstd):
                sl = slice(g * HEAD_DIM, (g + 1) * HEAD_DIM)
                o_ref[:, sl] = (y[:, sl] * r).astype(o_ref.dtype)
        elif epilogue == "kvnorm":
            slot = (j * tn) // cfg["kv_dim"]
            is_norm = functools.reduce(jnp.logical_or, [slot == s for s in cfg["norm_slots"]])
            rstd = _group_rstd(y)
            gain = e_ref[...]
            for g, r in enumerate(rstd):
                sl = slice(g * HEAD_DIM, (g + 1) * HEAD_DIM)
                mult = jnp.where(is_norm, r, 1.0)
                o_ref[g] = (y[:, sl] * mult * gain[:, sl]).astype(o_ref.dtype)

            blk_ref, y_scr = refs[n_in + 1], refs[n_in + 2]
            rows = y.shape[0] // CMP_STRIDE

            @pl.when(slot < 2)
            def _():
                for g in range(y.shape[1] // HEAD_DIM):
                    y_scr[g] = y[:, g * HEAD_DIM:(g + 1) * HEAD_DIM]
                    for r in range(CMP_STRIDE):
                        piece = y_scr[g, pl.ds(r, rows, stride=CMP_STRIDE), :]
                        blk_ref[g, :, r * HEAD_DIM:(r + 1) * HEAD_DIM] = piece.astype(blk_ref.dtype)
        elif epilogue == "mlstm_in":
            col0 = j * tn
            qk = cfg["m_qk"]
            is_k = jnp.logical_and(col0 >= qk, col0 < 2 * qk)
            o_ref[...] = (y * jnp.where(is_k, cfg["k_scale"], 1.0)).astype(o_ref.dtype)
        else:
            raise ValueError(epilogue)

    finish(part)


def _matmul(a, w, *, layer=None, nt=False, n_out=None, col_map=None, epilogue="none", extra=None, gain=None,
            gain_on=None, out_dtype=BF16, tm=2048, tn=512, tk=4096, cfg=None, name="matmul"):
    M, K = a.shape
    wshape = w.shape[1:] if layer is not None else w.shape
    N = n_out if n_out is not None else (wshape[0] if nt else wshape[1])
    tm, tn, tk = min(tm, M), min(tn, N), min(tk, K)
    assert M % tm == 0 and N % tn == 0 and K % tk == 0
    nk = K // tk
    cm = col_map if col_map is not None else (lambda j: j)
    if layer is not None and nt:
        w_spec = pl.BlockSpec((None, tn, tk), lambda i, j, k: (layer, cm(j), k))
    elif layer is not None:
        w_spec = pl.BlockSpec((None, tk, tn), lambda i, j, k: (layer, k, cm(j)))
    elif nt:
        w_spec = pl.BlockSpec((tn, tk), lambda i, j, k: (cm(j), k))
    else:
        w_spec = pl.BlockSpec((tk, tn), lambda i, j, k: (k, cm(j)))
    a_mode = dict(pipeline_mode=pl.Buffered(1)) if (nk == 1 and N // tn > 1) else {}
    in_specs = [pl.BlockSpec((tm, tk), lambda i, j, k: (i, k), **a_mode), w_spec]
    args = [a, w]
    if gain_on is not None:
        if gain_on == "w" and not nt:
            in_specs.append(pl.BlockSpec((tk, LANES), lambda i, j, k: (k, 0)))
            args.append(jnp.broadcast_to(gain.astype(F32).reshape(K, 1), (K, LANES)))
        else:
            in_specs.append(pl.BlockSpec((1, tk), lambda i, j, k: (0, k)))
            args.append(gain.astype(F32).reshape(1, K))
    if epilogue == "resid":
        if nk > 1:
            assert nk % 2 == 0 and tm % (2 * SUBLANES) == 0
            in_specs.append(pl.BlockSpec((tm // 2, tn), lambda i, j, k: (2 * i + k // (nk // 2), j)))
        else:
            in_specs.append(pl.BlockSpec((tm, tn), lambda i, j, k: (i, j)))
        args.append(extra)
    elif epilogue == "kvnorm":
        in_specs.append(pl.BlockSpec((1, tn), lambda i, j, k: (0, j)))
        args.append(extra)
    o_mode = dict(pipeline_mode=pl.Buffered(1)) if nk > 1 else {}
    out_specs = pl.BlockSpec((tm, tn), lambda i, j, k: (i, j), **o_mode)
    out_shape = jax.ShapeDtypeStruct((M, N), out_dtype)
    scratch = []
    semantics = ("parallel", "parallel", "arbitrary")
    if epilogue == "kvnorm":
        seq, groups = cfg["seq"], tn // HEAD_DIM
        assert seq % tm == 0 and tm % CMP_STRIDE == 0 and tn == cfg["kv_dim"]
        per_seq = seq // tm
        out_specs = pl.BlockSpec((groups, tm, HEAD_DIM), lambda i, j, k: (j, i, 0))
        out_shape = jax.ShapeDtypeStruct((N // HEAD_DIM, M, HEAD_DIM), out_dtype)
        out_specs = [out_specs, pl.BlockSpec((None, None, groups, tm // CMP_STRIDE, CMP_STRIDE * HEAD_DIM),
                                             lambda i, j, k: (jnp.minimum(j, 1), i // per_seq, 0, i % per_seq, 0))]
        out_shape = [out_shape, jax.ShapeDtypeStruct((2, M // seq, groups, seq // CMP_STRIDE, CMP_STRIDE * HEAD_DIM),
                                                     out_dtype)]
        scratch = [pltpu.VMEM((groups, tm, HEAD_DIM), F32)]
        semantics = ("parallel", "arbitrary", "arbitrary")
    return pl.pallas_call(
        functools.partial(_mm_kernel, epilogue=epilogue, nk=nk, tn=tn, nt=nt, gain_on=gain_on, cfg=cfg),
        grid=(M // tm, N // tn, nk),
        in_specs=in_specs,
        out_specs=out_specs,
        out_shape=out_shape,
        scratch_shapes=scratch,
        compiler_params=_params(*semantics),
        name=name,
    )(*args)


def _log_sigmoid(x):
    return jnp.minimum(x, 0.0) - jnp.log(1.0 + jnp.exp(-jnp.abs(x)))


def _gates_kernel(xn_ref, wg_ref, wgt_ref, brow_ref, bcol_ref, gcol_ref, grow_ref):
    xn = xn_ref[...]
    pre_c = _dot(xn, wg_ref[...]) + brow_ref[...]
    g_c = GATE_CAP * jnp.tanh(pre_c / GATE_CAP)
    col = lax.broadcasted_iota(jnp.int32, g_c.shape, 1)
    gcol_ref[...] = jnp.where(col < M_HEADS, g_c, _log_sigmoid(g_c))
    pre_r = _dot_nt(wgt_ref[...], xn) + bcol_ref[...]
    g_r = GATE_CAP * jnp.tanh(pre_r / GATE_CAP)
    row = lax.broadcasted_iota(jnp.int32, g_r.shape, 0)
    grow_ref[...] = jnp.where(row < M_HEADS, g_r, _log_sigmoid(g_r))


def _mlstm_gates(xn, w_g, b_g, tm=512):
    T, D = xn.shape
    ng = 2 * M_HEADS
    tm = min(tm, T)
    wg = jnp.zeros((D, LANES), F32).at[:, :ng].set(w_g).astype(BF16)
    wgt = w_g.T.astype(BF16)
    brow = jnp.zeros((1, LANES), F32).at[0, :ng].set(b_g)
    bcol = b_g.reshape(ng, 1).astype(F32)
    return pl.pallas_call(
        _gates_kernel,
        grid=(T // tm,),
        in_specs=[pl.BlockSpec((tm, D), lambda i: (i, 0)),
                  pl.BlockSpec((D, LANES), lambda i: (0, 0)),
                  pl.BlockSpec((ng, D), lambda i: (0, 0)),
                  pl.BlockSpec((1, LANES), lambda i: (0, 0)),
                  pl.BlockSpec((ng, 1), lambda i: (0, 0))],
        out_specs=[pl.BlockSpec((tm, LANES), lambda i: (i, 0)),
                   pl.BlockSpec((ng, tm), lambda i: (0, i))],
        out_shape=[jax.ShapeDtypeStruct((T, LANES), F32),
                   jax.ShapeDtypeStruct((ng, T), F32)],
        compiler_params=_params("parallel"),
        name="mlstm_gates",
    )(xn, wg, wgt, brow, bcol)


def _mlstm_kernel(q_ref, k_ref, v_ref, o_ref, gc_ref, gr_ref, out_ref, c_ref, m_ref, *, dk, dv):
    L = q_ref.shape[0]

    @pl.when(pl.program_id(1) == 0)
    def _():
        c_ref[...] = jnp.zeros_like(c_ref)
        m_ref[...] = jnp.zeros_like(m_ref)

    row = lax.broadcasted_iota(jnp.int32, (L, L), 0)
    col = lax.broadcasted_iota(jnp.int32, (L, L), 1)
    causal = col <= row
    tril = jnp.where(causal, 1.0, 0.0).astype(BF16)
    triu = jnp.where(row <= col, 1.0, 0.0).astype(BF16)

    gc = gc_ref[...]
    gr = gr_ref[...]
    c1, c2, c3 = _split3(gc)
    b_col_all = _dot(tril, c1) + _dot(tril, c2) + _dot(tril, c3)
    r1, r2, r3 = _split3(gr)
    b_row_all = _dot(r1, triu) + _dot(r2, triu) + _dot(r3, triu)

    ones_blk = jnp.where(lax.broadcasted_iota(jnp.int32, (L, LANES), 1) == 0, 1.0, 0.0).astype(BF16)

    for h in range(M_HEADS):
        li_row = gr[h:h + 1, :]
        li_col = gc[:, h:h + 1]
        b_row = b_row_all[M_HEADS + h:M_HEADS + h + 1, :]
        b_col = b_col_all[:, M_HEADS + h:M_HEADS + h + 1]
        m_prev = m_ref[h]

        log_d = jnp.where(causal, b_col + (li_row - b_row), -jnp.inf)
        log_inter = b_col + m_prev
        m_t = jnp.maximum(log_inter, jnp.max(log_d, axis=1, keepdims=True))
        w_intra = jnp.exp(log_d - m_t)
        w_inter = jnp.exp(log_inter - m_t)

        qh = q_ref[:, h * dk:(h + 1) * dk]
        kh = k_ref[:, h * dk:(h + 1) * dk]
        v_aug = jnp.concatenate([v_ref[:, h * dv:(h + 1) * dv], ones_blk], axis=1)
        c_old = c_ref[h]

        s = (_dot_nt(qh, kh) * w_intra).astype(BF16)
        qs = (qh.astype(F32) * w_inter).astype(BF16)
        numden = _dot(qs, c_old.astype(BF16)) + _dot(s, v_aug)
        num = numden[:, 0:dv]
        den = numden[:, dv:dv + 1]
        inv = 1.0 / jnp.maximum(jnp.abs(den), jnp.exp(-m_t))
        ssq = jnp.sum(num * num, axis=1, keepdims=True)
        scale = inv * lax.rsqrt(inv * inv * ssq * (1.0 / dv) + EPS)
        gate_o = jax.nn.sigmoid(o_ref[:, h * dv:(h + 1) * dv].astype(F32))
        out_ref[:, h * dv:(h + 1) * dv] = (gate_o * (num * scale)).astype(out_ref.dtype)

        b_last = b_col[L - 1:L, :]
        log_w = b_last - b_col + li_col
        m_new = jnp.maximum(b_last + m_prev, jnp.max(log_w, axis=0, keepdims=True))
        wk = jnp.exp(log_w - m_new) * kh.astype(F32)
        decay = jnp.exp(b_last + m_prev - m_new)
        c_ref[h] = decay * c_old + _dot(wk.T.astype(BF16), v_aug)
        m_ref[h] = m_new


def _mlstm(proj, gcol, grow, *, B, S, D):
    T = B * S
    dv = D // M_HEADS
    dk = dv // 2
    qk = M_HEADS * dk
    L = min(M_CHUNK, S)
    nc = S // L
    rb = lambda b, c: b * nc + c
    assert D % qk == 0
    return pl.pallas_call(
        functools.partial(_mlstm_kernel, dk=dk, dv=dv),
        grid=(B, nc),
        in_specs=[pl.BlockSpec((L, qk), lambda b, c: (rb(b, c), 0)),
                  pl.BlockSpec((L, qk), lambda b, c: (rb(b, c), 1)),
                  pl.BlockSpec((L, D), lambda b, c: (rb(b, c), (2 * qk) // D)),
                  pl.BlockSpec((L, D), lambda b, c: (rb(b, c), (2 * qk) // D + 1)),
                  pl.BlockSpec((L, LANES), lambda b, c: (rb(b, c), 0)),
                  pl.BlockSpec((2 * M_HEADS, L), lambda b, c: (0, rb(b, c)))],
        out_specs=pl.BlockSpec((L, D), lambda b, c: (rb(b, c), 0)),
        out_shape=jax.ShapeDtypeStruct((T, D), BF16),
        scratch_shapes=[pltpu.VMEM((M_HEADS, dk, dv + LANES), F32),
                        pltpu.VMEM((M_HEADS, 1, 1), F32)],
        compiler_params=_params("parallel", "arbitrary"),
        name="mlstm",
    )(proj, proj, proj, proj, gcol, grow)


def _compress_kernel(a_ref, pos_ref, w1_ref, w2k_ref, w2vt_ref, gain_ref, ok_ref, ovt_ref):
    half = a_ref.shape[2]
    ncp = a_ref.shape[1]

    def hidden(j):
        a = a_ref[j].astype(F32)
        lo = (a + pos_ref[j, 0:1, :]).astype(BF16)
        hi = (a + pos_ref[j, 1:2, :]).astype(BF16)
        u = _dot(lo, w1_ref[j, 0:half, :])
        v = _dot(hi, w1_ref[j, half:2 * half, :])
        hid = u + pltpu.roll(v, shift=ncp - 1, axis=0)
        c0 = float(np.sqrt(2.0 / np.pi))
        cdf = 0.5 * (1.0 + jnp.tanh(c0 * (hid + 0.044715 * (hid * hid * hid))))
        return (hid * cdf).astype(BF16)

    yk = _dot(hidden(0), w2k_ref[...])
    rstd = lax.rsqrt(jnp.mean(yk * yk, axis=-1, keepdims=True) + EPS)
    ok_ref[...] = (yk * rstd * gain_ref[...]).astype(ok_ref.dtype)
    ovt_ref[...] = _dot_nt(w2vt_ref[...], hidden(1)).astype(ovt_ref.dtype)


def _compress(a2, pos2, w1, w2k, w2vt, gain):
    _, n, ncp, half = a2.shape
    hid = w1.shape[2]
    return pl.pallas_call(
        _compress_kernel,
        grid=(n,),
        in_specs=[pl.BlockSpec((2, None, ncp, half), lambda i: (0, i, 0, 0)),
                  pl.BlockSpec((2, 2, half), lambda i: (0, 0, 0)),
                  pl.BlockSpec((2, 2 * half, hid), lambda i: (0, 0, 0)),
                  pl.BlockSpec((hid, HEAD_DIM), lambda i: (0, 0)),
                  pl.BlockSpec((HEAD_DIM, hid), lambda i: (0, 0)),
                  pl.BlockSpec((1, HEAD_DIM), lambda i: (0, 0))],
        out_specs=[pl.BlockSpec((None, ncp, HEAD_DIM), lambda i: (i, 0, 0)),
                   pl.BlockSpec((None, HEAD_DIM, ncp), lambda i: (i, 0, 0))],
        out_shape=[jax.ShapeDtypeStruct((n, ncp, HEAD_DIM), BF16),
                   jax.ShapeDtypeStruct((n, HEAD_DIM, ncp), BF16)],
        compiler_params=_params("parallel"),
        name="nsa_compress",
    )(a2, pos2, w1, w2k, w2vt, gain)


def _nsa_kernel(q_ref, kc_ref, vct_ref, ks_ref, vst_ref, kw_ref, vwt_ref, gate_ref, mt_ref, o_ref,
                qa_ref, out_acc, acc_ref, m_ref, l_ref, s_ref, *, tq, hpg, rank_w):
    qi = pl.program_id(2)
    q0 = pl.multiple_of(qi * tq, tq)
    hd = HEAD_DIM
    t_lane = q0 + lax.broadcasted_iota(jnp.int32, (1, tq), 1)

    def gate(br, h):
        r = br * hpg + h
        return gate_ref[r:r + 1, :]

    def q_head(h):
        return q_ref[:, h * hd:(h + 1) * hd]

    ncp = kc_ref.shape[0]
    c_sub = lax.broadcasted_iota(jnp.int32, (ncp, 1), 0)
    cmp_valid = (c_sub * CMP_STRIDE + (CMP_BLOCK - 1)) <= t_lane
    kc = kc_ref[...]
    vct = vct_ref[...]
    psum = jnp.zeros((ncp, tq), F32)
    for h in range(hpg):
        s = jnp.where(cmp_valid, _dot_nt(kc, q_head(h)), NEG_BIG)
        e = jnp.exp2(s - jnp.max(s, axis=0, keepdims=True))
        p = jnp.where(cmp_valid, e * (1.0 / jnp.sum(e, axis=0, keepdims=True)), 0.0)
        psum = psum + p
        out_acc[h] = gate(0, h) * _dot(vct, p.astype(BF16))

    mt = mt_ref[...]
    n_sb = mt.shape[0]
    p1, p2, p3 = _split3(psum)
    p_blk = _dot(mt, p1) + _dot(mt, p2) + _dot(mt, p3)
    blk = lax.broadcasted_iota(jnp.int32, (n_sb, 1), 0)
    cur = t_lane // SEL_BLOCK
    forced = jnp.logical_or(blk == cur, blk == 0)
    val = jnp.where(forced, jnp.inf, jnp.where(blk <= cur, p_blk, -jnp.inf))

    sub = lax.broadcasted_iota(jnp.int32, (SUBLANES, 1), 0)
    n_rb = n_sb // SUBLANES
    top = float(min(N_SEL, n_sb))
    bias_cols = []
    for c in range(tq // rank_w):
        v = val[:, c * rank_w:(c + 1) * rank_w]
        rows = [v[rb * SUBLANES:(rb + 1) * SUBLANES, :] for rb in range(n_rb)]
        cnt = [jnp.zeros((SUBLANES, rank_w), F32) for _ in range(n_rb)]
        for i in range(n_sb):
            vi = v[i:i + 1, :]
            for rb in range(n_rb):
                lo = rb * SUBLANES
                ge = lambda: jnp.where(vi >= rows[rb], 1.0, 0.0)
                gt = lambda: jnp.where(vi > rows[rb], 1.0, 0.0)
                if lo > i:
                    beats = ge()
                elif lo + SUBLANES - 1 <= i:
                    beats = gt()
                else:
                    beats = jnp.where(sub > (i - lo), ge(), gt())
                cnt[rb] = cnt[rb] + beats
        cnt = jnp.concatenate(cnt, axis=0)
        cur_c = cur[:, c * rank_w:(c + 1) * rank_w]
        picked = jnp.where(cnt < top, jnp.where(blk <= cur_c, 0.0, MASK_VAL), MASK_VAL)
        bias_cols.append(picked)
    sel_bias = jnp.concatenate(bias_cols, axis=1)
    sel_bias = jnp.concatenate([sel_bias, jnp.zeros((LANES - n_sb, tq), F32)], axis=0)
    sb_t = sel_bias.T.astype(BF16)
    for h in range(hpg):
        qa_ref[h, :, 0:hd] = q_head(h)
        qa_ref[h, :, hd:2 * hd] = sb_t

    def flash_reset():
        m_ref[...] = jnp.full(m_ref.shape, NEG_BIG, F32)
        l_ref[...] = jnp.zeros_like(l_ref)
        acc_ref[...] = jnp.zeros_like(acc_ref)

    def flash_step(k_t, vt_t, qsel, mask):
        tile_max = []
        for h in range(hpg):
            s = _dot_nt(k_t, qsel(h))
            if mask is not None:
                s = jnp.where(mask, s, MASK_VAL)
            s_ref[h] = s
            tile_max.append(jnp.max(s, axis=0, keepdims=True))
        for h in range(hpg):
            m_old = m_ref[h]
            m_new = jnp.maximum(m_old, tile_max[h])
            alpha = jnp.exp2(m_old - m_new)
            p = jnp.exp2(s_ref[h] - m_new)
            l_ref[h] = alpha * l_ref[h] + jnp.sum(p, axis=0, keepdims=True)
            acc_ref[h] = alpha * acc_ref[h] + _dot(vt_t, p.astype(BF16))
            m_ref[h] = m_new

    def flash_out(h):
        return acc_ref[h] * (1.0 / l_ref[h])

    tk = tq
    k_sub = lax.broadcasted_iota(jnp.int32, (tk, 1), 0)
    q_lane = lax.broadcasted_iota(jnp.int32, (1, tq), 1)
    causal = k_sub <= q_lane
    lane_j = lax.broadcasted_iota(jnp.int32, (1, LANES), 1)
    key_blk = k_sub // SEL_BLOCK

    def ks_aug(k0):
        onehot = jnp.where(lane_j == key_blk + k0 // SEL_BLOCK, 1.0, 0.0).astype(BF16)
        return jnp.concatenate([ks_ref[pl.ds(k0, tk), :], onehot], axis=1)

    qa_head = lambda h: qa_ref[h]

    flash_reset()

    def sel_body(kt, carry):
        k0 = pl.multiple_of(kt * tk, tk)
        flash_step(ks_aug(k0), vst_ref[:, pl.ds(k0, tk)], qa_head, None)
        return carry

    lax.fori_loop(0, qi, sel_body, 0)
    flash_step(ks_aug(q0), vst_ref[:, pl.ds(q0, tk)], qa_head, causal)
    for h in range(hpg):
        out_acc[h] = out_acc[h] + gate(1, h) * flash_out(h)

    flash_reset()
    flash_step(kw_ref[pl.ds(q0, tk), :], vwt_ref[:, pl.ds(q0, tk)], q_head, causal)
    d = 1
    while d * tq - (tk - 1) < WINDOW:
        all_valid = d * tq + (tq - 1) < WINDOW
        mask = None if all_valid else (q_lane + d * tq - k_sub) < WINDOW

        @pl.when(qi >= d)
        def _(d=d, mask=mask):
            k0 = pl.multiple_of(q0 - d * tq, tk)
            flash_step(kw_ref[pl.ds(k0, tk), :], vwt_ref[:, pl.ds(k0, tk)], q_head, mask)
        d += 1

    for h in range(hpg):
        o = out_acc[h] + gate(2, h) * flash_out(h)
        o_ref[:, h * hd:(h + 1) * hd] = o.T.astype(o_ref.dtype)


def _cmp_to_sel_t(ncp, n_cmp, n_sb):
    c0 = np.arange(ncp)[None, :] * CMP_STRIDE
    s0 = np.arange(n_sb)[:, None] * SEL_BLOCK
    ov = np.minimum(c0 + CMP_BLOCK, s0 + SEL_BLOCK) - np.maximum(c0, s0)
    w = np.maximum(ov, 0) / CMP_BLOCK
    w = w * (np.arange(ncp)[None, :] < n_cmp)
    return jnp.asarray(w, dtype=BF16)


def _nsa_attention(qhat, k_cmp, v_cmp_t, k_tok, v_t, gates_t, *, B, S, D, tq=512):
    T = B * S
    hpg = N_HEADS // N_KV
    gw = hpg * HEAD_DIM
    tq = min(tq, S)
    nq = S // tq
    ncp = k_cmp.shape[1]
    n_cmp = (S - CMP_BLOCK) // CMP_STRIDE + 1
    n_sb = S // SEL_BLOCK
    assert n_sb % SUBLANES == 0 and n_sb <= LANES
    mt = _cmp_to_sel_t(ncp, n_cmp, n_sb)
    grow = gates_t.shape[0] // N_KV
    return pl.pallas_call(
        functools.partial(_nsa_kernel, tq=tq, hpg=hpg, rank_w=min(256, tq)),
        grid=(B, N_KV, nq),
        in_specs=[pl.BlockSpec((tq, gw), lambda b, g, i: (b * nq + i, g)),
                  pl.BlockSpec((None, ncp, HEAD_DIM), lambda b, g, i: (b * N_KV + g, 0, 0)),
                  pl.BlockSpec((None, HEAD_DIM, ncp), lambda b, g, i: (b * N_KV + g, 0, 0)),
                  pl.BlockSpec((None, S, HEAD_DIM), lambda b, g, i: (2 * N_KV + g, b, 0)),
                  pl.BlockSpec((HEAD_DIM, S), lambda b, g, i: (g, b)),
                  pl.BlockSpec((None, S, HEAD_DIM), lambda b, g, i: (3 * N_KV + g, b, 0)),
                  pl.BlockSpec((HEAD_DIM, S), lambda b, g, i: (N_KV + g, b)),
                  pl.BlockSpec((grow, tq), lambda b, g, i: (g, b * nq + i)),
                  pl.BlockSpec((n_sb, ncp), lambda b, g, i: (0, 0))],
        out_specs=pl.BlockSpec((tq, gw), lambda b, g, i: (b * nq + i, g)),
        out_shape=jax.ShapeDtypeStruct((T, D), BF16),
        scratch_shapes=[pltpu.VMEM((hpg, tq, 2 * HEAD_DIM), BF16),
                        pltpu.VMEM((hpg, HEAD_DIM, tq), F32),
                        pltpu.VMEM((hpg, HEAD_DIM, tq), F32),
                        pltpu.VMEM((hpg, 1, tq), F32),
                        pltpu.VMEM((hpg, 1, tq), F32),
                        pltpu.VMEM((hpg, tq, tq), F32)],
        compiler_params=_params("parallel", "parallel", "parallel"),
        name="nsa_attention",
    )(qhat, k_cmp, v_cmp_t, k_tok, v_t, k_tok, v_t, gates_t, mt)


def _mlp(x, g, w_up, w_down, layer):
    xn = _rmsnorm(x, g)
    h = _matmul(xn, w_up, layer=layer, epilogue="relu2", name="mlp_up")
    return _matmul(h, w_down, layer=layer, epilogue="resid", extra=x, out_dtype=F32, tm=2048, tn=1024, tk=2048,
                   name="mlp_down")


def kernel(x, attn_norm_g, mlp_norm_g, m_w_in, m_b_gate, m_head_g, m_w_out, kv_norm_g, w_kv, k_norm_g,
           cmp_pos, cmp_w1, cmp_w2, n_w_qg, q_norm_g, n_w_out, mlp_w_up, mlp_w_down):
    B, S, D = x.shape
    T = B * S
    xf = x.reshape(T, D)
    dv = D // M_HEADS
    dk = dv // 2
    qk = M_HEADS * dk
    nq = 2 * qk + 2 * D
    kv_dim = N_KV * HEAD_DIM
    hpg = N_HEADS // N_KV
    k_fold = (HEAD_DIM ** -0.5) * LOG2E

    xn = _rmsnorm(xf, attn_norm_g[0])
    proj = _matmul(xn, jnp.swapaxes(m_w_in, 1, 2), layer=0, nt=True, n_out=nq, epilogue="mlstm_in", tn=min(512, qk),
                   cfg=dict(m_qk=qk, k_scale=float(dk ** -0.5)), name="mlstm_in_proj")
    gcol, grow = _mlstm_gates(xn, m_w_in[0, :, nq:], m_b_gate[0])
    hg = _mlstm(proj, gcol, grow, B=B, S=S, D=D)
    xf = _matmul(hg, m_w_out, layer=0, epilogue="resid", extra=xf, gain=m_head_g[0], gain_on="w", out_dtype=F32,
                 name="mlstm_out_proj")
    xf = _mlp(xf, mlp_norm_g[0], mlp_w_up, mlp_w_down, 0)

    qg = q_norm_g[0]
    ones = jnp.ones((kv_dim,), F32)
    kv_gain = jnp.concatenate([
        ones, ones,
        jnp.tile(k_norm_g[1] * qg[1] * k_fold, N_KV),
        jnp.tile(k_norm_g[2] * qg[2] * k_fold, N_KV)]).reshape(1, 4 * kv_dim)
    xh = _rmsnorm(xf, jnp.ones((D,), F32))
    k_tok, a2 = _matmul(xh, w_kv, n_out=4 * kv_dim, col_map=lambda j: j + j // 3, epilogue="kvnorm", extra=kv_gain,
                        gain=kv_norm_g, gain_on="w", tn=kv_dim, cfg=dict(kv_dim=kv_dim, norm_slots=(2, 3), seq=S),
                        name="nsa_kv_proj")
    w_vt = jnp.concatenate([w_kv[:, 3 * kv_dim:4 * kv_dim], w_kv[:, 5 * kv_dim:6 * kv_dim]], axis=1).T
    v_t = _matmul(w_vt, xh, nt=True, gain=kv_norm_g, gain_on="a", tm=kv_dim, tn=1024,
                  name="nsa_vt_proj")
    ncp = S // CMP_STRIDE
    a2 = a2.reshape(2, B * N_KV, ncp, CMP_STRIDE * HEAD_DIM)
    pos2 = cmp_pos.reshape(2, 2, CMP_STRIDE * HEAD_DIM).astype(F32)
    cmp_gain = (k_norm_g[0] * qg[0] * k_fold).reshape(1, HEAD_DIM)
    k_cmp, v_cmp_t = _compress(a2, pos2, cmp_w1.astype(BF16), cmp_w2[0].astype(BF16), cmp_w2[1].T.astype(BF16),
                               cmp_gain)

    w_qg = n_w_qg[0]
    qhat = _matmul(xh, jnp.swapaxes(n_w_qg, 1, 2), layer=0, nt=True, n_out=D, epilogue="headnorm",
                   gain=attn_norm_g[1], gain_on="w", name="nsa_q_proj")
    grow_n = -(-N_BRANCH * hpg // SUBLANES) * SUBLANES
    w_gate = w_qg[:, D:].reshape(D, N_BRANCH, N_KV, hpg).transpose(2, 1, 3, 0).reshape(N_KV, N_BRANCH * hpg, D)
    w_gate = jnp.zeros((N_KV, grow_n, D), F32).at[:, :N_BRANCH * hpg].set(w_gate).reshape(N_KV * grow_n, D)
    gates_t = _matmul(w_gate, xh, nt=True, epilogue="sigmoid", gain=attn_norm_g[1], gain_on="a", out_dtype=F32,
                      tn=1024, name="nsa_gate_proj")
    o = _nsa_attention(qhat, k_cmp, v_cmp_t, k_tok, v_t, gates_t, B=B, S=S, D=D)
    xf = _matmul(o, n_w_out, layer=0, epilogue="resid", extra=xf, out_dtype=F32, name="nsa_out_proj")
    xf = _mlp(xf, mlp_norm_g[1], mlp_w_up, mlp_w_down, 1)
    return xf.reshape(B, S, D)
```

```python
import functools

import numpy as np
import jax
import jax.numpy as jnp
from jax import lax
from jax.experimental import pallas as pl
from jax.experimental.pallas import tpu as pltpu

F32 = jnp.float32
BF16 = jnp.bfloat16

EPS = 1e-6
NEG_BIG = -1e30
MASK_VAL = -(2.0 ** 100)
LOG2E = 1.4426950408889634

M_HEADS = 8
GATE_CAP = 15.0
M_CHUNK = 256

N_HEADS = 32
N_KV = 4
HEAD_DIM = 128
N_BRANCH = 3
CMP_BLOCK = 32
CMP_STRIDE = 16
SEL_BLOCK = 64
N_SEL = 16
WINDOW = 512

LANES = 128
SUBLANES = 8
VMEM_LIMIT = 56 * 1024 * 1024


def _params(*sem):
    return pltpu.CompilerParams(dimension_semantics=sem, vmem_limit_bytes=VMEM_LIMIT)


def _dot(a, b):
    return jnp.dot(a, b, preferred_element_type=F32)


def _dot_nt(a, b):
    return lax.dot_general(a, b, (((1,), (1,)), ((), ())), preferred_element_type=F32)


def _split3(x):
    x1 = x.astype(BF16)
    r1 = x - x1.astype(F32)
    x2 = r1.astype(BF16)
    r2 = r1 - x2.astype(F32)
    return x1, x2, r2.astype(BF16)


def _rmsnorm_kernel(x_ref, g_ref, o_ref):
    x = x_ref[...]
    ms = jnp.mean(x * x, axis=-1, keepdims=True)
    o_ref[...] = (x * lax.rsqrt(ms + EPS) * g_ref[...]).astype(o_ref.dtype)


def _rmsnorm(x, g, tm=256):
    T, D = x.shape
    tm = min(tm, T)
    return pl.pallas_call(
        _rmsnorm_kernel,
        grid=(T // tm,),
        in_specs=[pl.BlockSpec((tm, D), lambda i: (i, 0)),
                  pl.BlockSpec((1, D), lambda i: (0, 0))],
        out_specs=pl.BlockSpec((tm, D), lambda i: (i, 0)),
        out_shape=jax.ShapeDtypeStruct((T, D), BF16),
        compiler_params=_params("parallel"),
        name="rmsnorm",
    )(x, g.reshape(1, D).astype(F32))


def _group_rstd(y):
    out = []
    for g in range(y.shape[1] // HEAD_DIM):
        yg = y[:, g * HEAD_DIM:(g + 1) * HEAD_DIM]
        out.append(lax.rsqrt(jnp.mean(yg * yg, axis=-1, keepdims=True) + EPS))
    return out


def _mm_kernel(*refs, epilogue, nk, tn, nt, gain_on, cfg):
    refs = list(refs)
    a_ref, w_ref = refs[:2]
    n_in = 2 + (gain_on is not None) + (epilogue in ("resid", "kvnorm"))
    g_ref = refs[2] if gain_on is not None else None
    e_ref = refs[n_in - 1] if epilogue in ("resid", "kvnorm") else None
    o_ref = refs[n_in]
    j = pl.program_id(1)
    k = pl.program_id(2)

    def product():
        a = a_ref[...]
        w = w_ref[...]
        if gain_on == "a":
            a = a * g_ref[...]
        elif gain_on == "w" and nt:
            w = w * g_ref[...]
        elif gain_on == "w":
            g = g_ref[...]
            w = jnp.concatenate([w[:, c * LANES:(c + 1) * LANES] * g for c in range(w.shape[1] // LANES)], axis=1)
        a = a.astype(BF16)
        w = w.astype(BF16)
        return _dot_nt(a, w) if nt else _dot(a, w)

    if nk > 1:
        assert epilogue == "resid" and o_ref.dtype == F32 and gain_on is None and not nt

        half = o_ref.shape[0] // 2

        @pl.when(k == 0)
        def _():
            o_ref[0:half, :] = e_ref[...]
            o_ref[half:, :] = jnp.zeros((half, o_ref.shape[1]), F32)

        @pl.when(k == nk // 2)
        def _():
            o_ref[half:, :] += e_ref[...]

        w = w_ref[...].astype(BF16)
        for r in range(2):
            rows = slice(r * half, (r + 1) * half)
            o_ref[rows, :] += _dot(a_ref[rows, :], w)
        return

    part = product()

    def finish(y):
        if epilogue == "none":
            o_ref[...] = y.astype(o_ref.dtype)
        elif epilogue == "relu2":
            r = jnp.maximum(y, 0.0)
            o_ref[...] = (r * r).astype(o_ref.dtype)
        elif epilogue == "sigmoid":
            o_ref[...] = jax.nn.sigmoid(y).astype(o_ref.dtype)
        elif epilogue == "resid":
            o_ref[...] = (e_ref[...] + y).astype(o_ref.dtype)
        elif epilogue == "headnorm":
            rstd = _group_rstd(y)
            for g, r in enumerate(rstd):
                sl = slice(g * HEAD_DIM, (g + 1) * HEAD_DIM)
                o_ref[:, sl] = (y[:, sl] * r).astype(o_ref.dtype)
        elif epilogue == "kvnorm":
            slot = (j * tn) // cfg["kv_dim"]
            is_norm = functools.reduce(jnp.logical_or, [slot == s for s in cfg["norm_slots"]])
            rstd = _group_rstd(y)
            gain = e_ref[...]
            for g, r in enumerate(rstd):
                sl = slice(g * HEAD_DIM, (g + 1) * HEAD_DIM)
                mult = jnp.where(is_norm, r, 1.0)
                o_ref[:, sl] = (y[:, sl] * mult * gain[:, sl]).astype(o_ref.dtype)

            blk_ref, y_scr = refs[n_in + 1], refs[n_in + 2]
            rows = y.shape[0] // CMP_STRIDE

            @pl.when(slot < 2)
            def _():
                for g in range(y.shape[1] // HEAD_DIM):
                    y_scr[g] = y[:, g * HEAD_DIM:(g + 1) * HEAD_DIM]
                    for r in range(CMP_STRIDE):
                        piece = y_scr[g, pl.ds(r, rows, stride=CMP_STRIDE), :]
                        blk_ref[g, :, r * HEAD_DIM:(r + 1) * HEAD_DIM] = piece.astype(blk_ref.dtype)
        elif epilogue == "mlstm_in":
            col0 = j * tn
            qk = cfg["m_qk"]
            is_k = jnp.logical_and(col0 >= qk, col0 < 2 * qk)
            o_ref[...] = (y * jnp.where(is_k, cfg["k_scale"], 1.0)).astype(o_ref.dtype)
        else:
            raise ValueError(epilogue)

    finish(part)


def _matmul(a, w, *, layer=None, nt=False, n_out=None, col_map=None, epilogue="none", extra=None, gain=None,
            gain_on=None, out_dtype=BF16, tm=2048, tn=512, tk=4096, cfg=None, name="matmul"):
    M, K = a.shape
    wshape = w.shape[1:] if layer is not None else w.shape
    N = n_out if n_out is not None else (wshape[0] if nt else wshape[1])
    tm, tn, tk = min(tm, M), min(tn, N), min(tk, K)
    assert M % tm == 0 and N % tn == 0 and K % tk == 0
    nk = K // tk
    cm = col_map if col_map is not None else (lambda j: j)
    if layer is not None and nt:
        w_spec = pl.BlockSpec((None, tn, tk), lambda i, j, k: (layer, cm(j), k))
    elif layer is not None:
        w_spec = pl.BlockSpec((None, tk, tn), lambda i, j, k: (layer, k, cm(j)))
    elif nt:
        w_spec = pl.BlockSpec((tn, tk), lambda i, j, k: (cm(j), k))
    else:
        w_spec = pl.BlockSpec((tk, tn), lambda i, j, k: (k, cm(j)))
    a_mode = dict(pipeline_mode=pl.Buffered(1)) if (nk == 1 and N // tn > 1) else {}
    in_specs = [pl.BlockSpec((tm, tk), lambda i, j, k: (i, k), **a_mode), w_spec]
    args = [a, w]
    if gain_on is not None:
        if gain_on == "w" and not nt:
            in_specs.append(pl.BlockSpec((tk, LANES), lambda i, j, k: (k, 0)))
            args.append(jnp.broadcast_to(gain.astype(F32).reshape(K, 1), (K, LANES)))
        else:
            in_specs.append(pl.BlockSpec((1, tk), lambda i, j, k: (0, k)))
            args.append(gain.astype(F32).reshape(1, K))
    if epilogue == "resid":
        if nk > 1:
            assert nk % 2 == 0 and tm % (2 * SUBLANES) == 0
            in_specs.append(pl.BlockSpec((tm // 2, tn), lambda i, j, k: (2 * i + k // (nk // 2), j)))
        else:
            in_specs.append(pl.BlockSpec((tm, tn), lambda i, j, k: (i, j)))
        args.append(extra)
    elif epilogue == "kvnorm":
        in_specs.append(pl.BlockSpec((1, tn), lambda i, j, k: (0, j)))
        args.append(extra)
    o_mode = dict(pipeline_mode=pl.Buffered(1)) if nk > 1 else {}
    out_specs = pl.BlockSpec((tm, tn), lambda i, j, k: (i, j), **o_mode)
    out_shape = jax.ShapeDtypeStruct((M, N), out_dtype)
    scratch = []
    semantics = ("parallel", "parallel", "arbitrary")
    if epilogue == "kvnorm":
        seq, groups = cfg["seq"], tn // HEAD_DIM
        assert seq % tm == 0 and tm % CMP_STRIDE == 0 and tn == cfg["kv_dim"]
        per_seq = seq // tm
        out_specs = [out_specs, pl.BlockSpec((None, None, groups, tm // CMP_STRIDE, CMP_STRIDE * HEAD_DIM),
                                             lambda i, j, k: (jnp.minimum(j, 1), i // per_seq, 0, i % per_seq, 0))]
        out_shape = [out_shape, jax.ShapeDtypeStruct((2, M // seq, groups, seq // CMP_STRIDE, CMP_STRIDE * HEAD_DIM),
                                                     out_dtype)]
        scratch = [pltpu.VMEM((groups, tm, HEAD_DIM), F32)]
        semantics = ("parallel", "arbitrary", "arbitrary")
    return pl.pallas_call(
        functools.partial(_mm_kernel, epilogue=epilogue, nk=nk, tn=tn, nt=nt, gain_on=gain_on, cfg=cfg),
        grid=(M // tm, N // tn, nk),
        in_specs=in_specs,
        out_specs=out_specs,
        out_shape=out_shape,
        scratch_shapes=scratch,
        compiler_params=_params(*semantics),
        name=name,
    )(*args)


def _log_sigmoid(x):
    return jnp.minimum(x, 0.0) - jnp.log(1.0 + jnp.exp(-jnp.abs(x)))


def _gates_kernel(xn_ref, wg_ref, wgt_ref, brow_ref, bcol_ref, gcol_ref, grow_ref):
    xn = xn_ref[...]
    pre_c = _dot(xn, wg_ref[...]) + brow_ref[...]
    g_c = GATE_CAP * jnp.tanh(pre_c / GATE_CAP)
    col = lax.broadcasted_iota(jnp.int32, g_c.shape, 1)
    gcol_ref[...] = jnp.where(col < M_HEADS, g_c, _log_sigmoid(g_c))
    pre_r = _dot_nt(wgt_ref[...], xn) + bcol_ref[...]
    g_r = GATE_CAP * jnp.tanh(pre_r / GATE_CAP)
    row = lax.broadcasted_iota(jnp.int32, g_r.shape, 0)
    grow_ref[...] = jnp.where(row < M_HEADS, g_r, _log_sigmoid(g_r))


def _mlstm_gates(xn, w_g, b_g, tm=512):
    T, D = xn.shape
    ng = 2 * M_HEADS
    tm = min(tm, T)
    wg = jnp.zeros((D, LANES), F32).at[:, :ng].set(w_g).astype(BF16)
    wgt = w_g.T.astype(BF16)
    brow = jnp.zeros((1, LANES), F32).at[0, :ng].set(b_g)
    bcol = b_g.reshape(ng, 1).astype(F32)
    return pl.pallas_call(
        _gates_kernel,
        grid=(T // tm,),
        in_specs=[pl.BlockSpec((tm, D), lambda i: (i, 0)),
                  pl.BlockSpec((D, LANES), lambda i: (0, 0)),
                  pl.BlockSpec((ng, D), lambda i: (0, 0)),
                  pl.BlockSpec((1, LANES), lambda i: (0, 0)),
                  pl.BlockSpec((ng, 1), lambda i: (0, 0))],
        out_specs=[pl.BlockSpec((tm, LANES), lambda i: (i, 0)),
                   pl.BlockSpec((ng, tm), lambda i: (0, i))],
        out_shape=[jax.ShapeDtypeStruct((T, LANES), F32),
                   jax.ShapeDtypeStruct((ng, T), F32)],
        compiler_params=_params("parallel"),
        name="mlstm_gates",
    )(xn, wg, wgt, brow, bcol)


def _mlstm_kernel(q_ref, k_ref, v_ref, o_ref, gc_ref, gr_ref, out_ref, c_ref, m_ref, *, dk, dv):
    L = q_ref.shape[0]

    @pl.when(pl.program_id(1) == 0)
    def _():
        c_ref[...] = jnp.zeros_like(c_ref)
        m_ref[...] = jnp.zeros_like(m_ref)

    row = lax.broadcasted_iota(jnp.int32, (L, L), 0)
    col = lax.broadcasted_iota(jnp.int32, (L, L), 1)
    causal = col <= row
    tril = jnp.where(causal, 1.0, 0.0).astype(BF16)
    triu = jnp.where(row <= col, 1.0, 0.0).astype(BF16)

    gc = gc_ref[...]
    gr = gr_ref[...]
    c1, c2, c3 = _split3(gc)
    b_col_all = _dot(tril, c1) + _dot(tril, c2) + _dot(tril, c3)
    r1, r2, r3 = _split3(gr)
    b_row_all = _dot(r1, triu) + _dot(r2, triu) + _dot(r3, triu)

    ones_blk = jnp.where(lax.broadcasted_iota(jnp.int32, (L, LANES), 1) == 0, 1.0, 0.0).astype(BF16)

    for h in range(M_HEADS):
        li_row = gr[h:h + 1, :]
        li_col = gc[:, h:h + 1]
        b_row = b_row_all[M_HEADS + h:M_HEADS + h + 1, :]
        b_col = b_col_all[:, M_HEADS + h:M_HEADS + h + 1]
        m_prev = m_ref[h]

        log_d = jnp.where(causal, b_col + (li_row - b_row), -jnp.inf)
        log_inter = b_col + m_prev
        m_t = jnp.maximum(log_inter, jnp.max(log_d, axis=1, keepdims=True))
        w_intra = jnp.exp(log_d - m_t)
        w_inter = jnp.exp(log_inter - m_t)

        qh = q_ref[:, h * dk:(h + 1) * dk]
        kh = k_ref[:, h * dk:(h + 1) * dk]
        v_aug = jnp.concatenate([v_ref[:, h * dv:(h + 1) * dv], ones_blk], axis=1)
        c_old = c_ref[h]

        s = (_dot_nt(qh, kh) * w_intra).astype(BF16)
        qs = (qh.astype(F32) * w_inter).astype(BF16)
        numden = _dot(qs, c_old.astype(BF16)) + _dot(s, v_aug)
        num = numden[:, 0:dv]
        den = numden[:, dv:dv + 1]
        inv = 1.0 / jnp.maximum(jnp.abs(den), jnp.exp(-m_t))
        ssq = jnp.sum(num * num, axis=1, keepdims=True)
        scale = inv * lax.rsqrt(inv * inv * ssq * (1.0 / dv) + EPS)
        gate_o = jax.nn.sigmoid(o_ref[:, h * dv:(h + 1) * dv].astype(F32))
        out_ref[:, h * dv:(h + 1) * dv] = (gate_o * (num * scale)).astype(out_ref.dtype)

        b_last = b_col[L - 1:L, :]
        log_w = b_last - b_col + li_col
        m_new = jnp.maximum(b_last + m_prev, jnp.max(log_w, axis=0, keepdims=True))
        wk = jnp.exp(log_w - m_new) * kh.astype(F32)
        decay = jnp.exp(b_last + m_prev - m_new)
        c_ref[h] = decay * c_old + _dot(wk.T.astype(BF16), v_aug)
        m_ref[h] = m_new


def _mlstm(proj, gcol, grow, *, B, S, D):
    T = B * S
    dv = D // M_HEADS
    dk = dv // 2
    qk = M_HEADS * dk
    L = min(M_CHUNK, S)
    nc = S // L
    rb = lambda b, c: b * nc + c
    assert D % qk == 0
    return pl.pallas_call(
        functools.partial(_mlstm_kernel, dk=dk, dv=dv),
        grid=(B, nc),
        in_specs=[pl.BlockSpec((L, qk), lambda b, c: (rb(b, c), 0)),
                  pl.BlockSpec((L, qk), lambda b, c: (rb(b, c), 1)),
                  pl.BlockSpec((L, D), lambda b, c: (rb(b, c), (2 * qk) // D)),
                  pl.BlockSpec((L, D), lambda b, c: (rb(b, c), (2 * qk) // D + 1)),
                  pl.BlockSpec((L, LANES), lambda b, c: (rb(b, c), 0)),
                  pl.BlockSpec((2 * M_HEADS, L), lambda b, c: (0, rb(b, c)))],
        out_specs=pl.BlockSpec((L, D), lambda b, c: (rb(b, c), 0)),
        out_shape=jax.ShapeDtypeStruct((T, D), BF16),
        scratch_shapes=[pltpu.VMEM((M_HEADS, dk, dv + LANES), F32),
                        pltpu.VMEM((M_HEADS, 1, 1), F32)],
        compiler_params=_params("parallel", "arbitrary"),
        name="mlstm",
    )(proj, proj, proj, proj, gcol, grow)


def _compress_kernel(a_ref, pos_ref, w1_ref, w2k_ref, w2vt_ref, gain_ref, ok_ref, ovt_ref):
    half = a_ref.shape[2]
    ncp = a_ref.shape[1]

    def hidden(j):
        a = a_ref[j].astype(F32)
        lo = (a + pos_ref[j, 0:1, :]).astype(BF16)
        hi = (a + pos_ref[j, 1:2, :]).astype(BF16)
        u = _dot(lo, w1_ref[j, 0:half, :])
        v = _dot(hi, w1_ref[j, half:2 * half, :])
        hid = u + pltpu.roll(v, shift=ncp - 1, axis=0)
        c0 = float(np.sqrt(2.0 / np.pi))
        cdf = 0.5 * (1.0 + jnp.tanh(c0 * (hid + 0.044715 * (hid * hid * hid))))
        return (hid * cdf).astype(BF16)

    yk = _dot(hidden(0), w2k_ref[...])
    rstd = lax.rsqrt(jnp.mean(yk * yk, axis=-1, keepdims=True) + EPS)
    ok_ref[...] = (yk * rstd * gain_ref[...]).astype(ok_ref.dtype)
    ovt_ref[...] = _dot_nt(w2vt_ref[...], hidden(1)).astype(ovt_ref.dtype)


def _compress(a2, pos2, w1, w2k, w2vt, gain):
    _, n, ncp, half = a2.shape
    hid = w1.shape[2]
    return pl.pallas_call(
        _compress_kernel,
        grid=(n,),
        in_specs=[pl.BlockSpec((2, None, ncp, half), lambda i: (0, i, 0, 0)),
                  pl.BlockSpec((2, 2, half), lambda i: (0, 0, 0)),
                  pl.BlockSpec((2, 2 * half, hid), lambda i: (0, 0, 0)),
                  pl.BlockSpec((hid, HEAD_DIM), lambda i: (0, 0)),
                  pl.BlockSpec((HEAD_DIM, hid), lambda i: (0, 0)),
                  pl.BlockSpec((1, HEAD_DIM), lambda i: (0, 0))],
        out_specs=[pl.BlockSpec((None, ncp, HEAD_DIM), lambda i: (i, 0, 0)),
                   pl.BlockSpec((None, HEAD_DIM, ncp), lambda i: (i, 0, 0))],
        out_shape=[jax.ShapeDtypeStruct((n, ncp, HEAD_DIM), BF16),
                   jax.ShapeDtypeStruct((n, HEAD_DIM, ncp), BF16)],
        compiler_params=_params("parallel"),
        name="nsa_compress",
    )(a2, pos2, w1, w2k, w2vt, gain)


def _nsa_kernel(q_ref, kc_ref, vct_ref, ks_ref, vst_ref, kw_ref, vwt_ref, gate_ref, mt_ref, o_ref,
                qa_ref, out_acc, acc_ref, m_ref, s_ref, *, tq, hpg, rank_w):
    qi = pl.program_id(2)
    q0 = pl.multiple_of(qi * tq, tq)
    hd = HEAD_DIM
    t_lane = q0 + lax.broadcasted_iota(jnp.int32, (1, tq), 1)

    def gate(br, h):
        r = br * hpg + h
        return gate_ref[r:r + 1, :]

    def q_head(h):
        return q_ref[:, h * hd:(h + 1) * hd]

    ncp = kc_ref.shape[0]
    c_sub = lax.broadcasted_iota(jnp.int32, (ncp, 1), 0)
    cmp_valid = (c_sub * CMP_STRIDE + (CMP_BLOCK - 1)) <= t_lane
    kc = kc_ref[...]
    vct = vct_ref[...]
    psum = jnp.zeros((ncp, tq), F32)
    for h in range(hpg):
        s = jnp.where(cmp_valid, _dot_nt(kc, q_head(h)), NEG_BIG)
        e = jnp.exp2(s - jnp.max(s, axis=0, keepdims=True))
        p = jnp.where(cmp_valid, e * (1.0 / jnp.sum(e, axis=0, keepdims=True)), 0.0)
        psum = psum + p
        out_acc[h] = gate(0, h) * _dot(vct, p.astype(BF16))

    mt = mt_ref[...]
    n_sb = mt.shape[0]
    p1, p2, p3 = _split3(psum)
    p_blk = _dot(mt, p1) + _dot(mt, p2) + _dot(mt, p3)
    blk = lax.broadcasted_iota(jnp.int32, (n_sb, 1), 0)
    cur = t_lane // SEL_BLOCK
    forced = jnp.logical_or(blk == cur, blk == 0)
    val = jnp.where(forced, jnp.inf, jnp.where(blk <= cur, p_blk, -jnp.inf))

    sub = lax.broadcasted_iota(jnp.int32, (SUBLANES, 1), 0)
    n_rb = n_sb // SUBLANES
    top = float(min(N_SEL, n_sb))
    bias_cols = []
    for c in range(tq // rank_w):
        v = val[:, c * rank_w:(c + 1) * rank_w]
        rows = [v[rb * SUBLANES:(rb + 1) * SUBLANES, :] for rb in range(n_rb)]
        cnt = [jnp.zeros((SUBLANES, rank_w), F32) for _ in range(n_rb)]
        for i in range(n_sb):
            vi = v[i:i + 1, :]
            for rb in range(n_rb):
                lo = rb * SUBLANES
                ge = lambda: jnp.where(vi >= rows[rb], 1.0, 0.0)
                gt = lambda: jnp.where(vi > rows[rb], 1.0, 0.0)
                if lo > i:
                    beats = ge()
                elif lo + SUBLANES - 1 <= i:
                    beats = gt()
                else:
                    beats = jnp.where(sub > (i - lo), ge(), gt())
                cnt[rb] = cnt[rb] + beats
        cnt = jnp.concatenate(cnt, axis=0)
        cur_c = cur[:, c * rank_w:(c + 1) * rank_w]
        picked = jnp.where(cnt < top, jnp.where(blk <= cur_c, 0.0, MASK_VAL), MASK_VAL)
        bias_cols.append(picked)
    sel_bias = jnp.concatenate(bias_cols, axis=1)
    sel_bias = jnp.concatenate([sel_bias, jnp.zeros((LANES - n_sb, tq), F32)], axis=0)
    sb_t = sel_bias.T.astype(BF16)
    for h in range(hpg):
        qa_ref[h, :, 0:hd] = q_head(h)
        qa_ref[h, :, hd:2 * hd] = sb_t

    def flash_reset():
        m_ref[...] = jnp.full(m_ref.shape, NEG_BIG, F32)
        acc_ref[...] = jnp.zeros_like(acc_ref)

    def flash_step(k_t, vt_t, qsel, mask):
        tile_max = []
        for h in range(hpg):
            s = _dot_nt(k_t, qsel(h))
            if mask is not None:
                s = jnp.where(mask, s, MASK_VAL)
            s_ref[h] = s.astype(BF16)
            tile_max.append(jnp.max(s, axis=0, keepdims=True))
        vt_aug = jnp.concatenate([vt_t, jnp.ones((SUBLANES, vt_t.shape[1]), BF16)], axis=0)
        for h in range(hpg):
            m_old = m_ref[h]
            m_new = jnp.maximum(m_old, tile_max[h]).astype(BF16)
            m_new32 = m_new.astype(F32)
            alpha = jnp.exp2(m_old - m_new32)
            p = jnp.exp2(s_ref[h] - m_new)
            acc_ref[h] = alpha * acc_ref[h] + _dot(vt_aug, p)
            m_ref[h] = m_new32

    def flash_out(h):
        a = acc_ref[h]
        return a[0:hd] * (1.0 / a[hd:hd + 1])

    tk = tq
    k_sub = lax.broadcasted_iota(jnp.int32, (tk, 1), 0)
    q_lane = lax.broadcasted_iota(jnp.int32, (1, tq), 1)
    causal = k_sub <= q_lane
    lane_j = lax.broadcasted_iota(jnp.int32, (1, LANES), 1)
    key_blk = k_sub // SEL_BLOCK

    def ks_aug(k0):
        onehot = jnp.where(lane_j == key_blk + k0 // SEL_BLOCK, 1.0, 0.0).astype(BF16)
        return jnp.concatenate([ks_ref[pl.ds(k0, tk), :], onehot], axis=1)

    qa_head = lambda h: qa_ref[h]

    flash_reset()

    def sel_body(kt, carry):
        k0 = pl.multiple_of(kt * tk, tk)
        flash_step(ks_aug(k0), vst_ref[:, pl.ds(k0, tk)], qa_head, None)
        return carry

    lax.fori_loop(0, qi, sel_body, 0)
    flash_step(ks_aug(q0), vst_ref[:, pl.ds(q0, tk)], qa_head, causal)
    for h in range(hpg):
        out_acc[h] = out_acc[h] + gate(1, h) * flash_out(h)

    flash_reset()
    flash_step(kw_ref[pl.ds(q0, tk), :], vwt_ref[:, pl.ds(q0, tk)], q_head, causal)
    d = 1
    while d * tq - (tk - 1) < WINDOW:
        all_valid = d * tq + (tq - 1) < WINDOW
        mask = None if all_valid else (q_lane + d * tq - k_sub) < WINDOW

        @pl.when(qi >= d)
        def _(d=d, mask=mask):
            k0 = pl.multiple_of(q0 - d * tq, tk)
            flash_step(kw_ref[pl.ds(k0, tk), :], vwt_ref[:, pl.ds(k0, tk)], q_head, mask)
        d += 1

    for h in range(hpg):
        o = out_acc[h] + gate(2, h) * flash_out(h)
        o_ref[:, h * hd:(h + 1) * hd] = o.T.astype(o_ref.dtype)


def _cmp_to_sel_t(ncp, n_cmp, n_sb):
    c0 = np.arange(ncp)[None, :] * CMP_STRIDE
    s0 = np.arange(n_sb)[:, None] * SEL_BLOCK
    ov = np.minimum(c0 + CMP_BLOCK, s0 + SEL_BLOCK) - np.maximum(c0, s0)
    w = np.maximum(ov, 0) / CMP_BLOCK
    w = w * (np.arange(ncp)[None, :] < n_cmp)
    return jnp.asarray(w, dtype=BF16)


def _nsa_attention(qhat, k_cmp, v_cmp_t, k_tok, v_t, gates_t, *, B, S, D, tq=512):
    T = B * S
    hpg = N_HEADS // N_KV
    gw = hpg * HEAD_DIM
    tq = min(tq, S)
    nq = S // tq
    ncp = k_cmp.shape[1]
    n_cmp = (S - CMP_BLOCK) // CMP_STRIDE + 1
    n_sb = S // SEL_BLOCK
    assert n_sb % SUBLANES == 0 and n_sb <= LANES
    mt = _cmp_to_sel_t(ncp, n_cmp, n_sb)
    grow = gates_t.shape[0] // N_KV
    return pl.pallas_call(
        functools.partial(_nsa_kernel, tq=tq, hpg=hpg, rank_w=min(256, tq)),
        grid=(B, N_KV, nq),
        in_specs=[pl.BlockSpec((tq, gw), lambda b, g, i: (b * nq + i, g)),
                  pl.BlockSpec((None, ncp, HEAD_DIM), lambda b, g, i: (b * N_KV + g, 0, 0)),
                  pl.BlockSpec((None, HEAD_DIM, ncp), lambda b, g, i: (b * N_KV + g, 0, 0)),
                  pl.BlockSpec((S, HEAD_DIM), lambda b, g, i: (b, 2 * N_KV + g)),
                  pl.BlockSpec((HEAD_DIM, S), lambda b, g, i: (g, b)),
                  pl.BlockSpec((S, HEAD_DIM), lambda b, g, i: (b, 3 * N_KV + g)),
                  pl.BlockSpec((HEAD_DIM, S), lambda b, g, i: (N_KV + g, b)),
                  pl.BlockSpec((grow, tq), lambda b, g, i: (g, b * nq + i)),
                  pl.BlockSpec((n_sb, ncp), lambda b, g, i: (0, 0))],
        out_specs=pl.BlockSpec((tq, gw), lambda b, g, i: (b * nq + i, g)),
        out_shape=jax.ShapeDtypeStruct((T, D), BF16),
        scratch_shapes=[pltpu.VMEM((hpg, tq, 2 * HEAD_DIM), BF16),
                        pltpu.VMEM((hpg, HEAD_DIM, tq), F32),
                        pltpu.VMEM((hpg, HEAD_DIM + SUBLANES, tq), F32),
                        pltpu.VMEM((hpg, 1, tq), F32),
                        pltpu.VMEM((hpg, tq, tq), BF16)],
        compiler_params=_params("parallel", "parallel", "parallel"),
        name="nsa_attention",
    )(qhat, k_cmp, v_cmp_t, k_tok, v_t, k_tok, v_t, gates_t, mt)


def _mlp(x, g, w_up, w_down, layer):
    xn = _rmsnorm(x, g)
    h = _matmul(xn, w_up, layer=layer, epilogue="relu2", name="mlp_up")
    return _matmul(h, w_down, layer=layer, epilogue="resid", extra=x, out_dtype=F32, tm=2048, tn=1024, tk=2048,
                   name="mlp_down")


def kernel(x, attn_norm_g, mlp_norm_g, m_w_in, m_b_gate, m_head_g, m_w_out, kv_norm_g, w_kv, k_norm_g,
           cmp_pos, cmp_w1, cmp_w2, n_w_qg, q_norm_g, n_w_out, mlp_w_up, mlp_w_down):
    B, S, D = x.shape
    T = B * S
    xf = x.reshape(T, D)
    dv = D // M_HEADS
    dk = dv // 2
    qk = M_HEADS * dk
    nq = 2 * qk + 2 * D
    kv_dim = N_KV * HEAD_DIM
    hpg = N_HEADS // N_KV
    k_fold = (HEAD_DIM ** -0.5) * LOG2E

    xn = _rmsnorm(xf, attn_norm_g[0])
    proj = _matmul(xn, jnp.swapaxes(m_w_in, 1, 2), layer=0, nt=True, n_out=nq, epilogue="mlstm_in", tn=min(512, qk),
                   cfg=dict(m_qk=qk, k_scale=float(dk ** -0.5)), name="mlstm_in_proj")
    gcol, grow = _mlstm_gates(xn, m_w_in[0, :, nq:], m_b_gate[0])
    hg = _mlstm(proj, gcol, grow, B=B, S=S, D=D)
    xf = _matmul(hg, m_w_out, layer=0, epilogue="resid", extra=xf, gain=m_head_g[0], gain_on="w", out_dtype=F32,
                 name="mlstm_out_proj")
    xf = _mlp(xf, mlp_norm_g[0], mlp_w_up, mlp_w_down, 0)

    qg = q_norm_g[0]
    ones = jnp.ones((kv_dim,), F32)
    kv_gain = jnp.concatenate([
        ones, ones,
        jnp.tile(k_norm_g[1] * qg[1] * k_fold, N_KV),
        jnp.tile(k_norm_g[2] * qg[2] * k_fold, N_KV)]).reshape(1, 4 * kv_dim)
    xh = _rmsnorm(xf, jnp.ones((D,), F32))
    k_tok, a2 = _matmul(xh, w_kv, n_out=4 * kv_dim, col_map=lambda j: j + j // 3, epilogue="kvnorm", extra=kv_gain,
                        gain=kv_norm_g, gain_on="w", tn=kv_dim, cfg=dict(kv_dim=kv_dim, norm_slots=(2, 3), seq=S),
                        name="nsa_kv_proj")
    w_vt = jnp.concatenate([w_kv[:, 3 * kv_dim:4 * kv_dim], w_kv[:, 5 * kv_dim:6 * kv_dim]], axis=1).T
    v_t = _matmul(w_vt, xh, nt=True, gain=kv_norm_g, gain_on="a", tm=kv_dim, tn=1024,
                  name="nsa_vt_proj")
    ncp = S // CMP_STRIDE
    a2 = a2.reshape(2, B * N_KV, ncp, CMP_STRIDE * HEAD_DIM)
    pos2 = cmp_pos.reshape(2, 2, CMP_STRIDE * HEAD_DIM).astype(F32)
    cmp_gain = (k_norm_g[0] * qg[0] * k_fold).reshape(1, HEAD_DIM)
    k_cmp, v_cmp_t = _compress(a2, pos2, cmp_w1.astype(BF16), cmp_w2[0].astype(BF16), cmp_w2[1].T.astype(BF16),
                               cmp_gain)

    w_qg = n_w_qg[0]
    qhat = _matmul(xh, jnp.swapaxes(n_w_qg, 1, 2), layer=0, nt=True, n_out=D, epilogue="headnorm",
                   gain=attn_norm_g[1], gain_on="w", name="nsa_q_proj")
    grow_n = -(-N_BRANCH * hpg // SUBLANES) * SUBLANES
    w_gate = w_qg[:, D:].reshape(D, N_BRANCH, N_KV, hpg).transpose(2, 1, 3, 0).reshape(N_KV, N_BRANCH * hpg, D)
    w_gate = jnp.zeros((N_KV, grow_n, D), F32).at[:, :N_BRANCH * hpg].set(w_gate).reshape(N_KV * grow_n, D)
    gates_t = _matmul(w_gate, xh, nt=True, epilogue="sigmoid", gain=attn_norm_g[1], gain_on="a", out_dtype=F32,
                      tn=1024, name="nsa_gate_proj")
    o = _nsa_attention(qhat, k_cmp, v_cmp_t, k_tok, v_t, gates_t, B=B, S=S, D=D)
    xf = _matmul(o, n_w_out, layer=0, epilogue="resid", extra=xf, out_dtype=F32, name="nsa_out_proj")
    xf = _mlp(xf, mlp_norm_g[1], mlp_w_up, mlp_w_down, 1)
    return xf.reshape(B, S, D)
```
